```python
import math
import jax, jax.numpy as jnp
from jax import lax
import numpy as np

D_MODEL = 1024
BATCH = 8
SEQ = 16384
DEPTH = 2

CHUNK = 64
SSM_GROUP = 16
SSM_GROUPS = 16
D_SSM = SSM_GROUP * SSM_GROUPS
SSM_STATE = 64
SB_HEADS = 8
SB_HEAD_DIM = 64
D_SB = SB_HEADS * SB_HEAD_DIM
D_CONV = 256
CONV_WIDTH = 3
D_FF = 4 * D_MODEL
QBLOCK = 128
EPS = 1e-6
DT_MIN = 1e-3
DT_MAX = 1e-1
N_BRANCH = 3
D_IN = D_SSM + 3 * D_SB + 3 * D_CONV
SPLITS = [D_SSM, D_SSM + D_SB, D_SSM + 2 * D_SB, D_SSM + 3 * D_SB,
          D_SSM + 3 * D_SB + D_CONV, D_SSM + 3 * D_SB + 2 * D_CONV]

kernel_name = "hybrid_s5_stickbreak_shortconv_block"


def rmsnorm(x, g):
    x32 = x.astype(jnp.float32)
    y = x32 * lax.rsqrt(jnp.mean(x32 * x32, axis=-1, keepdims=True) + EPS) * g.astype(jnp.float32)
    return y.astype(x.dtype)


def _complex_linear_combine(e1, e2):
    a1r, a1i, b1r, b1i = e1
    a2r, a2i, b2r, b2i = e2
    ar = a1r * a2r - a1i * a2i
    ai = a1r * a2i + a1i * a2r
    br = a2r * b1r - a2i * b1i + b2r
    bi = a2r * b1i + a2i * b1r + b2i
    return (ar, ai, br, bi)


def s5_branch(u, a_re, a_im, log_dt, b_re, b_im, c_re, c_im, d_skip, w_val, w_gate):
    bsz, s, _ = u.shape
    f32 = jnp.float32
    uf = u.astype(f32)
    ug = uf.reshape(bsz, s, SSM_GROUPS, SSM_GROUP)
    dt = jnp.exp(log_dt.astype(f32))[:, None]
    lr = a_re.astype(f32)
    li = a_im.astype(f32)
    mag = jnp.exp(lr * dt)
    ab_re = mag * jnp.cos(li * dt)
    ab_im = mag * jnp.sin(li * dt)
    den = lr * lr + li * li
    xr = ab_re - 1.0
    coef_re = (xr * lr + ab_im * li) / den
    coef_im = (ab_im * lr - xr * li) / den
    br = b_re.astype(f32)
    bi = b_im.astype(f32)
    bb_re = coef_re[..., None] * br - coef_im[..., None] * bi
    bb_im = coef_re[..., None] * bi + coef_im[..., None] * br
    bu_re = jnp.einsum('bsgc,gnc->bsgn', ug, bb_re)
    bu_im = jnp.einsum('bsgc,gnc->bsgn', ug, bb_im)
    a_r = jnp.broadcast_to(ab_re, bu_re.shape)
    a_i = jnp.broadcast_to(ab_im, bu_re.shape)
    _, _, h_re, h_im = lax.associative_scan(_complex_linear_combine, (a_r, a_i, bu_re, bu_im), axis=1)
    y = (jnp.einsum('bsgn,gcn->bsgc', h_re, c_re.astype(f32))
         - jnp.einsum('bsgn,gcn->bsgc', h_im, c_im.astype(f32)))
    y = y.reshape(bsz, s, D_SSM) + d_skip.astype(f32) * uf
    y = jax.nn.gelu(y).astype(u.dtype)
    return (y @ w_val) * jax.nn.sigmoid(y @ w_gate)


def stick_breaking_attention(q, k, v):
    s_len = q.shape[2]
    scale = SB_HEAD_DIM ** -0.5
    outs = []
    for i in range(s_len // QBLOCK):
        t0 = i * QBLOCK
        t1 = t0 + QBLOCK
        kb = k[:, :, :t1]
        vb = v[:, :, :t1]
        z = jnp.einsum('bhtd,bhsd->bhts', q[:, :, t0:t1], kb).astype(jnp.float32) * scale
        tpos = t0 + jnp.arange(QBLOCK)[:, None]
        spos = jnp.arange(t1)[None, :]
        mask = spos < tpos
        log_1m = jnp.where(mask, jax.nn.log_sigmoid(-z), 0.0)
        suffix = lax.cumsum(log_1m, axis=log_1m.ndim - 1, reverse=True) - log_1m
        w = jnp.where(mask, jnp.exp(jax.nn.log_sigmoid(z) + suffix), 0.0)
        outs.append(jnp.einsum('bhts,bhsd->bhtd', w.astype(v.dtype), vb))
    return jnp.concatenate(outs, axis=2)


def short_conv_branch(gate_b, gate_c, x_in, conv_w, w_out):
    z = gate_c * x_in
    conv_filter = conv_w[:, None, :].astype(z.dtype)
    y = lax.conv_general_dilated(z, conv_filter, window_strides=(1,),
                                 padding=[(CONV_WIDTH - 1, 0)],
                                 dimension_numbers=('NWC', 'WIO', 'NWC'),
                                 feature_group_count=D_CONV)
    return (gate_b * y) @ w_out


def _fwd_setup_inputs(seed: int = 0) -> dict:
    key = jax.random.key(seed)
    ks = jax.random.split(key, 24)
    f32 = jnp.float32

    def nrm(k, shape, scale):
        return jax.random.normal(k, shape, f32) * scale

    n_idx = jnp.arange(SSM_STATE, dtype=f32)
    return {
        "x": nrm(ks[0], (BATCH, SEQ, D_MODEL), 1.0),
        "norm_mix_pre": 1.0 + nrm(ks[1], (DEPTH, D_MODEL), 0.02),
        "norm_mix_post": 1.0 + nrm(ks[2], (DEPTH, D_MODEL), 0.02),
        "w_in": nrm(ks[3], (DEPTH, D_MODEL, D_IN), D_MODEL ** -0.5),
        "w_gate": nrm(ks[4], (DEPTH, D_MODEL, N_BRANCH * D_MODEL), D_MODEL ** -0.5),
        "b_gate": nrm(ks[5], (DEPTH, N_BRANCH * D_MODEL), 0.01),
        "ssm_a_re": -0.5 + nrm(ks[6], (DEPTH, SSM_GROUPS, SSM_STATE), 0.01),
        "ssm_a_im": math.pi * n_idx + nrm(ks[7], (DEPTH, SSM_GROUPS, SSM_STATE), 0.01),
        "ssm_log_dt": jax.random.uniform(ks[8], (DEPTH, SSM_GROUPS), f32,
                                         math.log(DT_MIN), math.log(DT_MAX)),
        "ssm_b_re": nrm(ks[9], (DEPTH, SSM_GROUPS, SSM_STATE, SSM_GROUP), (2 * SSM_GROUP) ** -0.5),
        "ssm_b_im": nrm(ks[10], (DEPTH, SSM_GROUPS, SSM_STATE, SSM_GROUP), (2 * SSM_GROUP) ** -0.5),
        "ssm_c_re": nrm(ks[11], (DEPTH, SSM_GROUPS, SSM_GROUP, SSM_STATE), SSM_STATE ** -0.5),
        "ssm_c_im": nrm(ks[12], (DEPTH, SSM_GROUPS, SSM_GROUP, SSM_STATE), SSM_STATE ** -0.5),
        "ssm_d": nrm(ks[13], (DEPTH, D_SSM), 1.0),
        "w_glu_val": nrm(ks[14], (DEPTH, D_SSM, D_MODEL), D_SSM ** -0.5),
        "w_glu_gate": nrm(ks[15], (DEPTH, D_SSM, D_MODEL), D_SSM ** -0.5),
        "w_attn_out": nrm(ks[16], (DEPTH, D_SB, D_MODEL), D_SB ** -0.5),
        "conv_w": nrm(ks[17], (DEPTH, CONV_WIDTH, D_CONV), CONV_WIDTH ** -0.5),
        "w_conv_out": nrm(ks[18], (DEPTH, D_CONV, D_MODEL), D_CONV ** -0.5),
        "w_mix_out": nrm(ks[19], (DEPTH, D_MODEL, D_MODEL), D_MODEL ** -0.5),
        "norm_ffn_pre": 1.0 + nrm(ks[20], (DEPTH, D_MODEL), 0.02),
        "norm_ffn_post": 1.0 + nrm(ks[21], (DEPTH, D_MODEL), 0.02),
        "w_ffn_up": nrm(ks[22], (DEPTH, D_MODEL, D_FF), D_MODEL ** -0.5),
        "w_ffn_down": nrm(ks[23], (DEPTH, D_FF, D_MODEL), D_FF ** -0.5),
    }


def _fwd_reference(x, norm_mix_pre, norm_mix_post, w_in, w_gate, b_gate,
              ssm_a_re, ssm_a_im, ssm_log_dt, ssm_b_re, ssm_b_im, ssm_c_re, ssm_c_im, ssm_d,
              w_glu_val, w_glu_gate, w_attn_out, conv_w, w_conv_out, w_mix_out,
              norm_ffn_pre, norm_ffn_post, w_ffn_up, w_ffn_down):
    bsz, s_len, _ = x.shape

    def heads(t):
        return t.reshape(bsz, s_len, SB_HEADS, SB_HEAD_DIM).transpose(0, 2, 1, 3)

    for l in range(DEPTH):
        h = rmsnorm(x, norm_mix_pre[l])
        p = h @ w_in[l]
        u, q, k, v, cb, cc, cx = jnp.split(p, SPLITS, axis=-1)

        y_ssm = s5_branch(u, ssm_a_re[l], ssm_a_im[l], ssm_log_dt[l], ssm_b_re[l], ssm_b_im[l],
                          ssm_c_re[l], ssm_c_im[l], ssm_d[l], w_glu_val[l], w_glu_gate[l])
        o = stick_breaking_attention(heads(q), heads(k), heads(v))
        y_sb = o.transpose(0, 2, 1, 3).reshape(bsz, s_len, D_SB) @ w_attn_out[l]
        y_conv = short_conv_branch(cb, cc, cx, conv_w[l], w_conv_out[l])

        g_ssm, g_sb, g_conv = jnp.split(jax.nn.sigmoid(h @ w_gate[l] + b_gate[l]), N_BRANCH, axis=-1)
        merged = g_ssm * y_ssm + g_sb * y_sb + g_conv * y_conv
        x = x + rmsnorm(merged @ w_mix_out[l], norm_mix_post[l])

        h = rmsnorm(x, norm_ffn_pre[l])
        f = jnp.square(jax.nn.relu(h @ w_ffn_up[l])) @ w_ffn_down[l]
        x = x + rmsnorm(f, norm_ffn_post[l])
    return x


import jax as _jax
import jax.numpy as _jnp

TWIN_FORMAT = 'train_step'
FWD_PARAMS = ['x', 'norm_mix_pre', 'norm_mix_post', 'w_in', 'w_gate', 'b_gate', 'ssm_a_re', 'ssm_a_im', 'ssm_log_dt', 'ssm_b_re', 'ssm_b_im', 'ssm_c_re', 'ssm_c_im', 'ssm_d', 'w_glu_val', 'w_glu_gate', 'w_attn_out', 'conv_w', 'w_conv_out', 'w_mix_out', 'norm_ffn_pre', 'norm_ffn_post', 'w_ffn_up', 'w_ffn_down']
TWIN_WEIGHTS = ['norm_mix_pre', 'norm_mix_post', 'w_in', 'w_gate', 'b_gate', 'ssm_a_re', 'ssm_a_im', 'ssm_log_dt', 'ssm_b_re', 'ssm_b_im', 'ssm_c_re', 'ssm_c_im', 'ssm_d', 'w_glu_val', 'w_glu_gate', 'w_attn_out', 'conv_w', 'w_conv_out', 'w_mix_out', 'norm_ffn_pre', 'norm_ffn_post', 'w_ffn_up', 'w_ffn_down']
TWIN_DIFF_INPUT = 'x'
TWIN_INPUTS = ['x', 'norm_mix_pre', 'norm_mix_post', 'w_in', 'w_gate', 'b_gate', 'ssm_a_re', 'ssm_a_im', 'ssm_log_dt', 'ssm_b_re', 'ssm_b_im', 'ssm_c_re', 'ssm_c_im', 'ssm_d', 'w_glu_val', 'w_glu_gate', 'w_attn_out', 'conv_w', 'w_conv_out', 'w_mix_out', 'norm_ffn_pre', 'norm_ffn_post', 'w_ffn_up', 'w_ffn_down', 'loss_target', 'm_norm_mix_pre', 'm_norm_mix_post', 'm_w_in', 'm_w_gate', 'm_b_gate', 'm_ssm_a_re', 'm_ssm_a_im', 'm_ssm_log_dt', 'm_ssm_b_re', 'm_ssm_b_im', 'm_ssm_c_re', 'm_ssm_c_im', 'm_ssm_d', 'm_w_glu_val', 'm_w_glu_gate', 'm_w_attn_out', 'm_conv_w', 'm_w_conv_out', 'm_w_mix_out', 'm_norm_ffn_pre', 'm_norm_ffn_post', 'm_w_ffn_up', 'm_w_ffn_down', 'v_norm_mix_pre', 'v_norm_mix_post', 'v_w_in', 'v_w_gate', 'v_b_gate', 'v_ssm_a_re', 'v_ssm_a_im', 'v_ssm_log_dt', 'v_ssm_b_re', 'v_ssm_b_im', 'v_ssm_c_re', 'v_ssm_c_im', 'v_ssm_d', 'v_w_glu_val', 'v_w_glu_gate', 'v_w_attn_out', 'v_conv_w', 'v_w_conv_out', 'v_w_mix_out', 'v_norm_ffn_pre', 'v_norm_ffn_post', 'v_w_ffn_up', 'v_w_ffn_down']
TWIN_OUTPUTS = ['loss', 'grad_x', 'grad_norm_mix_pre', 'grad_norm_mix_post', 'grad_w_in', 'grad_w_gate', 'grad_b_gate', 'grad_ssm_a_re', 'grad_ssm_a_im', 'grad_ssm_log_dt', 'grad_ssm_b_re', 'grad_ssm_b_im', 'grad_ssm_c_re', 'grad_ssm_c_im', 'grad_ssm_d', 'grad_w_glu_val', 'grad_w_glu_gate', 'grad_w_attn_out', 'grad_conv_w', 'grad_w_conv_out', 'grad_w_mix_out', 'grad_norm_ffn_pre', 'grad_norm_ffn_post', 'grad_w_ffn_up', 'grad_w_ffn_down', 'delta_norm_mix_pre', 'delta_norm_mix_post', 'delta_w_in', 'delta_w_gate', 'delta_b_gate', 'delta_ssm_a_re', 'delta_ssm_a_im', 'delta_ssm_log_dt', 'delta_ssm_b_re', 'delta_ssm_b_im', 'delta_ssm_c_re', 'delta_ssm_c_im', 'delta_ssm_d', 'delta_w_glu_val', 'delta_w_glu_gate', 'delta_w_attn_out', 'delta_conv_w', 'delta_w_conv_out', 'delta_w_mix_out', 'delta_norm_ffn_pre', 'delta_norm_ffn_post', 'delta_w_ffn_up', 'delta_w_ffn_down', 'new_m_norm_mix_pre', 'new_m_norm_mix_post', 'new_m_w_in', 'new_m_w_gate', 'new_m_b_gate', 'new_m_ssm_a_re', 'new_m_ssm_a_im', 'new_m_ssm_log_dt', 'new_m_ssm_b_re', 'new_m_ssm_b_im', 'new_m_ssm_c_re', 'new_m_ssm_c_im', 'new_m_ssm_d', 'new_m_w_glu_val', 'new_m_w_glu_gate', 'new_m_w_attn_out', 'new_m_conv_w', 'new_m_w_conv_out', 'new_m_w_mix_out', 'new_m_norm_ffn_pre', 'new_m_norm_ffn_post', 'new_m_w_ffn_up', 'new_m_w_ffn_down', 'new_v_norm_mix_pre', 'new_v_norm_mix_post', 'new_v_w_in', 'new_v_w_gate', 'new_v_b_gate', 'new_v_ssm_a_re', 'new_v_ssm_a_im', 'new_v_ssm_log_dt', 'new_v_ssm_b_re', 'new_v_ssm_b_im', 'new_v_ssm_c_re', 'new_v_ssm_c_im', 'new_v_ssm_d', 'new_v_w_glu_val', 'new_v_w_glu_gate', 'new_v_w_attn_out', 'new_v_conv_w', 'new_v_w_conv_out', 'new_v_w_mix_out', 'new_v_norm_ffn_pre', 'new_v_norm_ffn_post', 'new_v_w_ffn_up', 'new_v_w_ffn_down']
TWIN_LEAF_KINDS = {'loss': 'loss', 'grad_x': 'grad_x', 'grad_norm_mix_pre': 'grad_w', 'grad_norm_mix_post': 'grad_w', 'grad_w_in': 'grad_w', 'grad_w_gate': 'grad_w', 'grad_b_gate': 'grad_w', 'grad_ssm_a_re': 'grad_w', 'grad_ssm_a_im': 'grad_w', 'grad_ssm_log_dt': 'grad_w', 'grad_ssm_b_re': 'grad_w', 'grad_ssm_b_im': 'grad_w', 'grad_ssm_c_re': 'grad_w', 'grad_ssm_c_im': 'grad_w', 'grad_ssm_d': 'grad_w', 'grad_w_glu_val': 'grad_w', 'grad_w_glu_gate': 'grad_w', 'grad_w_attn_out': 'grad_w', 'grad_conv_w': 'grad_w', 'grad_w_conv_out': 'grad_w', 'grad_w_mix_out': 'grad_w', 'grad_norm_ffn_pre': 'grad_w', 'grad_norm_ffn_post': 'grad_w', 'grad_w_ffn_up': 'grad_w', 'grad_w_ffn_down': 'grad_w', 'delta_norm_mix_pre': 'delta_w', 'delta_norm_mix_post': 'delta_w', 'delta_w_in': 'delta_w', 'delta_w_gate': 'delta_w', 'delta_b_gate': 'delta_w', 'delta_ssm_a_re': 'delta_w', 'delta_ssm_a_im': 'delta_w', 'delta_ssm_log_dt': 'delta_w', 'delta_ssm_b_re': 'delta_w', 'delta_ssm_b_im': 'delta_w', 'delta_ssm_c_re': 'delta_w', 'delta_ssm_c_im': 'delta_w', 'delta_ssm_d': 'delta_w', 'delta_w_glu_val': 'delta_w', 'delta_w_glu_gate': 'delta_w', 'delta_w_attn_out': 'delta_w', 'delta_conv_w': 'delta_w', 'delta_w_conv_out': 'delta_w', 'delta_w_mix_out': 'delta_w', 'delta_norm_ffn_pre': 'delta_w', 'delta_norm_ffn_post': 'delta_w', 'delta_w_ffn_up': 'delta_w', 'delta_w_ffn_down': 'delta_w', 'new_m_norm_mix_pre': 'new_m', 'new_m_norm_mix_post': 'new_m', 'new_m_w_in': 'new_m', 'new_m_w_gate': 'new_m', 'new_m_b_gate': 'new_m', 'new_m_ssm_a_re': 'new_m', 'new_m_ssm_a_im': 'new_m', 'new_m_ssm_log_dt': 'new_m', 'new_m_ssm_b_re': 'new_m', 'new_m_ssm_b_im': 'new_m', 'new_m_ssm_c_re': 'new_m', 'new_m_ssm_c_im': 'new_m', 'new_m_ssm_d': 'new_m', 'new_m_w_glu_val': 'new_m', 'new_m_w_glu_gate': 'new_m', 'new_m_w_attn_out': 'new_m', 'new_m_conv_w': 'new_m', 'new_m_w_conv_out': 'new_m', 'new_m_w_mix_out': 'new_m', 'new_m_norm_ffn_pre': 'new_m', 'new_m_norm_ffn_post': 'new_m', 'new_m_w_ffn_up': 'new_m', 'new_m_w_ffn_down': 'new_m', 'new_v_norm_mix_pre': 'new_v', 'new_v_norm_mix_post': 'new_v', 'new_v_w_in': 'new_v', 'new_v_w_gate': 'new_v', 'new_v_b_gate': 'new_v', 'new_v_ssm_a_re': 'new_v', 'new_v_ssm_a_im': 'new_v', 'new_v_ssm_log_dt': 'new_v', 'new_v_ssm_b_re': 'new_v', 'new_v_ssm_b_im': 'new_v', 'new_v_ssm_c_re': 'new_v', 'new_v_ssm_c_im': 'new_v', 'new_v_ssm_d': 'new_v', 'new_v_w_glu_val': 'new_v', 'new_v_w_glu_gate': 'new_v', 'new_v_w_attn_out': 'new_v', 'new_v_conv_w': 'new_v', 'new_v_w_conv_out': 'new_v', 'new_v_w_mix_out': 'new_v', 'new_v_norm_ffn_pre': 'new_v', 'new_v_norm_ffn_post': 'new_v', 'new_v_w_ffn_up': 'new_v', 'new_v_w_ffn_down': 'new_v'}


def _forward(args):
    return _fwd_reference(*[args[k] for k in FWD_PARAMS])


def _output_shape():
    def fwd():
        inp = _fwd_setup_inputs(0)
        return _fwd_reference(*[inp[k] for k in FWD_PARAMS])
    out = _jax.eval_shape(fwd)
    return out.shape, out.dtype

N_MICROBATCH = 1
ADAM_LR = 0.001
ADAM_B1 = 0.9
ADAM_B2 = 0.999
ADAM_EPS = 1e-08
ADAM_WD = 0.01
ADAM_STEP = 10
PER_EXAMPLE_BATCH_AXIS = {'x': 0, 'loss_target': 0}
SHARED_INPUTS = []
_WEIGHT_DTYPES = {'norm_mix_pre': _jnp.float32, 'norm_mix_post': _jnp.float32, 'w_in': _jnp.float32, 'w_gate': _jnp.float32, 'b_gate': _jnp.float32, 'ssm_a_re': _jnp.float32, 'ssm_a_im': _jnp.float32, 'ssm_log_dt': _jnp.float32, 'ssm_b_re': _jnp.float32, 'ssm_b_im': _jnp.float32, 'ssm_c_re': _jnp.float32, 'ssm_c_im': _jnp.float32, 'ssm_d': _jnp.float32, 'w_glu_val': _jnp.float32, 'w_glu_gate': _jnp.float32, 'w_attn_out': _jnp.float32, 'conv_w': _jnp.float32, 'w_conv_out': _jnp.float32, 'w_mix_out': _jnp.float32, 'norm_ffn_pre': _jnp.float32, 'norm_ffn_post': _jnp.float32, 'w_ffn_up': _jnp.float32, 'w_ffn_down': _jnp.float32}
MOMENT_SCALE = {'norm_mix_pre': 7.628377e+00, 'norm_mix_post': 1.296447e+02, 'w_in': 4.756628e+00, 'w_gate': 5.746335e-01, 'b_gate': 3.794399e+00, 'ssm_a_re': 3.746996e-01, 'ssm_a_im': 3.221124e-01, 'ssm_log_dt': 6.881386e+01, 'ssm_b_re': 4.285633e-01, 'ssm_b_im': 3.860114e-01, 'ssm_c_re': 6.770424e-01, 'ssm_c_im': 4.910373e-01, 'ssm_d': 2.574004e+01, 'w_glu_val': 1.247657e+01, 'w_glu_gate': 1.948634e+00, 'w_attn_out': 1.069997e+01, 'conv_w': 3.534636e+00, 'w_conv_out': 2.378641e+00, 'w_mix_out': 1.560971e+01, 'norm_ffn_pre': 8.809032e+00, 'norm_ffn_post': 1.380955e+02, 'w_ffn_up': 4.220901e+00, 'w_ffn_down': 3.105717e+01}


def _to_microbatches(a, axis):
    t = _jnp.moveaxis(a, axis, 0)
    t = t.reshape((N_MICROBATCH, t.shape[0] // N_MICROBATCH) + t.shape[1:])
    return _jnp.moveaxis(t, 1, axis + 1)


def setup_inputs(seed: int = 0) -> dict:
    inp = _fwd_setup_inputs(seed)
    key = _jax.random.fold_in(_jax.random.key(seed), 7919)
    shape, _ = _output_shape()
    out = dict(inp)
    out["loss_target"] = _jax.random.normal(_jax.random.fold_in(key, 0), shape, _jnp.float32)
    for i, name in enumerate(TWIN_WEIGHTS):
        w = inp[name].astype(_jnp.float32)
        if MOMENT_SCALE is None:
            s = _jnp.sqrt(_jnp.mean(_jnp.square(w)) + 1e-30)
        else:
            s = MOMENT_SCALE[name]
        km, kv = _jax.random.split(_jax.random.fold_in(key, i + 1))
        out[name] = w
        out["m_" + name] = s * _jax.random.normal(km, w.shape, _jnp.float32)
        out["v_" + name] = (s * s) * _jax.random.uniform(kv, w.shape, _jnp.float32, 0.5, 1.5)
    if N_MICROBATCH > 1:
        for name, axis in PER_EXAMPLE_BATCH_AXIS.items():
            out[name] = _to_microbatches(out[name], axis)
    return {'x': out['x'], 'norm_mix_pre': out['norm_mix_pre'], 'norm_mix_post': out['norm_mix_post'], 'w_in': out['w_in'], 'w_gate': out['w_gate'], 'b_gate': out['b_gate'], 'ssm_a_re': out['ssm_a_re'], 'ssm_a_im': out['ssm_a_im'], 'ssm_log_dt': out['ssm_log_dt'], 'ssm_b_re': out['ssm_b_re'], 'ssm_b_im': out['ssm_b_im'], 'ssm_c_re': out['ssm_c_re'], 'ssm_c_im': out['ssm_c_im'], 'ssm_d': out['ssm_d'], 'w_glu_val': out['w_glu_val'], 'w_glu_gate': out['w_glu_gate'], 'w_attn_out': out['w_attn_out'], 'conv_w': out['conv_w'], 'w_conv_out': out['w_conv_out'], 'w_mix_out': out['w_mix_out'], 'norm_ffn_pre': out['norm_ffn_pre'], 'norm_ffn_post': out['norm_ffn_post'], 'w_ffn_up': out['w_ffn_up'], 'w_ffn_down': out['w_ffn_down'], 'loss_target': out['loss_target'], 'm_norm_mix_pre': out['m_norm_mix_pre'], 'm_norm_mix_post': out['m_norm_mix_post'], 'm_w_in': out['m_w_in'], 'm_w_gate': out['m_w_gate'], 'm_b_gate': out['m_b_gate'], 'm_ssm_a_re': out['m_ssm_a_re'], 'm_ssm_a_im': out['m_ssm_a_im'], 'm_ssm_log_dt': out['m_ssm_log_dt'], 'm_ssm_b_re': out['m_ssm_b_re'], 'm_ssm_b_im': out['m_ssm_b_im'], 'm_ssm_c_re': out['m_ssm_c_re'], 'm_ssm_c_im': out['m_ssm_c_im'], 'm_ssm_d': out['m_ssm_d'], 'm_w_glu_val': out['m_w_glu_val'], 'm_w_glu_gate': out['m_w_glu_gate'], 'm_w_attn_out': out['m_w_attn_out'], 'm_conv_w': out['m_conv_w'], 'm_w_conv_out': out['m_w_conv_out'], 'm_w_mix_out': out['m_w_mix_out'], 'm_norm_ffn_pre': out['m_norm_ffn_pre'], 'm_norm_ffn_post': out['m_norm_ffn_post'], 'm_w_ffn_up': out['m_w_ffn_up'], 'm_w_ffn_down': out['m_w_ffn_down'], 'v_norm_mix_pre': out['v_norm_mix_pre'], 'v_norm_mix_post': out['v_norm_mix_post'], 'v_w_in': out['v_w_in'], 'v_w_gate': out['v_w_gate'], 'v_b_gate': out['v_b_gate'], 'v_ssm_a_re': out['v_ssm_a_re'], 'v_ssm_a_im': out['v_ssm_a_im'], 'v_ssm_log_dt': out['v_ssm_log_dt'], 'v_ssm_b_re': out['v_ssm_b_re'], 'v_ssm_b_im': out['v_ssm_b_im'], 'v_ssm_c_re': out['v_ssm_c_re'], 'v_ssm_c_im': out['v_ssm_c_im'], 'v_ssm_d': out['v_ssm_d'], 'v_w_glu_val': out['v_w_glu_val'], 'v_w_glu_gate': out['v_w_glu_gate'], 'v_w_attn_out': out['v_w_attn_out'], 'v_conv_w': out['v_conv_w'], 'v_w_conv_out': out['v_w_conv_out'], 'v_w_mix_out': out['v_w_mix_out'], 'v_norm_ffn_pre': out['v_norm_ffn_pre'], 'v_norm_ffn_post': out['v_norm_ffn_post'], 'v_w_ffn_up': out['v_w_ffn_up'], 'v_w_ffn_down': out['v_w_ffn_down']}


def _loss(weights, diff, rest, loss_target):
    with _jax.named_scope("forward"):
        args = {**rest, TWIN_DIFF_INPUT: diff, **{k: w.astype(_WEIGHT_DTYPES[k]) for k, w in weights.items()}}
        y = _forward(args)
    with _jax.named_scope("loss_head"):
        err = _jnp.square(y.astype(_jnp.float32) - loss_target)
        return 0.5 * _jnp.sum(_jnp.mean(err, axis=-1)) if err.ndim else 0.5 * err


def _adamw(w, g, m, v):
    m = ADAM_B1 * m + (1.0 - ADAM_B1) * g
    v = ADAM_B2 * v + (1.0 - ADAM_B2) * _jnp.square(g)
    m_hat = m / (1.0 - ADAM_B1 ** ADAM_STEP)
    v_hat = v / (1.0 - ADAM_B2 ** ADAM_STEP)
    delta = -ADAM_LR * (m_hat / (_jnp.sqrt(v_hat) + ADAM_EPS) + ADAM_WD * w)
    return delta, m, v


def reference(x, norm_mix_pre, norm_mix_post, w_in, w_gate, b_gate, ssm_a_re, ssm_a_im, ssm_log_dt, ssm_b_re, ssm_b_im, ssm_c_re, ssm_c_im, ssm_d, w_glu_val, w_glu_gate, w_attn_out, conv_w, w_conv_out, w_mix_out, norm_ffn_pre, norm_ffn_post, w_ffn_up, w_ffn_down, loss_target, m_norm_mix_pre, m_norm_mix_post, m_w_in, m_w_gate, m_b_gate, m_ssm_a_re, m_ssm_a_im, m_ssm_log_dt, m_ssm_b_re, m_ssm_b_im, m_ssm_c_re, m_ssm_c_im, m_ssm_d, m_w_glu_val, m_w_glu_gate, m_w_attn_out, m_conv_w, m_w_conv_out, m_w_mix_out, m_norm_ffn_pre, m_norm_ffn_post, m_w_ffn_up, m_w_ffn_down, v_norm_mix_pre, v_norm_mix_post, v_w_in, v_w_gate, v_b_gate, v_ssm_a_re, v_ssm_a_im, v_ssm_log_dt, v_ssm_b_re, v_ssm_b_im, v_ssm_c_re, v_ssm_c_im, v_ssm_d, v_w_glu_val, v_w_glu_gate, v_w_attn_out, v_conv_w, v_w_conv_out, v_w_mix_out, v_norm_ffn_pre, v_norm_ffn_post, v_w_ffn_up, v_w_ffn_down):
    given = dict(x=x, norm_mix_pre=norm_mix_pre, norm_mix_post=norm_mix_post, w_in=w_in, w_gate=w_gate, b_gate=b_gate, ssm_a_re=ssm_a_re, ssm_a_im=ssm_a_im, ssm_log_dt=ssm_log_dt, ssm_b_re=ssm_b_re, ssm_b_im=ssm_b_im, ssm_c_re=ssm_c_re, ssm_c_im=ssm_c_im, ssm_d=ssm_d, w_glu_val=w_glu_val, w_glu_gate=w_glu_gate, w_attn_out=w_attn_out, conv_w=conv_w, w_conv_out=w_conv_out, w_mix_out=w_mix_out, norm_ffn_pre=norm_ffn_pre, norm_ffn_post=norm_ffn_post, w_ffn_up=w_ffn_up, w_ffn_down=w_ffn_down, loss_target=loss_target, m_norm_mix_pre=m_norm_mix_pre, m_norm_mix_post=m_norm_mix_post, m_w_in=m_w_in, m_w_gate=m_w_gate, m_b_gate=m_b_gate, m_ssm_a_re=m_ssm_a_re, m_ssm_a_im=m_ssm_a_im, m_ssm_log_dt=m_ssm_log_dt, m_ssm_b_re=m_ssm_b_re, m_ssm_b_im=m_ssm_b_im, m_ssm_c_re=m_ssm_c_re, m_ssm_c_im=m_ssm_c_im, m_ssm_d=m_ssm_d, m_w_glu_val=m_w_glu_val, m_w_glu_gate=m_w_glu_gate, m_w_attn_out=m_w_attn_out, m_conv_w=m_conv_w, m_w_conv_out=m_w_conv_out, m_w_mix_out=m_w_mix_out, m_norm_ffn_pre=m_norm_ffn_pre, m_norm_ffn_post=m_norm_ffn_post, m_w_ffn_up=m_w_ffn_up, m_w_ffn_down=m_w_ffn_down, v_norm_mix_pre=v_norm_mix_pre, v_norm_mix_post=v_norm_mix_post, v_w_in=v_w_in, v_w_gate=v_w_gate, v_b_gate=v_b_gate, v_ssm_a_re=v_ssm_a_re, v_ssm_a_im=v_ssm_a_im, v_ssm_log_dt=v_ssm_log_dt, v_ssm_b_re=v_ssm_b_re, v_ssm_b_im=v_ssm_b_im, v_ssm_c_re=v_ssm_c_re, v_ssm_c_im=v_ssm_c_im, v_ssm_d=v_ssm_d, v_w_glu_val=v_w_glu_val, v_w_glu_gate=v_w_glu_gate, v_w_attn_out=v_w_attn_out, v_conv_w=v_conv_w, v_w_conv_out=v_w_conv_out, v_w_mix_out=v_w_mix_out, v_norm_ffn_pre=v_norm_ffn_pre, v_norm_ffn_post=v_norm_ffn_post, v_w_ffn_up=v_w_ffn_up, v_w_ffn_down=v_w_ffn_down)
    weights = {n: given[n] for n in TWIN_WEIGHTS}
    shared = {n: given[n] for n in SHARED_INPUTS}
    per_example = {n: given[n] for n in ['x']}
    grad_fn = _jax.value_and_grad(_loss, argnums=(0, 1))

    def one_microbatch(ex, loss_target):
        ex = dict(ex)
        diff = ex.pop(TWIN_DIFF_INPUT)
        return grad_fn(weights, diff, {**shared, **ex}, loss_target)

    if N_MICROBATCH == 1:
        loss, (grad_w, grad_x) = one_microbatch(per_example, given["loss_target"])
    else:
        def body(carry, xs):
            loss_sum, grad_sum = carry
            l_k, (gw_k, gx_k) = one_microbatch(xs[0], xs[1])
            with _jax.named_scope("update"):
                return (loss_sum + l_k, _jax.tree.map(_jnp.add, grad_sum, gw_k)), gx_k

        init = (_jnp.zeros((), _jnp.float32), _jax.tree.map(_jnp.zeros_like, weights))
        (loss, grad_w), grad_x = _jax.lax.scan(body, init, (per_example, given["loss_target"]))
    with _jax.named_scope("update"):
        delta_w, new_m, new_v = {}, {}, {}
        for n in TWIN_WEIGHTS:
            delta_w[n], new_m[n], new_v[n] = _adamw(weights[n], grad_w[n], given["m_" + n], given["v_" + n])
    return (loss, grad_x, *[grad_w[n] for n in TWIN_WEIGHTS], *[delta_w[n] for n in TWIN_WEIGHTS],
            *[new_m[n] for n in TWIN_WEIGHTS], *[new_v[n] for n in TWIN_WEIGHTS])
```

```python
import functools
import math

import jax
import jax.numpy as jnp
from jax import lax
from jax.experimental import pallas as pl
from jax.experimental.pallas import tpu as pltpu

F32 = jnp.float32
BF16 = jnp.bfloat16

N_DEV = 8
D_MODEL = 1024
DEPTH = 2
D_SSM = 256
N_GROUPS = 16
GROUP_CH = 16
N_STATE = 64
D_STATE = N_GROUPS * N_STATE
D_SB = 512
HEAD_DIM = 64
D_CONV = 256
D_IN = 2560
D_FF = 4096
EPS = 1e-6
Q_SCALE = HEAD_DIM ** -0.5

ADAM_LR = 0.001
ADAM_B1 = 0.9
ADAM_B2 = 0.999
ADAM_EPS = 1e-08
ADAM_WD = 0.01
ADAM_STEP = 10

LANES = 128
PIECE_ALIGN = 2048
VMEM_LIMIT = 56 * 1024 * 1024

WEIGHTS = ['norm_mix_pre', 'norm_mix_post', 'w_in', 'w_gate', 'b_gate', 'ssm_a_re', 'ssm_a_im', 'ssm_log_dt',
           'ssm_b_re', 'ssm_b_im', 'ssm_c_re', 'ssm_c_im', 'ssm_d', 'w_glu_val', 'w_glu_gate', 'w_attn_out',
           'conv_w', 'w_conv_out', 'w_mix_out', 'norm_ffn_pre', 'norm_ffn_post', 'w_ffn_up', 'w_ffn_down']
SHARD_AXIS = {'w_in': 2, 'w_gate': 2, 'w_glu_val': 2, 'w_glu_gate': 2, 'w_attn_out': 2, 'conv_w': 2,
              'w_conv_out': 2, 'w_mix_out': 1, 'w_ffn_up': 2, 'w_ffn_down': 1}
SHARDED = [n for n in WEIGHTS if n in SHARD_AXIS]
REPLICATED = [n for n in WEIGHTS if n not in SHARD_AXIS]


def _cparams(n_grid):
    return pltpu.CompilerParams(dimension_semantics=("arbitrary",) * n_grid, vmem_limit_bytes=VMEM_LIMIT)


def _dot(a, b):
    return jnp.dot(a.astype(BF16), b.astype(BF16), preferred_element_type=F32)


def _dot_nt(a, b):
    return lax.dot_general(a.astype(BF16), b.astype(BF16), (((1,), (1,)), ((), ())), preferred_element_type=F32)


def _dot_tn(a, b):
    return lax.dot_general(a.astype(BF16), b.astype(BF16), (((0,), (0,)), ((), ())), preferred_element_type=F32)


def _split_dot(a, b):
    hi = a.astype(BF16)
    lo = (a - hi.astype(F32)).astype(BF16)
    return jnp.dot(hi, b, preferred_element_type=F32) + jnp.dot(lo, b, preferred_element_type=F32)


def _rms_fwd(x, g):
    r = lax.rsqrt(jnp.mean(x * x, axis=-1, keepdims=True) + EPS)
    return x * r * g


def _rms_bwd(dy, x, g):
    r = lax.rsqrt(jnp.mean(x * x, axis=-1, keepdims=True) + EPS)
    dyg = dy * g
    dx = r * dyg - x * (r * r * r) * jnp.mean(dyg * x, axis=-1, keepdims=True)
    dg = jnp.sum(dy * x * r, axis=0, keepdims=True)
    return dx, dg


def _sigmoid(x):
    return 1.0 / (1.0 + jnp.exp(-x))


_GELU_C = math.sqrt(2.0 / math.pi)


def _gelu(y):
    return 0.5 * y * (1.0 + jnp.tanh(_GELU_C * (y + 0.044715 * y * y * y)))


def _gelu_grad(y):
    t = jnp.tanh(_GELU_C * (y + 0.044715 * y * y * y))
    return 0.5 * (1.0 + t) + 0.5 * y * (1.0 - t * t) * _GELU_C * (1.0 + 3.0 * 0.044715 * y * y)


def _full(shape):
    return pl.BlockSpec(shape, lambda *_: (0,) * len(shape))


def _peer(x, y, c, k):
    px = 1 - x if (k >> 2) & 1 else x
    py = 1 - y if (k >> 1) & 1 else y
    pc = 1 - c if k & 1 else c
    return px, py, pc


def _all_gather(shard):
    rows, lanes = shard.shape

    def body(src_ref, out_ref, send_sems, recv_sems, local_sem):
        x, y, c = lax.axis_index("x"), lax.axis_index("y"), lax.axis_index("c")
        me = 4 * x + 2 * y + c
        mine = pltpu.make_async_copy(src_ref, out_ref.at[me], local_sem)
        mine.start()
        copies = []
        for k in range(1, N_DEV):
            cp = pltpu.make_async_remote_copy(
                src_ref=src_ref, dst_ref=out_ref.at[me],
                send_sem=send_sems.at[k - 1], recv_sem=recv_sems.at[k - 1],
                device_id=_peer(x, y, c, k), device_id_type=pl.DeviceIdType.MESH)
            cp.start()
            copies.append(cp)
        for cp in copies:
            cp.wait()
        mine.wait()

    return pl.pallas_call(
        body, name="weights_all_gather",
        out_shape=jax.ShapeDtypeStruct((N_DEV, rows, lanes), shard.dtype),
        in_specs=[pl.BlockSpec(memory_space=pl.ANY)],
        out_specs=pl.BlockSpec(memory_space=pl.ANY),
        scratch_shapes=[pltpu.SemaphoreType.DMA((N_DEV - 1,)), pltpu.SemaphoreType.DMA((N_DEV - 1,)),
                        pltpu.SemaphoreType.DMA],
    )(shard)


def _all_to_all(parts):
    n, rows, lanes = parts.shape

    def body(src_ref, out_ref, send_sems, recv_sems, local_sem):
        x, y, c = lax.axis_index("x"), lax.axis_index("y"), lax.axis_index("c")
        me = 4 * x + 2 * y + c
        mine = pltpu.make_async_copy(src_ref.at[me], out_ref.at[me], local_sem)
        mine.start()
        copies = []
        for k in range(1, N_DEV):
            px, py, pc = _peer(x, y, c, k)
            cp = pltpu.make_async_remote_copy(
                src_ref=src_ref.at[4 * px + 2 * py + pc], dst_ref=out_ref.at[me],
                send_sem=send_sems.at[k - 1], recv_sem=recv_sems.at[k - 1],
                device_id=(px, py, pc), device_id_type=pl.DeviceIdType.MESH)
            cp.start()
            copies.append(cp)
        for cp in copies:
            cp.wait()
        mine.wait()

    return pl.pallas_call(
        body, name="grads_all_to_all",
        out_shape=jax.ShapeDtypeStruct((N_DEV, rows, lanes), parts.dtype),
        in_specs=[pl.BlockSpec(memory_space=pl.ANY)],
        out_specs=pl.BlockSpec(memory_space=pl.ANY),
        scratch_shapes=[pltpu.SemaphoreType.DMA((N_DEV - 1,)), pltpu.SemaphoreType.DMA((N_DEV - 1,)),
                        pltpu.SemaphoreType.DMA],
    )(parts)


def _adamw(recv, w, m, v):
    rows = w.shape[0]
    tr = 512
    assert rows % tr == 0
    c1 = 1.0 / (1.0 - ADAM_B1 ** ADAM_STEP)
    c2 = 1.0 / (1.0 - ADAM_B2 ** ADAM_STEP)

    def body(r_ref, w_ref, m_ref, v_ref, g_ref, d_ref, mo_ref, vo_ref):
        g = r_ref[0]
        for s in range(1, N_DEV):
            g = g + r_ref[s]
        mn = ADAM_B1 * m_ref[...] + (1.0 - ADAM_B1) * g
        vn = ADAM_B2 * v_ref[...] + (1.0 - ADAM_B2) * (g * g)
        upd = (mn * c1) / (jnp.sqrt(vn * c2) + ADAM_EPS) + ADAM_WD * w_ref[...]
        g_ref[...] = g
        d_ref[...] = -ADAM_LR * upd
        mo_ref[...] = mn
        vo_ref[...] = vn

    spec = pl.BlockSpec((tr, LANES), lambda i: (i, 0))
    return pl.pallas_call(
        body, name="adamw", grid=(rows // tr,),
        out_shape=[jax.ShapeDtypeStruct((rows, LANES), F32)] * 4,
        in_specs=[pl.BlockSpec((N_DEV, tr, LANES), lambda i: (0, i, 0)), spec, spec, spec],
        out_specs=[spec] * 4,
        compiler_params=_cparams(1),
    )(recv, w, m, v)


def _tn_matmul(a, b, name):
    s, m = a.shape
    n = b.shape[1]
    tk = min(512, s)
    tm = min(1024, m)
    tn = max(c for c in range(LANES, 1280 + 1, LANES) if n % c == 0)
    assert m % tm == 0 and s % tk == 0
    nk = s // tk

    def body(a_ref, b_ref, o_ref):
        k = pl.program_id(2)

        @pl.when(k == 0)
        def _():
            o_ref[...] = jnp.zeros_like(o_ref)

        o_ref[...] += _dot_tn(a_ref[...], b_ref[...])

    return pl.pallas_call(
        body, name=name, grid=(m // tm, n // tn, nk),
        out_shape=jax.ShapeDtypeStruct((m, n), F32),
        in_specs=[pl.BlockSpec((tk, tm), lambda i, j, k: (k, i)), pl.BlockSpec((tk, tn), lambda i, j, k: (k, j))],
        out_specs=pl.BlockSpec((tm, tn), lambda i, j, k: (i, j)),
        compiler_params=_cparams(3),
    )(a, b)


def _token_tile(s):
    return min(256, s)


def _pre_fwd(x, n1, w_in):
    s = x.shape[0]
    tb = _token_tile(s)

    def body(x_ref, n_ref, w_ref, u_ref, q_ref, k_ref, v_ref, c_ref):
        h = _rms_fwd(x_ref[...], n_ref[...])
        p = _dot(h, w_ref[...])
        u_ref[...] = p[:, 0:256]
        q_ref[...] = (p[:, 256:768] * Q_SCALE).astype(BF16)
        k_ref[...] = p[:, 768:1280].astype(BF16)
        v_ref[...] = p[:, 1280:1792].astype(BF16)
        c_ref[...] = p[:, 1792:2560]

    def tok(w):
        return pl.BlockSpec((tb, w), lambda i: (i, 0))

    return pl.pallas_call(
        body, name="pre_fwd", grid=(s // tb,),
        out_shape=[jax.ShapeDtypeStruct((s, D_SSM), F32), jax.ShapeDtypeStruct((s, D_SB), BF16),
                   jax.ShapeDtypeStruct((s, D_SB), BF16), jax.ShapeDtypeStruct((s, D_SB), BF16),
                   jax.ShapeDtypeStruct((s, 3 * D_CONV), F32)],
        in_specs=[tok(D_MODEL), _full((1, D_MODEL)), _full((D_MODEL, D_IN))],
        out_specs=[tok(D_SSM), tok(D_SB), tok(D_SB), tok(D_SB), tok(3 * D_CONV)],
        compiler_params=_cparams(1),
    )(x, n1, w_in)


def _ssm_discretize(lr, li, ldt, br, bi):
    dt = jnp.exp(ldt)
    mag = jnp.exp(lr * dt)
    ab_re = mag * jnp.cos(li * dt)
    ab_im = mag * jnp.sin(li * dt)
    den = lr * lr + li * li
    xr = ab_re - 1.0
    coef_re = (xr * lr + ab_im * li) / den
    coef_im = (ab_im * lr - xr * li) / den
    bb_re = coef_re * br - coef_im * bi
    bb_im = coef_re * bi + coef_im * br
    return ab_re, ab_im, bb_re, bb_im


def _ssm_params_fwd(lr, li, ldt, br, bi):
    def body(lr_ref, li_ref, ldt_ref, br_ref, bi_ref, ar_ref, ai_ref, bbr_ref, bbi_ref):
        ar, ai, bbr, bbi = _ssm_discretize(lr_ref[...], li_ref[...], ldt_ref[...], br_ref[...], bi_ref[...])
        ar_ref[...] = ar
        ai_ref[...] = ai
        bbr_ref[...] = bbr
        bbi_ref[...] = bbi

    row = jax.ShapeDtypeStruct((1, D_STATE), F32)
    mat = jax.ShapeDtypeStruct((GROUP_CH, D_STATE), F32)
    return pl.pallas_call(body, name="ssm_params_fwd", out_shape=[row, row, mat, mat])(lr, li, ldt, br, bi)


def _ssm_params_bwd(lr, li, ldt, br, bi, d_ar, d_ai, d_bbr, d_bbi):
    def body(lr_ref, li_ref, ldt_ref, br_ref, bi_ref, dar_ref, dai_ref, dbbr_ref, dbbi_ref,
             glr_ref, gli_ref, gdt_ref, gbr_ref, gbi_ref):
        _, vjp = jax.vjp(_ssm_discretize, lr_ref[...], li_ref[...], ldt_ref[...], br_ref[...], bi_ref[...])
        glr, gli, gdt, gbr, gbi = vjp((dar_ref[...], dai_ref[...], dbbr_ref[...], dbbi_ref[...]))
        glr_ref[...] = glr
        gli_ref[...] = gli
        gbr_ref[...] = gbr
        gbi_ref[...] = gbi
        grp = lax.broadcasted_iota(jnp.int32, (N_GROUPS, D_STATE), 0)
        col = lax.broadcasted_iota(jnp.int32, (N_GROUPS, D_STATE), 1)
        own = (col // N_STATE) == grp
        per_group = jnp.sum(jnp.where(own, jnp.broadcast_to(gdt, (N_GROUPS, D_STATE)), 0.0), axis=1, keepdims=True)
        gdt_ref[...] = jnp.broadcast_to(per_group, (N_GROUPS, LANES))

    row = jax.ShapeDtypeStruct((1, D_STATE), F32)
    mat = jax.ShapeDtypeStruct((GROUP_CH, D_STATE), F32)
    return pl.pallas_call(
        body, name="ssm_params_bwd",
        out_shape=[row, row, jax.ShapeDtypeStruct((N_GROUPS, LANES), F32), mat, mat],
    )(lr, li, ldt, br, bi, d_ar, d_ai, d_bbr, d_bbi)


def _ssm_fwd(u, ab_re, ab_im, bb_re, bb_im, cc_re, cc_im, d_skip):
    s = u.shape[0]
    ts = _token_tile(s)
    nt = s // ts

    def body(u_ref, ar_ref, ai_ref, bbr_ref, bbi_ref, ccr_ref, cci_ref, d_ref, y_ref, ypre_ref, st_ref,
             bur, bui, hr_s, hi_s, cr, ci):
        @pl.when(pl.program_id(0) == 0)
        def _():
            cr[...] = jnp.zeros_like(cr)
            ci[...] = jnp.zeros_like(ci)

        u = u_ref[...]
        bur[...] = _dot(u, bbr_ref[...])
        bui[...] = _dot(u, bbi_ref[...])
        st_ref[0, 0:1, :] = cr[...]
        st_ref[0, 1:2, :] = ci[...]
        ar = ar_ref[...]
        ai = ai_ref[...]

        def step(t, carry):
            hr, hi = carry
            nr = ar * hr - ai * hi + bur[pl.ds(t, 1), :]
            ni = ar * hi + ai * hr + bui[pl.ds(t, 1), :]
            hr_s[pl.ds(t, 1), :] = nr
            hi_s[pl.ds(t, 1), :] = ni
            return nr, ni

        hr, hi = lax.fori_loop(0, ts, step, (cr[...], ci[...]), unroll=8)
        cr[...] = hr
        ci[...] = hi
        y = _dot_nt(hr_s[...], ccr_ref[...]) - _dot_nt(hi_s[...], cci_ref[...]) + d_ref[...] * u
        ypre_ref[...] = y
        y_ref[...] = _gelu(y).astype(BF16)

    tok = pl.BlockSpec((ts, D_SSM), lambda i: (i, 0))
    row = _full((1, D_STATE))
    mat = _full((D_SSM, D_STATE))
    return pl.pallas_call(
        body, name="ssm_fwd", grid=(nt,),
        out_shape=[jax.ShapeDtypeStruct((s, D_SSM), BF16), jax.ShapeDtypeStruct((s, D_SSM), F32),
                   jax.ShapeDtypeStruct((nt, 2, D_STATE), F32)],
        in_specs=[tok, row, row, mat, mat, mat, mat, _full((1, D_SSM))],
        out_specs=[tok, tok, pl.BlockSpec((1, 2, D_STATE), lambda i: (i, 0, 0))],
        scratch_shapes=[pltpu.VMEM((ts, D_STATE), F32)] * 4 + [pltpu.VMEM((1, D_STATE), F32)] * 2,
        compiler_params=_cparams(1),
    )(u, ab_re, ab_im, bb_re, bb_im, cc_re, cc_im, d_skip)


def _attn_tile(s):
    return min(256, s)


def _sb_block(z, valid):
    lp = jnp.log(1.0 + jnp.exp(-jnp.abs(z)))
    ls = jnp.minimum(z, 0.0) - lp
    l1m = ls - z
    if valid is not None:
        l1m = jnp.where(valid, l1m, 0.0)
    return ls, l1m


def _attn_fwd(q, k, v):
    s = q.shape[0]
    t = _attn_tile(s)
    nq = s // t

    def body(q_ref, k_ref, v_ref, o_ref):
        i = pl.program_id(1)
        lane = lax.broadcasted_iota(jnp.int32, (t, LANES), 1)
        first = lane < HEAD_DIM
        row = lax.broadcasted_iota(jnp.int32, (t, t), 0)
        col = lax.broadcasted_iota(jnp.int32, (t, t), 1)
        later = (row > col).astype(BF16)
        causal = col < row
        q2 = q_ref[...]
        zero = jnp.zeros_like(q2)
        qs = (jnp.where(first, q2, zero), jnp.where(first, zero, q2))

        def block(j, carry, valid):
            start = pl.multiple_of(j * t, t)
            k2 = k_ref[pl.ds(start, t), :]
            v2 = v_ref[pl.ds(start, t), :]
            out = []
            for a in range(2):
                c1, acc = carry[a]
                z = _dot_nt(qs[a], k2)
                ls, l1m = _sb_block(z, valid)
                suffix = _split_dot(l1m, later)
                w = jnp.exp(ls + suffix + c1)
                if valid is not None:
                    w = jnp.where(valid, w, 0.0)
                out.append((c1 + suffix[:, 0:1] + l1m[:, 0:1], acc + _split_dot(w, v2)))
            return tuple(out)

        init = tuple((jnp.zeros((t, 1), F32), jnp.zeros((t, LANES), F32)) for _ in range(2))
        carry = block(i, init, causal)
        carry = lax.fori_loop(0, i, lambda jj, c: block(i - 1 - jj, c, None), carry)
        o_ref[...] = jnp.where(first, carry[0][1], carry[1][1])

    return pl.pallas_call(
        body, name="attn_fwd", grid=(D_SB // LANES, nq),
        out_shape=jax.ShapeDtypeStruct((s, D_SB), F32),
        in_specs=[pl.BlockSpec((t, LANES), lambda p, i: (i, p)), pl.BlockSpec((s, LANES), lambda p, i: (0, p)),
                  pl.BlockSpec((s, LANES), lambda p, i: (0, p))],
        out_specs=pl.BlockSpec((t, LANES), lambda p, i: (i, p)),
        compiler_params=_cparams(2),
    )(q, k, v)


def _conv_taps(z, halo, rowi):
    z1 = jnp.where(rowi == 0, halo[7:8, :], pltpu.roll(z, 1, 0))
    z2 = jnp.where(rowi == 0, halo[6:7, :], jnp.where(rowi == 1, halo[7:8, :], pltpu.roll(z, 2, 0)))
    return z1, z2


def _merge_branches(x_ref, y_ref, o_ref, c_ref, ch_ref, n1_ref, wg_ref, bg_ref, wv_ref, wsg_ref, wao_ref,
                    cw_ref, wc_ref, tb):
    i = pl.program_id(0)
    h = _rms_fwd(x_ref[...], n1_ref[...])
    g = _sigmoid(_dot(h, wg_ref[...]) + bg_ref[...])
    y256 = y_ref[...]
    val = _dot(y256, wv_ref[...])
    sg = _sigmoid(_dot(y256, wsg_ref[...]))
    ysb = _dot(o_ref[...], wao_ref[...])
    c3 = c_ref[...]
    cb, cc, cx = c3[:, 0:256], c3[:, 256:512], c3[:, 512:768]
    ch = ch_ref[...]
    z = cc * cx
    zh = jnp.where(i > 0, ch[:, 256:512] * ch[:, 512:768], 0.0)
    rowi = lax.broadcasted_iota(jnp.int32, (tb, D_CONV), 0)
    z1, z2 = _conv_taps(z, zh, rowi)
    cw = cw_ref[...]
    cv = cw[0:1, :] * z2 + cw[1:2, :] * z1 + cw[2:3, :] * z
    yc = _dot(cb * cv, wc_ref[...])
    return dict(h=h, g=g, val=val, sg=sg, ysb=ysb, cb=cb, cv=cv, yc=yc)


def _merge_fwd(x, y256, o, c3, w):
    s = x.shape[0]
    tb = _token_tile(s)

    def body(x_ref, y_ref, o_ref, c_ref, ch_ref, n1_ref, wg_ref, bg_ref, wv_ref, wsg_ref, wao_ref, cw_ref, wc_ref,
             wm_ref, n2_ref, x1_ref, m2_ref, mg_ref):
        br = _merge_branches(x_ref, y_ref, o_ref, c_ref, ch_ref, n1_ref, wg_ref, bg_ref, wv_ref, wsg_ref, wao_ref,
                             cw_ref, wc_ref, tb)
        g = br["g"]
        merged = (g[:, 0:1024] * (br["val"] * br["sg"]) + g[:, 1024:2048] * br["ysb"] + g[:, 2048:3072] * br["yc"])
        mg_ref[...] = merged.astype(BF16)
        m2 = _dot(merged, wm_ref[...])
        m2_ref[...] = m2
        x1_ref[...] = x_ref[...] + _rms_fwd(m2, n2_ref[...])

    def tok(wd):
        return pl.BlockSpec((tb, wd), lambda i: (i, 0))

    halo = pl.BlockSpec((8, 3 * D_CONV), lambda i: (jnp.maximum(i * (tb // 8) - 1, 0), 0))
    return pl.pallas_call(
        body, name="merge_fwd", grid=(s // tb,),
        out_shape=[jax.ShapeDtypeStruct((s, D_MODEL), F32), jax.ShapeDtypeStruct((s, D_MODEL), F32),
                   jax.ShapeDtypeStruct((s, D_MODEL), BF16)],
        in_specs=[tok(D_MODEL), tok(D_SSM), tok(D_SB), tok(3 * D_CONV), halo,
                  _full((1, D_MODEL)), _full((D_MODEL, 3 * D_MODEL)), _full((1, 3 * D_MODEL)),
                  _full((D_SSM, D_MODEL)), _full((D_SSM, D_MODEL)), _full((D_SB, D_MODEL)),
                  _full((8, D_CONV)), _full((D_CONV, D_MODEL)), _full((D_MODEL, D_MODEL)), _full((1, D_MODEL))],
        out_specs=[tok(D_MODEL), tok(D_MODEL), tok(D_MODEL)],
        compiler_params=_cparams(1),
    )(x, y256, o, c3, c3, w["n1"], w["w_gate"], w["b_gate"], w["w_glu_val"], w["w_glu_gate"], w["w_attn_out"],
      w["conv_w"], w["w_conv_out"], w["w_mix_out"], w["n2"])


FF_CHUNK = 1024


def _ffn_fwd(x1, n3, w_up, w_dn, n4):
    s = x1.shape[0]
    tb = _token_tile(s)
    nh = D_FF // FF_CHUNK

    def body(x_ref, n3_ref, wu_ref, wd_ref, n4_ref, x2_ref, f_ref, h_s, acc):
        j = pl.program_id(1)

        @pl.when(j == 0)
        def _():
            h_s[...] = _rms_fwd(x_ref[...], n3_ref[...]).astype(BF16)
            acc[...] = jnp.zeros_like(acc)

        a = jnp.maximum(_dot(h_s[...], wu_ref[...]), 0.0)
        acc[...] += _dot(a * a, wd_ref[...])

        @pl.when(j == nh - 1)
        def _():
            f = acc[...]
            f_ref[...] = f
            x2_ref[...] = x_ref[...] + _rms_fwd(f, n4_ref[...])

    tok = pl.BlockSpec((tb, D_MODEL), lambda i, j: (i, 0))
    return pl.pallas_call(
        body, name="ffn_fwd", grid=(s // tb, nh),
        out_shape=[jax.ShapeDtypeStruct((s, D_MODEL), F32)] * 2,
        in_specs=[tok, _full((1, D_MODEL)), pl.BlockSpec((D_MODEL, FF_CHUNK), lambda i, j: (0, j)),
                  pl.BlockSpec((FF_CHUNK, D_MODEL), lambda i, j: (j, 0)), _full((1, D_MODEL))],
        out_specs=[tok, tok],
        scratch_shapes=[pltpu.VMEM((tb, D_MODEL), BF16), pltpu.VMEM((tb, D_MODEL), F32)],
        compiler_params=_cparams(2),
    )(x1, n3, w_up, w_dn, n4)


def _loss_head(y, target):
    s = y.shape[0]
    tb = _token_tile(s)

    def body(y_ref, t_ref, dy_ref, l_ref):
        @pl.when(pl.program_id(0) == 0)
        def _():
            l_ref[...] = jnp.zeros_like(l_ref)

        err = y_ref[...] - t_ref[...]
        dy_ref[...] = err * (1.0 / D_MODEL)
        l_ref[...] += 0.5 * jnp.sum(jnp.mean(err * err, axis=-1, keepdims=True), axis=0, keepdims=True)

    tok = pl.BlockSpec((tb, D_MODEL), lambda i: (i, 0))
    return pl.pallas_call(
        body, name="loss_head", grid=(s // tb,),
        out_shape=[jax.ShapeDtypeStruct((s, D_MODEL), F32), jax.ShapeDtypeStruct((8, LANES), F32)],
        in_specs=[tok, tok], out_specs=[tok, _full((8, LANES))],
        compiler_params=_cparams(1),
    )(y, target)


def _ffn_bwd(dx2, x1, f, n3, w_up, w_dn, n4):
    s = x1.shape[0]
    tb = _token_tile(s)
    nh = D_FF // FF_CHUNK
    nt = s // tb

    def body(dx2_ref, x_ref, f_ref, n3_ref, wu_ref, wd_ref, n4_ref,
             dx1_ref, r_ref, da_ref, h_ref, df_ref, dn3_ref, dn4_ref, acc):
        i = pl.program_id(0)
        j = pl.program_id(1)

        @pl.when((i == 0) & (j == 0))
        def _():
            dn3_ref[...] = jnp.zeros_like(dn3_ref)
            dn4_ref[...] = jnp.zeros_like(dn4_ref)

        @pl.when(j == 0)
        def _():
            h_ref[...] = _rms_fwd(x_ref[...], n3_ref[...]).astype(BF16)
            df, dn4 = _rms_bwd(dx2_ref[...], f_ref[...], n4_ref[...])
            df_ref[...] = df.astype(BF16)
            dn4_ref[...] += dn4
            acc[...] = jnp.zeros_like(acc)

        a = jnp.maximum(_dot(h_ref[...], wu_ref[...]), 0.0)
        r_ref[...] = (a * a).astype(BF16)
        da = (_dot_nt(df_ref[...], wd_ref[...]) * (2.0 * a)).astype(BF16)
        da_ref[...] = da
        acc[...] += _dot_nt(da, wu_ref[...])

        @pl.when(j == nh - 1)
        def _():
            dx, dn3 = _rms_bwd(acc[...], x_ref[...], n3_ref[...])
            dx1_ref[...] = dx2_ref[...] + dx
            dn3_ref[...] += dn3

    tok = pl.BlockSpec((tb, D_MODEL), lambda i, j: (i, 0))
    hid = pl.BlockSpec((tb, FF_CHUNK), lambda i, j: (i, j))
    tok_b = jax.ShapeDtypeStruct((s, D_MODEL), BF16)
    hid_b = jax.ShapeDtypeStruct((s, D_FF), BF16)
    row = jax.ShapeDtypeStruct((1, D_MODEL), F32)
    return pl.pallas_call(
        body, name="ffn_bwd", grid=(nt, nh),
        out_shape=[jax.ShapeDtypeStruct((s, D_MODEL), F32), hid_b, hid_b, tok_b, tok_b, row, row],
        in_specs=[tok, tok, tok, _full((1, D_MODEL)), pl.BlockSpec((D_MODEL, FF_CHUNK), lambda i, j: (0, j)),
                  pl.BlockSpec((FF_CHUNK, D_MODEL), lambda i, j: (j, 0)), _full((1, D_MODEL))],
        out_specs=[tok, hid, hid, tok, tok, _full((1, D_MODEL)), _full((1, D_MODEL))],
        scratch_shapes=[pltpu.VMEM((tb, D_MODEL), F32)],
        compiler_params=_cparams(2),
    )(dx2, x1, f, n3, w_up, w_dn, n4)


def _merge_bwd(dx1, x, m2, y256, o, c3, w):
    s = x.shape[0]
    tb = _token_tile(s)

    def body(dx1_ref, m2_ref, x_ref, y_ref, o_ref, c_ref, ch_ref, n1_ref, wg_ref, bg_ref, wv_ref, wsg_ref, wao_ref,
             cw_ref, wc_ref, wm_ref, n2_ref,
             dhg_ref, h_ref, dgz_ref, dvs_ref, dy_ref, do_ref, dyb_ref, dyc_ref, bcv_ref, dcv_ref, dcb_ref, dm2_ref,
             dbg_ref, dn2_ref):
        @pl.when(pl.program_id(0) == 0)
        def _():
            dbg_ref[...] = jnp.zeros_like(dbg_ref)
            dn2_ref[...] = jnp.zeros_like(dn2_ref)

        dm2, dn2 = _rms_bwd(dx1_ref[...], m2_ref[...], n2_ref[...])
        dn2_ref[...] += dn2
        dm2_ref[...] = dm2.astype(BF16)
        dmg = _dot_nt(dm2, wm_ref[...])
        br = _merge_branches(x_ref, y_ref, o_ref, c_ref, ch_ref, n1_ref, wg_ref, bg_ref, wv_ref, wsg_ref, wao_ref,
                             cw_ref, wc_ref, tb)
        g = br["g"]
        g1, g2, g3 = g[:, 0:1024], g[:, 1024:2048], g[:, 2048:3072]
        val, sg = br["val"], br["sg"]
        h_ref[...] = br["h"].astype(BF16)
        dgz = jnp.concatenate([dmg * (val * sg) * g1 * (1.0 - g1), dmg * br["ysb"] * g2 * (1.0 - g2),
                               dmg * br["yc"] * g3 * (1.0 - g3)], axis=1)
        dbg_ref[...] += jnp.sum(dgz, axis=0, keepdims=True)
        dgz_ref[...] = dgz.astype(BF16)
        dhg_ref[...] = _dot_nt(dgz, wg_ref[...])
        dys = dmg * g1
        dval = dys * sg
        dsg = dys * val * sg * (1.0 - sg)
        dvs_ref[:, 0:1024] = dval.astype(BF16)
        dvs_ref[:, 1024:2048] = dsg.astype(BF16)
        dy_ref[...] = _dot_nt(dval, wv_ref[...]) + _dot_nt(dsg, wsg_ref[...])
        dyb = (dmg * g2).astype(BF16)
        dyb_ref[...] = dyb
        do_ref[...] = _dot_nt(dyb, wao_ref[...]).astype(BF16)
        dyc = (dmg * g3).astype(BF16)
        dyc_ref[...] = dyc
        dcq = _dot_nt(dyc, wc_ref[...])
        bcv_ref[...] = (br["cb"] * br["cv"]).astype(BF16)
        dcb_ref[...] = dcq * br["cv"]
        dcv_ref[...] = dcq * br["cb"]

    def tok(wd):
        return pl.BlockSpec((tb, wd), lambda i: (i, 0))

    def out(wd, dt):
        return jax.ShapeDtypeStruct((s, wd), dt)

    halo = pl.BlockSpec((8, 3 * D_CONV), lambda i: (jnp.maximum(i * (tb // 8) - 1, 0), 0))
    return pl.pallas_call(
        body, name="merge_bwd", grid=(s // tb,),
        out_shape=[out(D_MODEL, F32), out(D_MODEL, BF16), out(3 * D_MODEL, BF16), out(2 * D_MODEL, BF16),
                   out(D_SSM, F32), out(D_SB, BF16), out(D_MODEL, BF16), out(D_MODEL, BF16), out(D_CONV, BF16),
                   out(D_CONV, F32), out(D_CONV, F32), out(D_MODEL, BF16),
                   jax.ShapeDtypeStruct((1, 3 * D_MODEL), F32), jax.ShapeDtypeStruct((1, D_MODEL), F32)],
        in_specs=[tok(D_MODEL), tok(D_MODEL), tok(D_MODEL), tok(D_SSM), tok(D_SB), tok(3 * D_CONV), halo,
                  _full((1, D_MODEL)), _full((D_MODEL, 3 * D_MODEL)), _full((1, 3 * D_MODEL)),
                  _full((D_SSM, D_MODEL)), _full((D_SSM, D_MODEL)), _full((D_SB, D_MODEL)),
                  _full((8, D_CONV)), _full((D_CONV, D_MODEL)), _full((D_MODEL, D_MODEL)), _full((1, D_MODEL))],
        out_specs=[tok(D_MODEL), tok(D_MODEL), tok(3 * D_MODEL), tok(2 * D_MODEL), tok(D_SSM), tok(D_SB),
                   tok(D_MODEL), tok(D_MODEL), tok(D_CONV), tok(D_CONV), tok(D_CONV), tok(D_MODEL),
                   _full((1, 3 * D_MODEL)), _full((1, D_MODEL))],
        compiler_params=_cparams(1),
    )(dx1, m2, x, y256, o, c3, c3, w["n1"], w["w_gate"], w["b_gate"], w["w_glu_val"], w["w_glu_gate"],
      w["w_attn_out"], w["conv_w"], w["w_conv_out"], w["w_mix_out"], w["n2"])


def _ssm_bwd(u, ypre, dy, states, ab_re, ab_im, bb_re, bb_im, cc_re, cc_im, d_skip):
    s = u.shape[0]
    ts = _token_tile(s)
    nt = s // ts

    def body(u_ref, yp_ref, dy_ref, st_ref, ar_ref, ai_ref, bbr_ref, bbi_ref, ccr_ref, cci_ref, d_ref,
             du_ref, dar_ref, dai_ref, dbbr_ref, dbbi_ref, dccr_ref, dcci_ref, dd_ref,
             bur, bui, hr_s, hi_s, pr_s, pi_s, lr_s, li_s, cr, ci):
        @pl.when(pl.program_id(0) == 0)
        def _():
            cr[...] = jnp.zeros_like(cr)
            ci[...] = jnp.zeros_like(ci)
            for ref in (dar_ref, dai_ref, dbbr_ref, dbbi_ref, dccr_ref, dcci_ref, dd_ref):
                ref[...] = jnp.zeros_like(ref)

        u = u_ref[...]
        ub = u.astype(BF16)
        bur[...] = _dot(ub, bbr_ref[...])
        bui[...] = _dot(ub, bbi_ref[...])
        ar = ar_ref[...]
        ai = ai_ref[...]

        def fwd_step(t, carry):
            hr, hi = carry
            pr_s[pl.ds(t, 1), :] = hr
            pi_s[pl.ds(t, 1), :] = hi
            nr = ar * hr - ai * hi + bur[pl.ds(t, 1), :]
            ni = ar * hi + ai * hr + bui[pl.ds(t, 1), :]
            hr_s[pl.ds(t, 1), :] = nr
            hi_s[pl.ds(t, 1), :] = ni
            return nr, ni

        lax.fori_loop(0, ts, fwd_step, (st_ref[0, 0:1, :], st_ref[0, 1:2, :]), unroll=8)

        dyp = dy_ref[...] * _gelu_grad(yp_ref[...])
        dypb = dyp.astype(BF16)
        lr_s[...] = _dot(dypb, ccr_ref[...])
        li_s[...] = -_dot(dypb, cci_ref[...])

        def bwd_step(tt, carry):
            t = ts - 1 - tt
            nr, ni = carry
            qr = lr_s[pl.ds(t, 1), :] + ar * nr + ai * ni
            qi = li_s[pl.ds(t, 1), :] + ar * ni - ai * nr
            lr_s[pl.ds(t, 1), :] = qr
            li_s[pl.ds(t, 1), :] = qi
            return qr, qi

        nr, ni = lax.fori_loop(0, ts, bwd_step, (cr[...], ci[...]), unroll=8)
        cr[...] = nr
        ci[...] = ni
        lam_r = lr_s[...]
        lam_i = li_s[...]
        pr = pr_s[...]
        pi = pi_s[...]
        dar_ref[...] += jnp.sum(lam_r * pr + lam_i * pi, axis=0, keepdims=True)
        dai_ref[...] += jnp.sum(lam_i * pr - lam_r * pi, axis=0, keepdims=True)
        lrb = lam_r.astype(BF16)
        lib = lam_i.astype(BF16)
        du_ref[...] = _dot_nt(lrb, bbr_ref[...]) + _dot_nt(lib, bbi_ref[...]) + d_ref[...] * dyp
        dbbr_ref[...] += _dot_tn(ub, lrb)
        dbbi_ref[...] += _dot_tn(ub, lib)
        dccr_ref[...] += _dot_tn(dypb, hr_s[...])
        dcci_ref[...] -= _dot_tn(dypb, hi_s[...])
        dd_ref[...] += jnp.sum(dyp * u, axis=0, keepdims=True)

    tok = pl.BlockSpec((ts, D_SSM), lambda i: (nt - 1 - i, 0))
    row = _full((1, D_STATE))
    mat = _full((D_SSM, D_STATE))
    row_o = jax.ShapeDtypeStruct((1, D_STATE), F32)
    mat_o = jax.ShapeDtypeStruct((D_SSM, D_STATE), F32)
    return pl.pallas_call(
        body, name="ssm_bwd", grid=(nt,),
        out_shape=[jax.ShapeDtypeStruct((s, D_SSM), F32), row_o, row_o, mat_o, mat_o, mat_o, mat_o,
                   jax.ShapeDtypeStruct((1, D_SSM), F32)],
        in_specs=[tok, tok, tok, pl.BlockSpec((1, 2, D_STATE), lambda i: (nt - 1 - i, 0, 0)),
                  row, row, mat, mat, mat, mat, _full((1, D_SSM))],
        out_specs=[tok, row, row, mat, mat, mat, mat, _full((1, D_SSM))],
        scratch_shapes=[pltpu.VMEM((ts, D_STATE), F32)] * 8 + [pltpu.VMEM((1, D_STATE), F32)] * 2,
        compiler_params=_cparams(1),
    )(u, ypre, dy, states, ab_re, ab_im, bb_re, bb_im, cc_re, cc_im, d_skip)


def _attn_bwd(q, k, v, do, o):
    s = q.shape[0]
    t = _attn_tile(s)
    nq = s // t
    n_pairs = D_SB // LANES

    def body(q_ref, do_ref, o_ref, k_ref, v_ref, dq_ref, dk_hbm, dv_hbm, dk_s, dv_s, sems):
        p = pl.program_id(0)
        i = pl.program_id(1)

        @pl.when(i == 0)
        def _():
            dk_s[...] = jnp.zeros_like(dk_s)
            dv_s[...] = jnp.zeros_like(dv_s)

        lane = lax.broadcasted_iota(jnp.int32, (t, LANES), 1)
        first = lane < HEAD_DIM
        row = lax.broadcasted_iota(jnp.int32, (t, t), 0)
        col = lax.broadcasted_iota(jnp.int32, (t, t), 1)
        later = (row > col).astype(BF16)
        not_before = (row >= col).astype(BF16)
        causal = col < row
        q2 = q_ref[...]
        do2 = do_ref[...]
        zero = jnp.zeros_like(q2)
        qs = (jnp.where(first, q2, zero), jnp.where(first, zero, q2))
        dos = (jnp.where(first, do2, zero), jnp.where(first, zero, do2))
        prod = do2.astype(F32) * o_ref[...]
        deltas = (jnp.sum(jnp.where(first, prod, 0.0), axis=1, keepdims=True),
                  jnp.sum(jnp.where(first, 0.0, prod), axis=1, keepdims=True))

        def block(j, carry, valid):
            start = pl.multiple_of(j * t, t)
            k2 = k_ref[pl.ds(start, t), :]
            v2 = v_ref[pl.ds(start, t), :]
            out, dks, dvs = [], [], []
            for a in range(2):
                c1, c2, dq = carry[a]
                z = _dot_nt(qs[a], k2)
                ls, l1m = _sb_block(z, valid)
                suffix = _split_dot(l1m, later)
                w = jnp.exp(ls + suffix + c1)
                if valid is not None:
                    w = jnp.where(valid, w, 0.0)
                dl = w * _dot_nt(dos[a], v2)
                dl_suffix = _split_dot(dl, not_before)
                before = deltas[a] - c2 - dl_suffix
                dz = dl - jnp.exp(ls) * (dl + before)
                if valid is not None:
                    dz = jnp.where(valid, dz, 0.0)
                dzb = dz.astype(BF16)
                dks.append(_dot_tn(dzb, q2))
                dvs.append(_dot_tn(w.astype(BF16), do2))
                out.append((c1 + suffix[:, 0:1] + l1m[:, 0:1], c2 + dl_suffix[:, 0:1], dq + _dot(dzb, k2)))
            dk_s[pl.ds(start, t), :] += jnp.where(first, dks[0], dks[1])
            dv_s[pl.ds(start, t), :] += jnp.where(first, dvs[0], dvs[1])
            return tuple(out)

        init = tuple((jnp.zeros((t, 1), F32), jnp.zeros((t, 1), F32), jnp.zeros((t, LANES), F32)) for _ in range(2))
        carry = block(i, init, causal)
        carry = lax.fori_loop(0, i, lambda jj, c: block(i - 1 - jj, c, None), carry)
        dq_ref[...] = Q_SCALE * jnp.where(first, carry[0][2], carry[1][2])

        @pl.when(i == nq - 1)
        def _():
            ck = pltpu.make_async_copy(dk_s, dk_hbm.at[p], sems.at[0])
            cv = pltpu.make_async_copy(dv_s, dv_hbm.at[p], sems.at[1])
            ck.start()
            cv.start()
            ck.wait()
            cv.wait()

    blk = pl.BlockSpec((t, LANES), lambda p, i: (i, p))
    seq = pl.BlockSpec((s, LANES), lambda p, i: (0, p))
    pairs = jax.ShapeDtypeStruct((n_pairs, s, LANES), F32)
    return pl.pallas_call(
        body, name="attn_bwd", grid=(n_pairs, nq),
        out_shape=[jax.ShapeDtypeStruct((s, D_SB), F32), pairs, pairs],
        in_specs=[blk, blk, blk, seq, seq],
        out_specs=[blk, pl.BlockSpec(memory_space=pl.ANY), pl.BlockSpec(memory_space=pl.ANY)],
        scratch_shapes=[pltpu.VMEM((s, LANES), F32), pltpu.VMEM((s, LANES), F32), pltpu.SemaphoreType.DMA((2,))],
        compiler_params=_cparams(2),
    )(q, do, o, k, v)


def _pre_bwd(dres, dhg, x, du, dq, dk, dv, dcb, dcv, c3, n1, w_in, conv_w):
    s = x.shape[0]
    tb = _token_tile(s)
    nt = s // tb

    def body(dres_ref, dhg_ref, x_ref, du_ref, dq_ref, dk_ref, dv_ref, dcb_ref, dcv_ref, dnext_ref, c_ref, ch_ref,
             n1_ref, w_ref, cw_ref, dx_ref, dp_ref, dn1_ref, dcw_ref):
        i = pl.program_id(0)

        @pl.when(i == 0)
        def _():
            dn1_ref[...] = jnp.zeros_like(dn1_ref)
            dcw_ref[...] = jnp.zeros_like(dcw_ref)

        c3 = c_ref[...]
        cc, cx = c3[:, 256:512], c3[:, 512:768]
        ch = ch_ref[...]
        z = cc * cx
        zh = jnp.where(i > 0, ch[:, 256:512] * ch[:, 512:768], 0.0)
        rowi = lax.broadcasted_iota(jnp.int32, (tb, D_CONV), 0)
        z1, z2 = _conv_taps(z, zh, rowi)
        dcv = dcv_ref[...]
        nxt = jnp.where(i < nt - 1, dnext_ref[...], 0.0)
        d1 = jnp.where(rowi == tb - 1, nxt[0:1, :], pltpu.roll(dcv, tb - 1, 0))
        d2 = jnp.where(rowi == tb - 2, nxt[0:1, :], jnp.where(rowi == tb - 1, nxt[1:2, :], pltpu.roll(dcv, tb - 2, 0)))
        cw = cw_ref[...]
        dz = cw[2:3, :] * dcv + cw[1:2, :] * d1 + cw[0:1, :] * d2
        dcw_ref[0:1, :] += jnp.sum(dcv * z2, axis=0, keepdims=True)
        dcw_ref[1:2, :] += jnp.sum(dcv * z1, axis=0, keepdims=True)
        dcw_ref[2:3, :] += jnp.sum(dcv * z, axis=0, keepdims=True)
        dp_ref[:, 0:256] = du_ref[...].astype(BF16)
        dp_ref[:, 256:768] = dq_ref[...].astype(BF16)
        dp_ref[:, 768:1280] = dk_ref[...].astype(BF16)
        dp_ref[:, 1280:1792] = dv_ref[...].astype(BF16)
        dp_ref[:, 1792:2048] = dcb_ref[...].astype(BF16)
        dp_ref[:, 2048:2304] = (dz * cx).astype(BF16)
        dp_ref[:, 2304:2560] = (dz * cc).astype(BF16)
        dh = dhg_ref[...] + _dot_nt(dp_ref[...], w_ref[...])
        dx, dn1 = _rms_bwd(dh, x_ref[...], n1_ref[...])
        dx_ref[...] = dres_ref[...] + dx
        dn1_ref[...] += dn1

    def tok(wd):
        return pl.BlockSpec((tb, wd), lambda i: (i, 0))

    halo_prev = pl.BlockSpec((8, 3 * D_CONV), lambda i: (jnp.maximum(i * (tb // 8) - 1, 0), 0))
    halo_next = pl.BlockSpec((8, D_CONV), lambda i: (jnp.minimum((i + 1) * (tb // 8), s // 8 - 1), 0))
    return pl.pallas_call(
        body, name="pre_bwd", grid=(nt,),
        out_shape=[jax.ShapeDtypeStruct((s, D_MODEL), F32), jax.ShapeDtypeStruct((s, D_IN), BF16),
                   jax.ShapeDtypeStruct((1, D_MODEL), F32), jax.ShapeDtypeStruct((8, D_CONV), F32)],
        in_specs=[tok(D_MODEL), tok(D_MODEL), tok(D_MODEL), tok(D_SSM), tok(D_SB), tok(D_SB), tok(D_SB),
                  tok(D_CONV), tok(D_CONV), halo_next, tok(3 * D_CONV), halo_prev,
                  _full((1, D_MODEL)), _full((D_MODEL, D_IN)), _full((8, D_CONV))],
        out_specs=[tok(D_MODEL), tok(D_IN), _full((1, D_MODEL)), _full((8, D_CONV))],
        compiler_params=_cparams(1),
    )(dres, dhg, x, du, dq, dk, dv, dcb, dcv, dcv, c3, c3, n1, w_in, conv_w)


def _padded(n):
    return -(-n // PIECE_ALIGN) * PIECE_ALIGN


def _pack(pieces, total_rows=None):
    flat = []
    for a in pieces:
        a = a.reshape(-1)
        flat.append(jnp.pad(a, (0, _padded(a.size) - a.size)))
    out = jnp.concatenate(flat)
    if total_rows is not None:
        out = jnp.pad(out, (0, total_rows * LANES - out.size))
    return out.reshape(-1, LANES)


def _unpack(flat, shapes, lead=()):
    flat = flat.reshape(lead + (-1,))
    out, off = [], 0
    for shp in shapes:
        n = math.prod(shp)
        out.append(flat[..., off:off + n].reshape(lead + tuple(shp)))
        off += _padded(n)
    return out


def _to_shards(full, axis):
    shp = full.shape
    cut = shp[:axis] + (N_DEV, shp[axis] // N_DEV) + shp[axis + 1:]
    return jnp.moveaxis(full.reshape(cut), axis, 0)


def _from_shards(shards, axis):
    moved = jnp.moveaxis(shards, 0, axis)
    shp = moved.shape
    return moved.reshape(shp[:axis] + (shp[axis] * shp[axis + 1],) + shp[axis + 2:])


def _expand_groups(compact):
    rows = lax.broadcasted_iota(jnp.int32, (D_SSM, D_STATE), 0) // GROUP_CH
    cols = lax.broadcasted_iota(jnp.int32, (D_SSM, D_STATE), 1) // N_STATE
    return jnp.where(rows == cols, jnp.tile(compact, (N_GROUPS, 1)), 0.0)


def _collect_groups(dense):
    rows = lax.broadcasted_iota(jnp.int32, (D_SSM, D_STATE), 0) // GROUP_CH
    cols = lax.broadcasted_iota(jnp.int32, (D_SSM, D_STATE), 1) // N_STATE
    return jnp.where(rows == cols, dense, 0.0).reshape(N_GROUPS, GROUP_CH, D_STATE).sum(axis=0)


def kernel(x, norm_mix_pre, norm_mix_post, w_in, w_gate, b_gate, ssm_a_re, ssm_a_im, ssm_log_dt, ssm_b_re, ssm_b_im, ssm_c_re, ssm_c_im, ssm_d, w_glu_val, w_glu_gate, w_attn_out, conv_w, w_conv_out, w_mix_out, norm_ffn_pre, norm_ffn_post, w_ffn_up, w_ffn_down, loss_target, m_norm_mix_pre, m_norm_mix_post, m_w_in, m_w_gate, m_b_gate, m_ssm_a_re, m_ssm_a_im, m_ssm_log_dt, m_ssm_b_re, m_ssm_b_im, m_ssm_c_re, m_ssm_c_im, m_ssm_d, m_w_glu_val, m_w_glu_gate, m_w_attn_out, m_conv_w, m_w_conv_out, m_w_mix_out, m_norm_ffn_pre, m_norm_ffn_post, m_w_ffn_up, m_w_ffn_down, v_norm_mix_pre, v_norm_mix_post, v_w_in, v_w_gate, v_b_gate, v_ssm_a_re, v_ssm_a_im, v_ssm_log_dt, v_ssm_b_re, v_ssm_b_im, v_ssm_c_re, v_ssm_c_im, v_ssm_d, v_w_glu_val, v_w_glu_gate, v_w_attn_out, v_conv_w, v_w_conv_out, v_w_mix_out, v_norm_ffn_pre, v_norm_ffn_post, v_w_ffn_up, v_w_ffn_down):
    args = dict(locals())
    wts = {n: args[n] for n in WEIGHTS}
    mom = {n: args["m_" + n] for n in WEIGHTS}
    vel = {n: args["v_" + n] for n in WEIGHTS}
    seq = x.shape[1]
    x0 = x.reshape(seq, D_MODEL)
    target = loss_target.reshape(seq, D_MODEL)

    pieces = []
    for n in SHARDED:
        if n == "conv_w":
            hi = wts[n].astype(BF16)
            pieces += [hi, (wts[n] - hi.astype(F32)).astype(BF16)]
        else:
            pieces.append(wts[n].astype(BF16))
    shard_shapes = [p.shape for p in pieces]
    gathered = _unpack(_all_gather(_pack(pieces)), shard_shapes, lead=(N_DEV,))
    full = {}
    it = iter(gathered)
    for n in SHARDED:
        if n == "conv_w":
            full[n] = _from_shards(next(it).astype(F32) + next(it).astype(F32), SHARD_AXIS[n])
        else:
            full[n] = _from_shards(next(it), SHARD_AXIS[n])

    def layer_weights(l):
        return dict(
            n1=norm_mix_pre[l][None], n2=norm_mix_post[l][None], n3=norm_ffn_pre[l][None], n4=norm_ffn_post[l][None],
            w_in=full["w_in"][l], w_gate=full["w_gate"][l], b_gate=b_gate[l][None],
            w_glu_val=full["w_glu_val"][l], w_glu_gate=full["w_glu_gate"][l], w_attn_out=full["w_attn_out"][l],
            conv_w=jnp.pad(full["conv_w"][l], ((0, 5), (0, 0))), w_conv_out=full["w_conv_out"][l],
            w_mix_out=full["w_mix_out"][l], w_ffn_up=full["w_ffn_up"][l], w_ffn_down=full["w_ffn_down"][l],
            lr=ssm_a_re[l].reshape(1, D_STATE), li=ssm_a_im[l].reshape(1, D_STATE),
            ldt=jnp.repeat(ssm_log_dt[l], N_STATE).reshape(1, D_STATE),
            br=jnp.transpose(ssm_b_re[l], (2, 0, 1)).reshape(GROUP_CH, D_STATE),
            bi=jnp.transpose(ssm_b_im[l], (2, 0, 1)).reshape(GROUP_CH, D_STATE),
            cc_re=_expand_groups(jnp.transpose(ssm_c_re[l], (1, 0, 2)).reshape(GROUP_CH, D_STATE)).astype(BF16),
            cc_im=_expand_groups(jnp.transpose(ssm_c_im[l], (1, 0, 2)).reshape(GROUP_CH, D_STATE)).astype(BF16),
            d_skip=ssm_d[l][None],
        )

    saved = []
    xin = x0
    for l in range(DEPTH):
        w = layer_weights(l)
        ab_re, ab_im, bbr, bbi = _ssm_params_fwd(w["lr"], w["li"], w["ldt"], w["br"], w["bi"])
        w.update(ab_re=ab_re, ab_im=ab_im, bb_re=_expand_groups(bbr).astype(BF16), bb_im=_expand_groups(bbi).astype(BF16))
        u, q, k, v, c3 = _pre_fwd(xin, w["n1"], w["w_in"])
        y256, ypre, states = _ssm_fwd(u, ab_re, ab_im, w["bb_re"], w["bb_im"], w["cc_re"], w["cc_im"], w["d_skip"])
        o = _attn_fwd(q, k, v)
        x1, m2, merged = _merge_fwd(xin, y256, o, c3, w)
        x2, f = _ffn_fwd(x1, w["n3"], w["w_ffn_up"], w["w_ffn_down"], w["n4"])
        saved.append(dict(w=w, x0=xin, u=u, q=q, k=k, v=v, c3=c3, y256=y256, ypre=ypre, states=states, o=o, x1=x1,
                          m2=m2, merged=merged, f=f))
        xin = x2

    dxo, loss_part = _loss_head(xin, target)
    loss = lax.psum(loss_part[0, 0], ("x", "y", "c"))

    grads = {n: [None] * DEPTH for n in WEIGHTS}
    for l in reversed(range(DEPTH)):
        sv = saved[l]
        w = sv["w"]
        dx1, r, da, h2, df, dn3, dn4 = _ffn_bwd(dxo, sv["x1"], sv["f"], w["n3"], w["w_ffn_up"], w["w_ffn_down"], w["n4"])
        grads["w_ffn_down"][l] = _tn_matmul(r, df, "grad_w_ffn_down")
        grads["w_ffn_up"][l] = _tn_matmul(h2, da, "grad_w_ffn_up")
        grads["norm_ffn_pre"][l] = dn3[0]
        grads["norm_ffn_post"][l] = dn4[0]
        (dhg, hb, dgz, dvs, dy256, do, dyb, dyc, bcv, dcv, dcb, dm2, dbg, dn2) = _merge_bwd(
            dx1, sv["x0"], sv["m2"], sv["y256"], sv["o"], sv["c3"], w)
        grads["w_mix_out"][l] = _tn_matmul(sv["merged"], dm2, "grad_w_mix_out")
        grads["w_gate"][l] = _tn_matmul(hb, dgz, "grad_w_gate")
        dglu = _tn_matmul(sv["y256"], dvs, "grad_w_glu")
        grads["w_glu_val"][l] = dglu[:, :D_MODEL]
        grads["w_glu_gate"][l] = dglu[:, D_MODEL:]
        grads["w_attn_out"][l] = _tn_matmul(sv["o"], dyb, "grad_w_attn_out")
        grads["w_conv_out"][l] = _tn_matmul(bcv, dyc, "grad_w_conv_out")
        grads["b_gate"][l] = dbg[0]
        grads["norm_mix_post"][l] = dn2[0]
        du, dar, dai, dbbr, dbbi, dccr, dcci, dd = _ssm_bwd(
            sv["u"], sv["ypre"], dy256, sv["states"], w["ab_re"], w["ab_im"], w["bb_re"], w["bb_im"],
            w["cc_re"], w["cc_im"], w["d_skip"])
        glr, gli, gdt, gbr, gbi = _ssm_params_bwd(w["lr"], w["li"], w["ldt"], w["br"], w["bi"], dar, dai,
                                                   _collect_groups(dbbr), _collect_groups(dbbi))
        grads["ssm_a_re"][l] = glr.reshape(N_GROUPS, N_STATE)
        grads["ssm_a_im"][l] = gli.reshape(N_GROUPS, N_STATE)
        grads["ssm_log_dt"][l] = gdt[:, 0]
        grads["ssm_b_re"][l] = jnp.transpose(gbr.reshape(GROUP_CH, N_GROUPS, N_STATE), (1, 2, 0))
        grads["ssm_b_im"][l] = jnp.transpose(gbi.reshape(GROUP_CH, N_GROUPS, N_STATE), (1, 2, 0))
        grads["ssm_c_re"][l] = jnp.transpose(_collect_groups(dccr).reshape(GROUP_CH, N_GROUPS, N_STATE), (1, 0, 2))
        grads["ssm_c_im"][l] = jnp.transpose(_collect_groups(dcci).reshape(GROUP_CH, N_GROUPS, N_STATE), (1, 0, 2))
        grads["ssm_d"][l] = dd[0]
        dq, dk4, dv4 = _attn_bwd(sv["q"], sv["k"], sv["v"], do, sv["o"])
        dk = jnp.transpose(dk4, (1, 0, 2)).reshape(seq, D_SB)
        dv = jnp.transpose(dv4, (1, 0, 2)).reshape(seq, D_SB)
        dxo, dp, dn1, dcw = _pre_bwd(dx1, dhg, sv["x0"], du, dq, dk, dv, dcb, dcv, sv["c3"], w["n1"], w["w_in"], w["conv_w"])
        grads["w_in"][l] = _tn_matmul(hb, dp, "grad_w_in")
        grads["norm_mix_pre"][l] = dn1[0]
        grads["conv_w"][l] = dcw[0:3]
    grad_x = dxo.reshape(x.shape)
    gfull = {n: jnp.stack(grads[n]) for n in WEIGHTS}

    rep_parts = _pack([gfull[n] for n in REPLICATED])
    sh_parts = jnp.stack([_pack([_to_shards(gfull[n], SHARD_AXIS[n])[d] for n in SHARDED]) for d in range(N_DEV)])
    rows = sh_parts.shape[1] + rep_parts.shape[0]
    total_rows = -(-rows // 512) * 512
    parts = jnp.concatenate([sh_parts, jnp.broadcast_to(rep_parts, (N_DEV,) + rep_parts.shape),
                             jnp.zeros((N_DEV, total_rows - rows, LANES), F32)], axis=1)
    recv = _all_to_all(parts)

    def mine(d):
        sh = _pack([d[n] for n in SHARDED])
        rp = _pack([d[n] for n in REPLICATED])
        return jnp.concatenate([sh, rp, jnp.zeros((total_rows - rows, LANES), F32)], axis=0)

    flat_out = _adamw(recv, mine(wts), mine(mom), mine(vel))
    n_sh_rows = sh_parts.shape[1]
    results = []
    for fo in flat_out:
        sh = _unpack(fo[:n_sh_rows], [wts[n].shape for n in SHARDED])
        rp = _unpack(fo[n_sh_rows:rows], [wts[n].shape for n in REPLICATED])
        by_name = dict(zip(SHARDED, sh))
        by_name.update(zip(REPLICATED, rp))
        results.append([by_name[n] for n in WEIGHTS])
    g_out, d_out, m_out, v_out = results
    return (loss, grad_x, *g_out, *d_out, *m_out, *v_out)
```

```python
import functools
import math

import jax
import jax.numpy as jnp
from jax import lax
from jax.experimental import pallas as pl
from jax.experimental.pallas import tpu as pltpu

F32 = jnp.float32
BF16 = jnp.bfloat16

N_DEV = 8
D_MODEL = 1024
DEPTH = 2
D_SSM = 256
N_GROUPS = 16
GROUP_CH = 16
N_STATE = 64
D_STATE = N_GROUPS * N_STATE
D_SB = 512
HEAD_DIM = 64
D_CONV = 256
D_IN = 2560
D_FF = 4096
EPS = 1e-6
Q_SCALE = HEAD_DIM ** -0.5
LOG2E = math.log2(math.e)

ADAM_LR = 0.001
ADAM_B1 = 0.9
ADAM_B2 = 0.999
ADAM_EPS = 1e-08
ADAM_WD = 0.01
ADAM_STEP = 10

LANES = 128
PIECE_ALIGN = 2048
VMEM_LIMIT = 56 * 1024 * 1024

WEIGHTS = ['norm_mix_pre', 'norm_mix_post', 'w_in', 'w_gate', 'b_gate', 'ssm_a_re', 'ssm_a_im', 'ssm_log_dt',
           'ssm_b_re', 'ssm_b_im', 'ssm_c_re', 'ssm_c_im', 'ssm_d', 'w_glu_val', 'w_glu_gate', 'w_attn_out',
           'conv_w', 'w_conv_out', 'w_mix_out', 'norm_ffn_pre', 'norm_ffn_post', 'w_ffn_up', 'w_ffn_down']
SHARD_AXIS = {'w_in': 2, 'w_gate': 2, 'w_glu_val': 2, 'w_glu_gate': 2, 'w_attn_out': 2, 'conv_w': 2,
              'w_conv_out': 2, 'w_mix_out': 1, 'w_ffn_up': 2, 'w_ffn_down': 1}
SHARDED = [n for n in WEIGHTS if n in SHARD_AXIS]
REPLICATED = [n for n in WEIGHTS if n not in SHARD_AXIS]


def _cparams(n_grid):
    return pltpu.CompilerParams(dimension_semantics=("arbitrary",) * n_grid, vmem_limit_bytes=VMEM_LIMIT)


def _dot(a, b):
    return jnp.dot(a.astype(BF16), b.astype(BF16), preferred_element_type=F32)


def _dot_nt(a, b):
    return lax.dot_general(a.astype(BF16), b.astype(BF16), (((1,), (1,)), ((), ())), preferred_element_type=F32)


def _dot_tn(a, b):
    return lax.dot_general(a.astype(BF16), b.astype(BF16), (((0,), (0,)), ((), ())), preferred_element_type=F32)


def _split_dot(a, b):
    hi = a.astype(BF16)
    lo = (a - hi.astype(F32)).astype(BF16)
    return jnp.dot(hi, b, preferred_element_type=F32) + jnp.dot(lo, b, preferred_element_type=F32)


def _rms_fwd(x, g):
    r = lax.rsqrt(jnp.mean(x * x, axis=-1, keepdims=True) + EPS)
    return x * r * g


def _rms_bwd(dy, x, g):
    r = lax.rsqrt(jnp.mean(x * x, axis=-1, keepdims=True) + EPS)
    dyg = dy * g
    dx = r * dyg - x * (r * r * r) * jnp.mean(dyg * x, axis=-1, keepdims=True)
    dg = jnp.sum(dy * x * r, axis=0, keepdims=True)
    return dx, dg


def _sigmoid(x):
    return 1.0 / (1.0 + jnp.exp(-x))


_GELU_C = math.sqrt(2.0 / math.pi)


def _gelu(y):
    return 0.5 * y * (1.0 + jnp.tanh(_GELU_C * (y + 0.044715 * y * y * y)))


def _gelu_grad(y):
    t = jnp.tanh(_GELU_C * (y + 0.044715 * y * y * y))
    return 0.5 * (1.0 + t) + 0.5 * y * (1.0 - t * t) * _GELU_C * (1.0 + 3.0 * 0.044715 * y * y)


def _full(shape):
    return pl.BlockSpec(shape, lambda *_: (0,) * len(shape))


def _peer(x, y, c, k):
    px = 1 - x if (k >> 2) & 1 else x
    py = 1 - y if (k >> 1) & 1 else y
    pc = 1 - c if k & 1 else c
    return px, py, pc


def _all_gather(shard):
    rows, lanes = shard.shape

    def body(src_ref, out_ref, send_sems, recv_sems, local_sem):
        x, y, c = lax.axis_index("x"), lax.axis_index("y"), lax.axis_index("c")
        me = 4 * x + 2 * y + c
        mine = pltpu.make_async_copy(src_ref, out_ref.at[me], local_sem)
        mine.start()
        copies = []
        for k in range(1, N_DEV):
            cp = pltpu.make_async_remote_copy(
                src_ref=src_ref, dst_ref=out_ref.at[me],
                send_sem=send_sems.at[k - 1], recv_sem=recv_sems.at[k - 1],
                device_id=_peer(x, y, c, k), device_id_type=pl.DeviceIdType.MESH)
            cp.start()
            copies.append(cp)
        for cp in copies:
            cp.wait()
        mine.wait()

    return pl.pallas_call(
        body, name="weights_all_gather",
        out_shape=jax.ShapeDtypeStruct((N_DEV, rows, lanes), shard.dtype),
        in_specs=[pl.BlockSpec(memory_space=pl.ANY)],
        out_specs=pl.BlockSpec(memory_space=pl.ANY),
        scratch_shapes=[pltpu.SemaphoreType.DMA((N_DEV - 1,)), pltpu.SemaphoreType.DMA((N_DEV - 1,)),
                        pltpu.SemaphoreType.DMA],
    )(shard)


def _all_to_all(parts):
    n, rows, lanes = parts.shape

    def body(src_ref, out_ref, send_sems, recv_sems, local_sem):
        x, y, c = lax.axis_index("x"), lax.axis_index("y"), lax.axis_index("c")
        me = 4 * x + 2 * y + c
        mine = pltpu.make_async_copy(src_ref.at[me], out_ref.at[me], local_sem)
        mine.start()
        copies = []
        for k in range(1, N_DEV):
            px, py, pc = _peer(x, y, c, k)
            cp = pltpu.make_async_remote_copy(
                src_ref=src_ref.at[4 * px + 2 * py + pc], dst_ref=out_ref.at[me],
                send_sem=send_sems.at[k - 1], recv_sem=recv_sems.at[k - 1],
                device_id=(px, py, pc), device_id_type=pl.DeviceIdType.MESH)
            cp.start()
            copies.append(cp)
        for cp in copies:
            cp.wait()
        mine.wait()

    return pl.pallas_call(
        body, name="grads_all_to_all",
        out_shape=jax.ShapeDtypeStruct((N_DEV, rows, lanes), parts.dtype),
        in_specs=[pl.BlockSpec(memory_space=pl.ANY)],
        out_specs=pl.BlockSpec(memory_space=pl.ANY),
        scratch_shapes=[pltpu.SemaphoreType.DMA((N_DEV - 1,)), pltpu.SemaphoreType.DMA((N_DEV - 1,)),
                        pltpu.SemaphoreType.DMA],
    )(parts)


def _adamw(recv, w, m, v):
    rows = w.shape[0]
    tr = 512
    assert rows % tr == 0
    c1 = 1.0 / (1.0 - ADAM_B1 ** ADAM_STEP)
    c2 = 1.0 / (1.0 - ADAM_B2 ** ADAM_STEP)

    def body(r_ref, w_ref, m_ref, v_ref, g_ref, d_ref, mo_ref, vo_ref):
        g = r_ref[0]
        for s in range(1, N_DEV):
            g = g + r_ref[s]
        mn = ADAM_B1 * m_ref[...] + (1.0 - ADAM_B1) * g
        vn = ADAM_B2 * v_ref[...] + (1.0 - ADAM_B2) * (g * g)
        upd = (mn * c1) / (jnp.sqrt(vn * c2) + ADAM_EPS) + ADAM_WD * w_ref[...]
        g_ref[...] = g
        d_ref[...] = -ADAM_LR * upd
        mo_ref[...] = mn
        vo_ref[...] = vn

    spec = pl.BlockSpec((tr, LANES), lambda i: (i, 0))
    return pl.pallas_call(
        body, name="adamw", grid=(rows // tr,),
        out_shape=[jax.ShapeDtypeStruct((rows, LANES), F32)] * 4,
        in_specs=[pl.BlockSpec((N_DEV, tr, LANES), lambda i: (0, i, 0)), spec, spec, spec],
        out_specs=[spec] * 4,
        compiler_params=_cparams(1),
    )(recv, w, m, v)


def _tn_matmul(a, b, name):
    s, m = a.shape
    n = b.shape[1]
    tk = min(512, s)
    tm = min(1024, m)
    tn = max(c for c in range(LANES, 1280 + 1, LANES) if n % c == 0)
    assert m % tm == 0 and s % tk == 0
    nk = s // tk

    def body(a_ref, b_ref, o_ref):
        k = pl.program_id(2)

        @pl.when(k == 0)
        def _():
            o_ref[...] = jnp.zeros_like(o_ref)

        o_ref[...] += _dot_tn(a_ref[...], b_ref[...])

    return pl.pallas_call(
        body, name=name, grid=(m // tm, n // tn, nk),
        out_shape=jax.ShapeDtypeStruct((m, n), F32),
        in_specs=[pl.BlockSpec((tk, tm), lambda i, j, k: (k, i)), pl.BlockSpec((tk, tn), lambda i, j, k: (k, j))],
        out_specs=pl.BlockSpec((tm, tn), lambda i, j, k: (i, j)),
        compiler_params=_cparams(3),
    )(a, b)


def _token_tile(s):
    return min(256, s)


def _pre_fwd(x, n1, w_in):
    s = x.shape[0]
    tb = _token_tile(s)

    def body(x_ref, n_ref, w_ref, u_ref, q_ref, k_ref, v_ref, c_ref):
        h = _rms_fwd(x_ref[...], n_ref[...])
        p = _dot(h, w_ref[...])
        u_ref[...] = p[:, 0:256]
        q_ref[...] = (p[:, 256:768] * (Q_SCALE * LOG2E)).astype(BF16)
        k_ref[...] = p[:, 768:1280].astype(BF16)
        v_ref[...] = p[:, 1280:1792].astype(BF16)
        c_ref[...] = p[:, 1792:2560]

    def tok(w):
        return pl.BlockSpec((tb, w), lambda i: (i, 0))

    return pl.pallas_call(
        body, name="pre_fwd", grid=(s // tb,),
        out_shape=[jax.ShapeDtypeStruct((s, D_SSM), F32), jax.ShapeDtypeStruct((s, D_SB), BF16),
                   jax.ShapeDtypeStruct((s, D_SB), BF16), jax.ShapeDtypeStruct((s, D_SB), BF16),
                   jax.ShapeDtypeStruct((s, 3 * D_CONV), F32)],
        in_specs=[tok(D_MODEL), _full((1, D_MODEL)), _full((D_MODEL, D_IN))],
        out_specs=[tok(D_SSM), tok(D_SB), tok(D_SB), tok(D_SB), tok(3 * D_CONV)],
        compiler_params=_cparams(1),
    )(x, n1, w_in)


def _ssm_discretize(lr, li, ldt, br, bi):
    dt = jnp.exp(ldt)
    mag = jnp.exp(lr * dt)
    ab_re = mag * jnp.cos(li * dt)
    ab_im = mag * jnp.sin(li * dt)
    den = lr * lr + li * li
    xr = ab_re - 1.0
    coef_re = (xr * lr + ab_im * li) / den
    coef_im = (ab_im * lr - xr * li) / den
    bb_re = coef_re * br - coef_im * bi
    bb_im = coef_re * bi + coef_im * br
    return ab_re, ab_im, bb_re, bb_im


def _ssm_params_fwd(lr, li, ldt, br, bi):
    def body(lr_ref, li_ref, ldt_ref, br_ref, bi_ref, ar_ref, ai_ref, bbr_ref, bbi_ref):
        ar, ai, bbr, bbi = _ssm_discretize(lr_ref[...], li_ref[...], ldt_ref[...], br_ref[...], bi_ref[...])
        ar_ref[...] = ar
        ai_ref[...] = ai
        bbr_ref[...] = bbr
        bbi_ref[...] = bbi

    row = jax.ShapeDtypeStruct((1, D_STATE), F32)
    mat = jax.ShapeDtypeStruct((GROUP_CH, D_STATE), F32)
    return pl.pallas_call(body, name="ssm_params_fwd", out_shape=[row, row, mat, mat])(lr, li, ldt, br, bi)


def _ssm_params_bwd(lr, li, ldt, br, bi, d_ar, d_ai, d_bbr, d_bbi):
    def body(lr_ref, li_ref, ldt_ref, br_ref, bi_ref, dar_ref, dai_ref, dbbr_ref, dbbi_ref,
             glr_ref, gli_ref, gdt_ref, gbr_ref, gbi_ref):
        _, vjp = jax.vjp(_ssm_discretize, lr_ref[...], li_ref[...], ldt_ref[...], br_ref[...], bi_ref[...])
        glr, gli, gdt, gbr, gbi = vjp((dar_ref[...], dai_ref[...], dbbr_ref[...], dbbi_ref[...]))
        glr_ref[...] = glr
        gli_ref[...] = gli
        gbr_ref[...] = gbr
        gbi_ref[...] = gbi
        grp = lax.broadcasted_iota(jnp.int32, (N_GROUPS, D_STATE), 0)
        col = lax.broadcasted_iota(jnp.int32, (N_GROUPS, D_STATE), 1)
        own = (col // N_STATE) == grp
        per_group = jnp.sum(jnp.where(own, jnp.broadcast_to(gdt, (N_GROUPS, D_STATE)), 0.0), axis=1, keepdims=True)
        gdt_ref[...] = jnp.broadcast_to(per_group, (N_GROUPS, LANES))

    row = jax.ShapeDtypeStruct((1, D_STATE), F32)
    mat = jax.ShapeDtypeStruct((GROUP_CH, D_STATE), F32)
    return pl.pallas_call(
        body, name="ssm_params_bwd",
        out_shape=[row, row, jax.ShapeDtypeStruct((N_GROUPS, LANES), F32), mat, mat],
    )(lr, li, ldt, br, bi, d_ar, d_ai, d_bbr, d_bbi)


def _ssm_fwd(u, ab_re, ab_im, bb_re, bb_im, cc_re, cc_im, d_skip):
    s = u.shape[0]
    ts = _token_tile(s)
    nt = s // ts

    def body(u_ref, ar_ref, ai_ref, bbr_ref, bbi_ref, ccr_ref, cci_ref, d_ref, y_ref, ypre_ref, st_ref,
             bur, bui, hr_s, hi_s, cr, ci):
        @pl.when(pl.program_id(0) == 0)
        def _():
            cr[...] = jnp.zeros_like(cr)
            ci[...] = jnp.zeros_like(ci)

        u = u_ref[...]
        bur[...] = _dot(u, bbr_ref[...])
        bui[...] = _dot(u, bbi_ref[...])
        st_ref[0, 0:1, :] = cr[...]
        st_ref[0, 1:2, :] = ci[...]
        ar = ar_ref[...]
        ai = ai_ref[...]

        def step(t, carry):
            hr, hi = carry
            nr = ar * hr - ai * hi + bur[pl.ds(t, 1), :]
            ni = ar * hi + ai * hr + bui[pl.ds(t, 1), :]
            hr_s[pl.ds(t, 1), :] = nr
            hi_s[pl.ds(t, 1), :] = ni
            return nr, ni

        hr, hi = lax.fori_loop(0, ts, step, (cr[...], ci[...]), unroll=8)
        cr[...] = hr
        ci[...] = hi
        y = _dot_nt(hr_s[...], ccr_ref[...]) - _dot_nt(hi_s[...], cci_ref[...]) + d_ref[...] * u
        ypre_ref[...] = y
        y_ref[...] = _gelu(y).astype(BF16)

    tok = pl.BlockSpec((ts, D_SSM), lambda i: (i, 0))
    row = _full((1, D_STATE))
    mat = _full((D_SSM, D_STATE))
    return pl.pallas_call(
        body, name="ssm_fwd", grid=(nt,),
        out_shape=[jax.ShapeDtypeStruct((s, D_SSM), BF16), jax.ShapeDtypeStruct((s, D_SSM), F32),
                   jax.ShapeDtypeStruct((nt, 2, D_STATE), F32)],
        in_specs=[tok, row, row, mat, mat, mat, mat, _full((1, D_SSM))],
        out_specs=[tok, tok, pl.BlockSpec((1, 2, D_STATE), lambda i: (i, 0, 0))],
        scratch_shapes=[pltpu.VMEM((ts, D_STATE), F32)] * 4 + [pltpu.VMEM((1, D_STATE), F32)] * 2,
        compiler_params=_cparams(1),
    )(u, ab_re, ab_im, bb_re, bb_im, cc_re, cc_im, d_skip)


def _attn_tile(s):
    return min(256, s)


def _sb_block(z, valid):
    lp = jnp.log2(1.0 + jnp.exp2(-jnp.abs(z)))
    ls = jnp.minimum(z, 0.0) - lp
    l1m = ls - z
    if valid is not None:
        l1m = jnp.where(valid, l1m, 0.0)
    return ls, l1m


def _attn_fwd(q, k, v):
    s = q.shape[0]
    t = _attn_tile(s)
    nq = s // t
    assert nq <= LANES

    def body(q_ref, k_ref, v_ref, o_ref, tab_ref, z_s, ls_s, lg_s, q_s, tri_s, acc_s):
        i = pl.program_id(1)
        lane = lax.broadcasted_iota(jnp.int32, (t, LANES), 1)
        first = lane < HEAD_DIM
        row = lax.broadcasted_iota(jnp.int32, (t, t), 0)
        col = lax.broadcasted_iota(jnp.int32, (t, t), 1)
        tri_s[...] = (row > col).astype(BF16)
        causal = col < row
        q2 = q_ref[...]
        zero = jnp.zeros_like(q2)
        q_s[0] = jnp.where(first, q2, zero)
        q_s[1] = jnp.where(first, zero, q2)
        acc_s[...] = jnp.zeros_like(acc_s)
        tab_ref[...] = jnp.zeros_like(tab_ref)

        def scores(j):
            k2 = k_ref[pl.ds(pl.multiple_of(jnp.maximum(j, 0) * t, t), t), :]
            for a in range(2):
                z_s[a] = _dot_nt(q_s[a], k2)

        def logits(j, carries, valid):
            out = []
            for a in range(2):
                c1 = carries[a]
                ls, l1m = _sb_block(z_s[a], valid)
                ls_s[a] = ls + c1
                suffix = _dot(l1m, tri_s[...])
                lg_s[a] = ls_s[a] + suffix
                cols = slice(a * LANES, (a + 1) * LANES)
                tab_ref[:, cols] = jnp.where(lane == j, c1, tab_ref[:, cols])
                out.append(c1 + suffix[:, 0:1] + l1m[:, 0:1])
            return tuple(out)

        def accumulate(j, valid):
            v2 = v_ref[pl.ds(pl.multiple_of(jnp.maximum(j, 0) * t, t), t), :]
            for a in range(2):
                w = jnp.exp2(lg_s[a])
                if valid is not None:
                    w = jnp.where(valid, w, 0.0)
                acc_s[a] += _dot(w, v2)

        carries = tuple(jnp.zeros((t, 1), F32) for _ in range(2))
        scores(i)
        carries = logits(i, carries, causal)
        scores(i - 1)
        accumulate(i, causal)
        carries = logits(i - 1, carries, None)
        scores(i - 2)

        def trip(n, carries):
            accumulate(i - n + 2, None)
            carries = logits(i - n + 1, carries, None)
            scores(i - n)
            return carries

        lax.fori_loop(3, i + 3, trip, carries)
        o_ref[...] = jnp.where(first, acc_s[0], acc_s[1]).astype(BF16)

    return pl.pallas_call(
        body, name="attn_fwd", grid=(D_SB // LANES, nq),
        out_shape=[jax.ShapeDtypeStruct((s, D_SB), BF16), jax.ShapeDtypeStruct((s, 2 * D_SB), F32)],
        in_specs=[pl.BlockSpec((t, LANES), lambda p, i: (i, p)), pl.BlockSpec((s, LANES), lambda p, i: (0, p)),
                  pl.BlockSpec((s, LANES), lambda p, i: (0, p))],
        out_specs=[pl.BlockSpec((t, LANES), lambda p, i: (i, p)), pl.BlockSpec((t, 2 * LANES), lambda p, i: (i, p))],
        scratch_shapes=[pltpu.VMEM((2, t, t), F32)] * 3 + [pltpu.VMEM((2, t, LANES), BF16), pltpu.VMEM((t, t), BF16),
                                                           pltpu.VMEM((2, t, LANES), F32)],
        compiler_params=_cparams(2),
    )(q, k, v)


def _conv_taps(z, halo, rowi):
    z1 = jnp.where(rowi == 0, halo[7:8, :], pltpu.roll(z, 1, 0))
    z2 = jnp.where(rowi == 0, halo[6:7, :], jnp.where(rowi == 1, halo[7:8, :], pltpu.roll(z, 2, 0)))
    return z1, z2


def _merge_branches(x_ref, y_ref, o_ref, c_ref, ch_ref, n1_ref, wg_ref, bg_ref, wv_ref, wsg_ref, wao_ref,
                    cw_ref, wc_ref, tb):
    i = pl.program_id(0)
    h = _rms_fwd(x_ref[...], n1_ref[...])
    g = _sigmoid(_dot(h, wg_ref[...]) + bg_ref[...])
    y256 = y_ref[...]
    val = _dot(y256, wv_ref[...])
    sg = _sigmoid(_dot(y256, wsg_ref[...]))
    ysb = _dot(o_ref[...], wao_ref[...])
    c3 = c_ref[...]
    cb, cc, cx = c3[:, 0:256], c3[:, 256:512], c3[:, 512:768]
    ch = ch_ref[...]
    z = cc * cx
    zh = jnp.where(i > 0, ch[:, 256:512] * ch[:, 512:768], 0.0)
    rowi = lax.broadcasted_iota(jnp.int32, (tb, D_CONV), 0)
    z1, z2 = _conv_taps(z, zh, rowi)
    cw = cw_ref[...]
    cv = cw[0:1, :] * z2 + cw[1:2, :] * z1 + cw[2:3, :] * z
    yc = _dot(cb * cv, wc_ref[...])
    return dict(h=h, g=g, val=val, sg=sg, ysb=ysb, cb=cb, cv=cv, yc=yc)


def _merge_fwd(x, y256, o, c3, w):
    s = x.shape[0]
    tb = _token_tile(s)

    def body(x_ref, y_ref, o_ref, c_ref, ch_ref, n1_ref, wg_ref, bg_ref, wv_ref, wsg_ref, wao_ref, cw_ref, wc_ref,
             wm_ref, n2_ref, x1_ref, m2_ref, mg_ref):
        br = _merge_branches(x_ref, y_ref, o_ref, c_ref, ch_ref, n1_ref, wg_ref, bg_ref, wv_ref, wsg_ref, wao_ref,
                             cw_ref, wc_ref, tb)
        g = br["g"]
        merged = (g[:, 0:1024] * (br["val"] * br["sg"]) + g[:, 1024:2048] * br["ysb"] + g[:, 2048:3072] * br["yc"])
        mg_ref[...] = merged.astype(BF16)
        m2 = _dot(merged, wm_ref[...])
        m2_ref[...] = m2
        x1_ref[...] = x_ref[...] + _rms_fwd(m2, n2_ref[...])

    def tok(wd):
        return pl.BlockSpec((tb, wd), lambda i: (i, 0))

    halo = pl.BlockSpec((8, 3 * D_CONV), lambda i: (jnp.maximum(i * (tb // 8) - 1, 0), 0))
    return pl.pallas_call(
        body, name="merge_fwd", grid=(s // tb,),
        out_shape=[jax.ShapeDtypeStruct((s, D_MODEL), F32), jax.ShapeDtypeStruct((s, D_MODEL), F32),
                   jax.ShapeDtypeStruct((s, D_MODEL), BF16)],
        in_specs=[tok(D_MODEL), tok(D_SSM), tok(D_SB), tok(3 * D_CONV), halo,
                  _full((1, D_MODEL)), _full((D_MODEL, 3 * D_MODEL)), _full((1, 3 * D_MODEL)),
                  _full((D_SSM, D_MODEL)), _full((D_SSM, D_MODEL)), _full((D_SB, D_MODEL)),
                  _full((8, D_CONV)), _full((D_CONV, D_MODEL)), _full((D_MODEL, D_MODEL)), _full((1, D_MODEL))],
        out_specs=[tok(D_MODEL), tok(D_MODEL), tok(D_MODEL)],
        compiler_params=_cparams(1),
    )(x, y256, o, c3, c3, w["n1"], w["w_gate"], w["b_gate"], w["w_glu_val"], w["w_glu_gate"], w["w_attn_out"],
      w["conv_w"], w["w_conv_out"], w["w_mix_out"], w["n2"])


FF_CHUNK = 1024


def _ffn_fwd(x1, n3, w_up, w_dn, n4):
    s = x1.shape[0]
    tb = _token_tile(s)
    nh = D_FF // FF_CHUNK

    def body(x_ref, n3_ref, wu_ref, wd_ref, n4_ref, x2_ref, f_ref, h_s, acc):
        j = pl.program_id(1)

        @pl.when(j == 0)
        def _():
            h_s[...] = _rms_fwd(x_ref[...], n3_ref[...]).astype(BF16)
            acc[...] = jnp.zeros_like(acc)

        a = jnp.maximum(_dot(h_s[...], wu_ref[...]), 0.0)
        acc[...] += _dot(a * a, wd_ref[...])

        @pl.when(j == nh - 1)
        def _():
            f = acc[...]
            f_ref[...] = f
            x2_ref[...] = x_ref[...] + _rms_fwd(f, n4_ref[...])

    tok = pl.BlockSpec((tb, D_MODEL), lambda i, j: (i, 0))
    return pl.pallas_call(
        body, name="ffn_fwd", grid=(s // tb, nh),
        out_shape=[jax.ShapeDtypeStruct((s, D_MODEL), F32)] * 2,
        in_specs=[tok, _full((1, D_MODEL)), pl.BlockSpec((D_MODEL, FF_CHUNK), lambda i, j: (0, j)),
                  pl.BlockSpec((FF_CHUNK, D_MODEL), lambda i, j: (j, 0)), _full((1, D_MODEL))],
        out_specs=[tok, tok],
        scratch_shapes=[pltpu.VMEM((tb, D_MODEL), BF16), pltpu.VMEM((tb, D_MODEL), F32)],
        compiler_params=_cparams(2),
    )(x1, n3, w_up, w_dn, n4)


def _loss_head(y, target):
    s = y.shape[0]
    tb = _token_tile(s)

    def body(y_ref, t_ref, dy_ref, l_ref):
        @pl.when(pl.program_id(0) == 0)
        def _():
            l_ref[...] = jnp.zeros_like(l_ref)

        err = y_ref[...] - t_ref[...]
        dy_ref[...] = err * (1.0 / D_MODEL)
        l_ref[...] += 0.5 * jnp.sum(jnp.mean(err * err, axis=-1, keepdims=True), axis=0, keepdims=True)

    tok = pl.BlockSpec((tb, D_MODEL), lambda i: (i, 0))
    return pl.pallas_call(
        body, name="loss_head", grid=(s // tb,),
        out_shape=[jax.ShapeDtypeStruct((s, D_MODEL), F32), jax.ShapeDtypeStruct((8, LANES), F32)],
        in_specs=[tok, tok], out_specs=[tok, _full((8, LANES))],
        compiler_params=_cparams(1),
    )(y, target)


def _ffn_bwd(dx2, x1, f, n3, w_up, w_dn, n4):
    s = x1.shape[0]
    tb = _token_tile(s)
    nh = D_FF // FF_CHUNK
    nt = s // tb

    def body(dx2_ref, x_ref, f_ref, n3_ref, wu_ref, wd_ref, n4_ref,
             dx1_ref, r_ref, da_ref, h_ref, df_ref, dn3_ref, dn4_ref, acc):
        i = pl.program_id(0)
        j = pl.program_id(1)

        @pl.when((i == 0) & (j == 0))
        def _():
            dn3_ref[...] = jnp.zeros_like(dn3_ref)
            dn4_ref[...] = jnp.zeros_like(dn4_ref)

        @pl.when(j == 0)
        def _():
            h_ref[...] = _rms_fwd(x_ref[...], n3_ref[...]).astype(BF16)
            df, dn4 = _rms_bwd(dx2_ref[...], f_ref[...], n4_ref[...])
            df_ref[...] = df.astype(BF16)
            dn4_ref[...] += dn4
            acc[...] = jnp.zeros_like(acc)

        a = jnp.maximum(_dot(h_ref[...], wu_ref[...]), 0.0)
        r_ref[...] = (a * a).astype(BF16)
        da = (_dot_nt(df_ref[...], wd_ref[...]) * (2.0 * a)).astype(BF16)
        da_ref[...] = da
        acc[...] += _dot_nt(da, wu_ref[...])

        @pl.when(j == nh - 1)
        def _():
            dx, dn3 = _rms_bwd(acc[...], x_ref[...], n3_ref[...])
            dx1_ref[...] = dx2_ref[...] + dx
            dn3_ref[...] += dn3

    tok = pl.BlockSpec((tb, D_MODEL), lambda i, j: (i, 0))
    hid = pl.BlockSpec((tb, FF_CHUNK), lambda i, j: (i, j))
    tok_b = jax.ShapeDtypeStruct((s, D_MODEL), BF16)
    hid_b = jax.ShapeDtypeStruct((s, D_FF), BF16)
    row = jax.ShapeDtypeStruct((1, D_MODEL), F32)
    return pl.pallas_call(
        body, name="ffn_bwd", grid=(nt, nh),
        out_shape=[jax.ShapeDtypeStruct((s, D_MODEL), F32), hid_b, hid_b, tok_b, tok_b, row, row],
        in_specs=[tok, tok, tok, _full((1, D_MODEL)), pl.BlockSpec((D_MODEL, FF_CHUNK), lambda i, j: (0, j)),
                  pl.BlockSpec((FF_CHUNK, D_MODEL), lambda i, j: (j, 0)), _full((1, D_MODEL))],
        out_specs=[tok, hid, hid, tok, tok, _full((1, D_MODEL)), _full((1, D_MODEL))],
        scratch_shapes=[pltpu.VMEM((tb, D_MODEL), F32)],
        compiler_params=_cparams(2),
    )(dx2, x1, f, n3, w_up, w_dn, n4)


def _merge_bwd(dx1, x, m2, y256, o, c3, w):
    s = x.shape[0]
    tb = _token_tile(s)

    def body(dx1_ref, m2_ref, x_ref, y_ref, o_ref, c_ref, ch_ref, n1_ref, wg_ref, bg_ref, wv_ref, wsg_ref, wao_ref,
             cw_ref, wc_ref, wm_ref, n2_ref,
             dhg_ref, h_ref, dgz_ref, dvs_ref, dy_ref, do_ref, dyb_ref, dyc_ref, bcv_ref, dcv_ref, dcb_ref, dm2_ref,
             dbg_ref, dn2_ref):
        @pl.when(pl.program_id(0) == 0)
        def _():
            dbg_ref[...] = jnp.zeros_like(dbg_ref)
            dn2_ref[...] = jnp.zeros_like(dn2_ref)

        dm2, dn2 = _rms_bwd(dx1_ref[...], m2_ref[...], n2_ref[...])
        dn2_ref[...] += dn2
        dm2_ref[...] = dm2.astype(BF16)
        dmg = _dot_nt(dm2, wm_ref[...])
        br = _merge_branches(x_ref, y_ref, o_ref, c_ref, ch_ref, n1_ref, wg_ref, bg_ref, wv_ref, wsg_ref, wao_ref,
                             cw_ref, wc_ref, tb)
        g = br["g"]
        g1, g2, g3 = g[:, 0:1024], g[:, 1024:2048], g[:, 2048:3072]
        val, sg = br["val"], br["sg"]
        h_ref[...] = br["h"].astype(BF16)
        dgz = jnp.concatenate([dmg * (val * sg) * g1 * (1.0 - g1), dmg * br["ysb"] * g2 * (1.0 - g2),
                               dmg * br["yc"] * g3 * (1.0 - g3)], axis=1)
        dbg_ref[...] += jnp.sum(dgz, axis=0, keepdims=True)
        dgz_ref[...] = dgz.astype(BF16)
        dhg_ref[...] = _dot_nt(dgz, wg_ref[...])
        dys = dmg * g1
        dval = dys * sg
        dsg = dys * val * sg * (1.0 - sg)
        dvs_ref[:, 0:1024] = dval.astype(BF16)
        dvs_ref[:, 1024:2048] = dsg.astype(BF16)
        dy_ref[...] = _dot_nt(dval, wv_ref[...]) + _dot_nt(dsg, wsg_ref[...])
        dyb = (dmg * g2).astype(BF16)
        dyb_ref[...] = dyb
        do_ref[...] = _dot_nt(dyb, wao_ref[...]).astype(BF16)
        dyc = (dmg * g3).astype(BF16)
        dyc_ref[...] = dyc
        dcq = _dot_nt(dyc, wc_ref[...])
        bcv_ref[...] = (br["cb"] * br["cv"]).astype(BF16)
        dcb_ref[...] = dcq * br["cv"]
        dcv_ref[...] = dcq * br["cb"]

    def tok(wd):
        return pl.BlockSpec((tb, wd), lambda i: (i, 0))

    def out(wd, dt):
        return jax.ShapeDtypeStruct((s, wd), dt)

    halo = pl.BlockSpec((8, 3 * D_CONV), lambda i: (jnp.maximum(i * (tb // 8) - 1, 0), 0))
    return pl.pallas_call(
        body, name="merge_bwd", grid=(s // tb,),
        out_shape=[out(D_MODEL, F32), out(D_MODEL, BF16), out(3 * D_MODEL, BF16), out(2 * D_MODEL, BF16),
                   out(D_SSM, F32), out(D_SB, BF16), out(D_MODEL, BF16), out(D_MODEL, BF16), out(D_CONV, BF16),
                   out(D_CONV, F32), out(D_CONV, F32), out(D_MODEL, BF16),
                   jax.ShapeDtypeStruct((1, 3 * D_MODEL), F32), jax.ShapeDtypeStruct((1, D_MODEL), F32)],
        in_specs=[tok(D_MODEL), tok(D_MODEL), tok(D_MODEL), tok(D_SSM), tok(D_SB), tok(3 * D_CONV), halo,
                  _full((1, D_MODEL)), _full((D_MODEL, 3 * D_MODEL)), _full((1, 3 * D_MODEL)),
                  _full((D_SSM, D_MODEL)), _full((D_SSM, D_MODEL)), _full((D_SB, D_MODEL)),
                  _full((8, D_CONV)), _full((D_CONV, D_MODEL)), _full((D_MODEL, D_MODEL)), _full((1, D_MODEL))],
        out_specs=[tok(D_MODEL), tok(D_MODEL), tok(3 * D_MODEL), tok(2 * D_MODEL), tok(D_SSM), tok(D_SB),
                   tok(D_MODEL), tok(D_MODEL), tok(D_CONV), tok(D_CONV), tok(D_CONV), tok(D_MODEL),
                   _full((1, 3 * D_MODEL)), _full((1, D_MODEL))],
        compiler_params=_cparams(1),
    )(dx1, m2, x, y256, o, c3, c3, w["n1"], w["w_gate"], w["b_gate"], w["w_glu_val"], w["w_glu_gate"],
      w["w_attn_out"], w["conv_w"], w["w_conv_out"], w["w_mix_out"], w["n2"])


def _ssm_bwd(u, ypre, dy, states, ab_re, ab_im, bb_re, bb_im, cc_re, cc_im, d_skip):
    s = u.shape[0]
    ts = _token_tile(s)
    nt = s // ts

    def body(u_ref, yp_ref, dy_ref, st_ref, ar_ref, ai_ref, bbr_ref, bbi_ref, ccr_ref, cci_ref, d_ref,
             du_ref, dar_ref, dai_ref, dbbr_ref, dbbi_ref, dccr_ref, dcci_ref, dd_ref,
             bur, bui, hr_s, hi_s, pr_s, pi_s, lr_s, li_s, cr, ci):
        @pl.when(pl.program_id(0) == 0)
        def _():
            cr[...] = jnp.zeros_like(cr)
            ci[...] = jnp.zeros_like(ci)
            for ref in (dar_ref, dai_ref, dbbr_ref, dbbi_ref, dccr_ref, dcci_ref, dd_ref):
                ref[...] = jnp.zeros_like(ref)

        u = u_ref[...]
        ub = u.astype(BF16)
        bur[...] = _dot(ub, bbr_ref[...])
        bui[...] = _dot(ub, bbi_ref[...])
        ar = ar_ref[...]
        ai = ai_ref[...]

        def fwd_step(t, carry):
            hr, hi = carry
            pr_s[pl.ds(t, 1), :] = hr
            pi_s[pl.ds(t, 1), :] = hi
            nr = ar * hr - ai * hi + bur[pl.ds(t, 1), :]
            ni = ar * hi + ai * hr + bui[pl.ds(t, 1), :]
            hr_s[pl.ds(t, 1), :] = nr
            hi_s[pl.ds(t, 1), :] = ni
            return nr, ni

        lax.fori_loop(0, ts, fwd_step, (st_ref[0, 0:1, :], st_ref[0, 1:2, :]), unroll=8)

        dyp = dy_ref[...] * _gelu_grad(yp_ref[...])
        dypb = dyp.astype(BF16)
        lr_s[...] = _dot(dypb, ccr_ref[...])
        li_s[...] = -_dot(dypb, cci_ref[...])

        def bwd_step(tt, carry):
            t = ts - 1 - tt
            nr, ni = carry
            qr = lr_s[pl.ds(t, 1), :] + ar * nr + ai * ni
            qi = li_s[pl.ds(t, 1), :] + ar * ni - ai * nr
            lr_s[pl.ds(t, 1), :] = qr
            li_s[pl.ds(t, 1), :] = qi
            return qr, qi

        nr, ni = lax.fori_loop(0, ts, bwd_step, (cr[...], ci[...]), unroll=8)
        cr[...] = nr
        ci[...] = ni
        lam_r = lr_s[...]
        lam_i = li_s[...]
        pr = pr_s[...]
        pi = pi_s[...]
        dar_ref[...] += jnp.sum(lam_r * pr + lam_i * pi, axis=0, keepdims=True)
        dai_ref[...] += jnp.sum(lam_i * pr - lam_r * pi, axis=0, keepdims=True)
        lrb = lam_r.astype(BF16)
        lib = lam_i.astype(BF16)
        du_ref[...] = _dot_nt(lrb, bbr_ref[...]) + _dot_nt(lib, bbi_ref[...]) + d_ref[...] * dyp
        dbbr_ref[...] += _dot_tn(ub, lrb)
        dbbi_ref[...] += _dot_tn(ub, lib)
        dccr_ref[...] += _dot_tn(dypb, hr_s[...])
        dcci_ref[...] -= _dot_tn(dypb, hi_s[...])
        dd_ref[...] += jnp.sum(dyp * u, axis=0, keepdims=True)

    tok = pl.BlockSpec((ts, D_SSM), lambda i: (nt - 1 - i, 0))
    row = _full((1, D_STATE))
    mat = _full((D_SSM, D_STATE))
    row_o = jax.ShapeDtypeStruct((1, D_STATE), F32)
    mat_o = jax.ShapeDtypeStruct((D_SSM, D_STATE), F32)
    return pl.pallas_call(
        body, name="ssm_bwd", grid=(nt,),
        out_shape=[jax.ShapeDtypeStruct((s, D_SSM), F32), row_o, row_o, mat_o, mat_o, mat_o, mat_o,
                   jax.ShapeDtypeStruct((1, D_SSM), F32)],
        in_specs=[tok, tok, tok, pl.BlockSpec((1, 2, D_STATE), lambda i: (nt - 1 - i, 0, 0)),
                  row, row, mat, mat, mat, mat, _full((1, D_SSM))],
        out_specs=[tok, row, row, mat, mat, mat, mat, _full((1, D_SSM))],
        scratch_shapes=[pltpu.VMEM((ts, D_STATE), F32)] * 8 + [pltpu.VMEM((1, D_STATE), F32)] * 2,
        compiler_params=_cparams(1),
    )(u, ypre, dy, states, ab_re, ab_im, bb_re, bb_im, cc_re, cc_im, d_skip)


def _attn_bwd(q, k, v, do, tab):
    s = q.shape[0]
    t = _attn_tile(s)
    nq = s // t
    n_pairs = D_SB // LANES

    def body(q_ref, do_ref, tab_ref, k_ref, v_ref, dq_ref, dk_hbm, dv_hbm, dk_s, dv_s, sems,
             z_s, ls_s, suf_s, dw_s, dl_s, beta_s, pre_s, wb_s, dq_s, q_s, do_s, qt_s, dot_s, tri_s):
        p = pl.program_id(0)
        i = pl.program_id(1)

        @pl.when(i == 0)
        def _():
            dk_s[...] = jnp.zeros_like(dk_s)
            dv_s[...] = jnp.zeros_like(dv_s)

        lane = lax.broadcasted_iota(jnp.int32, (t, LANES), 1)
        first = lane < HEAD_DIM
        row = lax.broadcasted_iota(jnp.int32, (t, t), 0)
        col = lax.broadcasted_iota(jnp.int32, (t, t), 1)
        tri_s[0] = (row > col).astype(BF16)
        tri_s[1] = (row < col).astype(BF16)
        causal = col < row
        q2 = q_ref[...]
        do2 = do_ref[...]
        zero = jnp.zeros_like(q2)
        for a in range(2):
            mine = first if a == 0 else jnp.logical_not(first)
            q_s[a] = jnp.where(mine, q2, zero)
            do_s[a] = jnp.where(mine, do2, zero)
            qt_s[a] = jnp.where(mine, q2, zero).astype(F32).T.astype(BF16)
            dot_s[a] = jnp.where(mine, do2, zero).astype(F32).T.astype(BF16)
        z_s[...] = jnp.full_like(z_s, -1e30)
        ls_s[...] = jnp.full_like(ls_s, -1e30)
        for ref in (suf_s, dw_s, dl_s, beta_s, pre_s, wb_s, dq_s):
            ref[...] = jnp.zeros_like(ref)

        def tile_rows(j):
            return pl.ds(pl.multiple_of(jnp.maximum(j, 0) * t, t), t)

        def scores(j):
            k2 = k_ref[tile_rows(j), :]
            for a in range(2):
                z_s[a] = _dot_nt(q_s[a], k2)

        def spread(j, valid):
            v2 = v_ref[tile_rows(j), :]
            for a in range(2):
                ls, l1m = _sb_block(z_s[a], valid)
                ls_s[a] = ls
                suf_s[a] = _dot(l1m, tri_s[0])
                dw_s[a] = _dot_nt(do_s[a], v2)

        def weights(j, valid):
            for a in range(2):
                c1 = jnp.sum(jnp.where(lane == j, tab_ref[:, a * LANES:(a + 1) * LANES], 0.0), axis=1, keepdims=True)
                ls = ls_s[a]
                w = jnp.exp2(ls + suf_s[a] + c1)
                if valid is not None:
                    w = jnp.where(valid, w, 0.0)
                dl = w * dw_s[a]
                dl_s[a] = dl
                beta_s[a] = jnp.exp2(ls)
                wb_s[a] = w.astype(BF16)
                pre_s[a] = _dot(dl, tri_s[1])

        def grads(j, c2s, valid):
            rows = tile_rows(j)
            k2 = k_ref[rows, :]
            out = []
            dkt = dvt = None
            for a in range(2):
                dl = dl_s[a]
                pre = pre_s[a]
                dz = dl - beta_s[a] * (dl + pre + c2s[a])
                if valid is not None:
                    dz = jnp.where(valid, dz, 0.0)
                dzb = dz.astype(BF16)
                dq_s[a] += _dot(dzb, k2)
                dk_a = jnp.dot(qt_s[a], dzb, preferred_element_type=F32)
                dv_a = jnp.dot(dot_s[a], wb_s[a], preferred_element_type=F32)
                dkt = dk_a if dkt is None else dkt + dk_a
                dvt = dv_a if dvt is None else dvt + dv_a
                out.append(c2s[a] + pre[:, t - 1:t] + dl[:, t - 1:t])
            dk_s[:, rows] += dkt
            dv_s[:, rows] += dvt
            return tuple(out)

        def trip(n, c2s):
            c2s = grads(n - 3, c2s, None)
            weights(n - 2, None)
            spread(n - 1, None)
            scores(n)
            return c2s

        c2s = lax.fori_loop(0, i + 1, trip, tuple(jnp.zeros((t, 1), F32) for _ in range(2)))
        c2s = grads(i - 2, c2s, None)
        weights(i - 1, None)
        spread(i, causal)
        c2s = grads(i - 1, c2s, None)
        weights(i, causal)
        grads(i, c2s, causal)
        dq_ref[...] = Q_SCALE * jnp.where(first, dq_s[0], dq_s[1])

        @pl.when(i == nq - 1)
        def _():
            dk_s[...] = dk_s[...] * (1.0 / LOG2E)
            ck = pltpu.make_async_copy(dk_s, dk_hbm.at[p], sems.at[0])
            cv = pltpu.make_async_copy(dv_s, dv_hbm.at[p], sems.at[1])
            ck.start()
            cv.start()
            ck.wait()
            cv.wait()

    blk = pl.BlockSpec((t, LANES), lambda p, i: (i, p))
    seq = pl.BlockSpec((s, LANES), lambda p, i: (0, p))
    pairs = jax.ShapeDtypeStruct((n_pairs, LANES, s), F32)
    stage = pltpu.VMEM((2, t, t), F32)
    return pl.pallas_call(
        body, name="attn_bwd", grid=(n_pairs, nq),
        out_shape=[jax.ShapeDtypeStruct((s, D_SB), F32), pairs, pairs],
        in_specs=[blk, blk, pl.BlockSpec((t, 2 * LANES), lambda p, i: (i, p)), seq, seq],
        out_specs=[blk, pl.BlockSpec(memory_space=pl.ANY), pl.BlockSpec(memory_space=pl.ANY)],
        scratch_shapes=[pltpu.VMEM((LANES, s), F32), pltpu.VMEM((LANES, s), F32), pltpu.SemaphoreType.DMA((2,))]
        + [stage] * 7 + [pltpu.VMEM((2, t, t), BF16), pltpu.VMEM((2, t, LANES), F32),
                         pltpu.VMEM((2, t, LANES), BF16), pltpu.VMEM((2, t, LANES), BF16),
                         pltpu.VMEM((2, LANES, t), BF16), pltpu.VMEM((2, LANES, t), BF16),
                         pltpu.VMEM((2, t, t), BF16)],
        compiler_params=_cparams(2),
    )(q, do, tab, k, v)


def _pre_bwd(dres, dhg, x, du, dq, dk, dv, dcb, dcv, c3, n1, w_in, conv_w):
    s = x.shape[0]
    tb = _token_tile(s)
    nt = s // tb

    def body(dres_ref, dhg_ref, x_ref, du_ref, dq_ref, dk_ref, dv_ref, dcb_ref, dcv_ref, dnext_ref, c_ref, ch_ref,
             n1_ref, w_ref, cw_ref, dx_ref, dp_ref, dn1_ref, dcw_ref):
        i = pl.program_id(0)

        @pl.when(i == 0)
        def _():
            dn1_ref[...] = jnp.zeros_like(dn1_ref)
            dcw_ref[...] = jnp.zeros_like(dcw_ref)

        c3 = c_ref[...]
        cc, cx = c3[:, 256:512], c3[:, 512:768]
        ch = ch_ref[...]
        z = cc * cx
        zh = jnp.where(i > 0, ch[:, 256:512] * ch[:, 512:768], 0.0)
        rowi = lax.broadcasted_iota(jnp.int32, (tb, D_CONV), 0)
        z1, z2 = _conv_taps(z, zh, rowi)
        dcv = dcv_ref[...]
        nxt = jnp.where(i < nt - 1, dnext_ref[...], 0.0)
        d1 = jnp.where(rowi == tb - 1, nxt[0:1, :], pltpu.roll(dcv, tb - 1, 0))
        d2 = jnp.where(rowi == tb - 2, nxt[0:1, :], jnp.where(rowi == tb - 1, nxt[1:2, :], pltpu.roll(dcv, tb - 2, 0)))
        cw = cw_ref[...]
        dz = cw[2:3, :] * dcv + cw[1:2, :] * d1 + cw[0:1, :] * d2
        dcw_ref[0:1, :] += jnp.sum(dcv * z2, axis=0, keepdims=True)
        dcw_ref[1:2, :] += jnp.sum(dcv * z1, axis=0, keepdims=True)
        dcw_ref[2:3, :] += jnp.sum(dcv * z, axis=0, keepdims=True)
        dp_ref[:, 0:256] = du_ref[...].astype(BF16)
        dp_ref[:, 256:768] = dq_ref[...].astype(BF16)
        dp_ref[:, 768:1280] = dk_ref[...].astype(BF16)
        dp_ref[:, 1280:1792] = dv_ref[...].astype(BF16)
        dp_ref[:, 1792:2048] = dcb_ref[...].astype(BF16)
        dp_ref[:, 2048:2304] = (dz * cx).astype(BF16)
        dp_ref[:, 2304:2560] = (dz * cc).astype(BF16)
        dh = dhg_ref[...] + _dot_nt(dp_ref[...], w_ref[...])
        dx, dn1 = _rms_bwd(dh, x_ref[...], n1_ref[...])
        dx_ref[...] = dres_ref[...] + dx
        dn1_ref[...] += dn1

    def tok(wd):
        return pl.BlockSpec((tb, wd), lambda i: (i, 0))

    halo_prev = pl.BlockSpec((8, 3 * D_CONV), lambda i: (jnp.maximum(i * (tb // 8) - 1, 0), 0))
    halo_next = pl.BlockSpec((8, D_CONV), lambda i: (jnp.minimum((i + 1) * (tb // 8), s // 8 - 1), 0))
    return pl.pallas_call(
        body, name="pre_bwd", grid=(nt,),
        out_shape=[jax.ShapeDtypeStruct((s, D_MODEL), F32), jax.ShapeDtypeStruct((s, D_IN), BF16),
                   jax.ShapeDtypeStruct((1, D_MODEL), F32), jax.ShapeDtypeStruct((8, D_CONV), F32)],
        in_specs=[tok(D_MODEL), tok(D_MODEL), tok(D_MODEL), tok(D_SSM), tok(D_SB), tok(D_SB), tok(D_SB),
                  tok(D_CONV), tok(D_CONV), halo_next, tok(3 * D_CONV), halo_prev,
                  _full((1, D_MODEL)), _full((D_MODEL, D_IN)), _full((8, D_CONV))],
        out_specs=[tok(D_MODEL), tok(D_IN), _full((1, D_MODEL)), _full((8, D_CONV))],
        compiler_params=_cparams(1),
    )(dres, dhg, x, du, dq, dk, dv, dcb, dcv, dcv, c3, c3, n1, w_in, conv_w)


def _padded(n):
    return -(-n // PIECE_ALIGN) * PIECE_ALIGN


def _pack(pieces, total_rows=None):
    flat = []
    for a in pieces:
        a = a.reshape(-1)
        flat.append(jnp.pad(a, (0, _padded(a.size) - a.size)))
    out = jnp.concatenate(flat)
    if total_rows is not None:
        out = jnp.pad(out, (0, total_rows * LANES - out.size))
    return out.reshape(-1, LANES)


def _unpack(flat, shapes, lead=()):
    flat = flat.reshape(lead + (-1,))
    out, off = [], 0
    for shp in shapes:
        n = math.prod(shp)
        out.append(flat[..., off:off + n].reshape(lead + tuple(shp)))
        off += _padded(n)
    return out


def _to_shards(full, axis):
    shp = full.shape
    cut = shp[:axis] + (N_DEV, shp[axis] // N_DEV) + shp[axis + 1:]
    return jnp.moveaxis(full.reshape(cut), axis, 0)


def _from_shards(shards, axis):
    moved = jnp.moveaxis(shards, 0, axis)
    shp = moved.shape
    return moved.reshape(shp[:axis] + (shp[axis] * shp[axis + 1],) + shp[axis + 2:])


def _expand_groups(compact):
    rows = lax.broadcasted_iota(jnp.int32, (D_SSM, D_STATE), 0) // GROUP_CH
    cols = lax.broadcasted_iota(jnp.int32, (D_SSM, D_STATE), 1) // N_STATE
    return jnp.where(rows == cols, jnp.tile(compact, (N_GROUPS, 1)), 0.0)


def _collect_groups(dense):
    rows = lax.broadcasted_iota(jnp.int32, (D_SSM, D_STATE), 0) // GROUP_CH
    cols = lax.broadcasted_iota(jnp.int32, (D_SSM, D_STATE), 1) // N_STATE
    return jnp.where(rows == cols, dense, 0.0).reshape(N_GROUPS, GROUP_CH, D_STATE).sum(axis=0)


def kernel(x, norm_mix_pre, norm_mix_post, w_in, w_gate, b_gate, ssm_a_re, ssm_a_im, ssm_log_dt, ssm_b_re, ssm_b_im, ssm_c_re, ssm_c_im, ssm_d, w_glu_val, w_glu_gate, w_attn_out, conv_w, w_conv_out, w_mix_out, norm_ffn_pre, norm_ffn_post, w_ffn_up, w_ffn_down, loss_target, m_norm_mix_pre, m_norm_mix_post, m_w_in, m_w_gate, m_b_gate, m_ssm_a_re, m_ssm_a_im, m_ssm_log_dt, m_ssm_b_re, m_ssm_b_im, m_ssm_c_re, m_ssm_c_im, m_ssm_d, m_w_glu_val, m_w_glu_gate, m_w_attn_out, m_conv_w, m_w_conv_out, m_w_mix_out, m_norm_ffn_pre, m_norm_ffn_post, m_w_ffn_up, m_w_ffn_down, v_norm_mix_pre, v_norm_mix_post, v_w_in, v_w_gate, v_b_gate, v_ssm_a_re, v_ssm_a_im, v_ssm_log_dt, v_ssm_b_re, v_ssm_b_im, v_ssm_c_re, v_ssm_c_im, v_ssm_d, v_w_glu_val, v_w_glu_gate, v_w_attn_out, v_conv_w, v_w_conv_out, v_w_mix_out, v_norm_ffn_pre, v_norm_ffn_post, v_w_ffn_up, v_w_ffn_down):
    args = dict(locals())
    wts = {n: args[n] for n in WEIGHTS}
    mom = {n: args["m_" + n] for n in WEIGHTS}
    vel = {n: args["v_" + n] for n in WEIGHTS}
    seq = x.shape[1]
    x0 = x.reshape(seq, D_MODEL)
    target = loss_target.reshape(seq, D_MODEL)

    pieces = []
    for n in SHARDED:
        if n == "conv_w":
            hi = wts[n].astype(BF16)
            pieces += [hi, (wts[n] - hi.astype(F32)).astype(BF16)]
        else:
            pieces.append(wts[n].astype(BF16))
    shard_shapes = [p.shape for p in pieces]
    gathered = _unpack(_all_gather(_pack(pieces)), shard_shapes, lead=(N_DEV,))
    full = {}
    it = iter(gathered)
    for n in SHARDED:
        if n == "conv_w":
            full[n] = _from_shards(next(it).astype(F32) + next(it).astype(F32), SHARD_AXIS[n])
        else:
            full[n] = _from_shards(next(it), SHARD_AXIS[n])

    def layer_weights(l):
        return dict(
            n1=norm_mix_pre[l][None], n2=norm_mix_post[l][None], n3=norm_ffn_pre[l][None], n4=norm_ffn_post[l][None],
            w_in=full["w_in"][l], w_gate=full["w_gate"][l], b_gate=b_gate[l][None],
            w_glu_val=full["w_glu_val"][l], w_glu_gate=full["w_glu_gate"][l], w_attn_out=full["w_attn_out"][l],
            conv_w=jnp.pad(full["conv_w"][l], ((0, 5), (0, 0))), w_conv_out=full["w_conv_out"][l],
            w_mix_out=full["w_mix_out"][l], w_ffn_up=full["w_ffn_up"][l], w_ffn_down=full["w_ffn_down"][l],
            lr=ssm_a_re[l].reshape(1, D_STATE), li=ssm_a_im[l].reshape(1, D_STATE),
            ldt=jnp.repeat(ssm_log_dt[l], N_STATE).reshape(1, D_STATE),
            br=jnp.transpose(ssm_b_re[l], (2, 0, 1)).reshape(GROUP_CH, D_STATE),
            bi=jnp.transpose(ssm_b_im[l], (2, 0, 1)).reshape(GROUP_CH, D_STATE),
            cc_re=_expand_groups(jnp.transpose(ssm_c_re[l], (1, 0, 2)).reshape(GROUP_CH, D_STATE)).astype(BF16),
            cc_im=_expand_groups(jnp.transpose(ssm_c_im[l], (1, 0, 2)).reshape(GROUP_CH, D_STATE)).astype(BF16),
            d_skip=ssm_d[l][None],
        )

    saved = []
    xin = x0
    for l in range(DEPTH):
        w = layer_weights(l)
        ab_re, ab_im, bbr, bbi = _ssm_params_fwd(w["lr"], w["li"], w["ldt"], w["br"], w["bi"])
        w.update(ab_re=ab_re, ab_im=ab_im, bb_re=_expand_groups(bbr).astype(BF16), bb_im=_expand_groups(bbi).astype(BF16))
        u, q, k, v, c3 = _pre_fwd(xin, w["n1"], w["w_in"])
        y256, ypre, states = _ssm_fwd(u, ab_re, ab_im, w["bb_re"], w["bb_im"], w["cc_re"], w["cc_im"], w["d_skip"])
        o, tab = _attn_fwd(q, k, v)
        x1, m2, merged = _merge_fwd(xin, y256, o, c3, w)
        x2, f = _ffn_fwd(x1, w["n3"], w["w_ffn_up"], w["w_ffn_down"], w["n4"])
        saved.append(dict(w=w, x0=xin, u=u, q=q, k=k, v=v, c3=c3, y256=y256, ypre=ypre, states=states, o=o, tab=tab, x1=x1,
                          m2=m2, merged=merged, f=f))
        xin = x2

    dxo, loss_part = _loss_head(xin, target)
    loss = lax.psum(loss_part[0, 0], ("x", "y", "c"))

    grads = {n: [None] * DEPTH for n in WEIGHTS}
    for l in reversed(range(DEPTH)):
        sv = saved[l]
        w = sv["w"]
        dx1, r, da, h2, df, dn3, dn4 = _ffn_bwd(dxo, sv["x1"], sv["f"], w["n3"], w["w_ffn_up"], w["w_ffn_down"], w["n4"])
        grads["w_ffn_down"][l] = _tn_matmul(r, df, "grad_w_ffn_down")
        grads["w_ffn_up"][l] = _tn_matmul(h2, da, "grad_w_ffn_up")
        grads["norm_ffn_pre"][l] = dn3[0]
        grads["norm_ffn_post"][l] = dn4[0]
        (dhg, hb, dgz, dvs, dy256, do, dyb, dyc, bcv, dcv, dcb, dm2, dbg, dn2) = _merge_bwd(
            dx1, sv["x0"], sv["m2"], sv["y256"], sv["o"], sv["c3"], w)
        grads["w_mix_out"][l] = _tn_matmul(sv["merged"], dm2, "grad_w_mix_out")
        grads["w_gate"][l] = _tn_matmul(hb, dgz, "grad_w_gate")
        dglu = _tn_matmul(sv["y256"], dvs, "grad_w_glu")
        grads["w_glu_val"][l] = dglu[:, :D_MODEL]
        grads["w_glu_gate"][l] = dglu[:, D_MODEL:]
        grads["w_attn_out"][l] = _tn_matmul(sv["o"], dyb, "grad_w_attn_out")
        grads["w_conv_out"][l] = _tn_matmul(bcv, dyc, "grad_w_conv_out")
        grads["b_gate"][l] = dbg[0]
        grads["norm_mix_post"][l] = dn2[0]
        du, dar, dai, dbbr, dbbi, dccr, dcci, dd = _ssm_bwd(
            sv["u"], sv["ypre"], dy256, sv["states"], w["ab_re"], w["ab_im"], w["bb_re"], w["bb_im"],
            w["cc_re"], w["cc_im"], w["d_skip"])
        glr, gli, gdt, gbr, gbi = _ssm_params_bwd(w["lr"], w["li"], w["ldt"], w["br"], w["bi"], dar, dai,
                                                   _collect_groups(dbbr), _collect_groups(dbbi))
        grads["ssm_a_re"][l] = glr.reshape(N_GROUPS, N_STATE)
        grads["ssm_a_im"][l] = gli.reshape(N_GROUPS, N_STATE)
        grads["ssm_log_dt"][l] = gdt[:, 0]
        grads["ssm_b_re"][l] = jnp.transpose(gbr.reshape(GROUP_CH, N_GROUPS, N_STATE), (1, 2, 0))
        grads["ssm_b_im"][l] = jnp.transpose(gbi.reshape(GROUP_CH, N_GROUPS, N_STATE), (1, 2, 0))
        grads["ssm_c_re"][l] = jnp.transpose(_collect_groups(dccr).reshape(GROUP_CH, N_GROUPS, N_STATE), (1, 0, 2))
        grads["ssm_c_im"][l] = jnp.transpose(_collect_groups(dcci).reshape(GROUP_CH, N_GROUPS, N_STATE), (1, 0, 2))
        grads["ssm_d"][l] = dd[0]
        dq, dk4, dv4 = _attn_bwd(sv["q"], sv["k"], sv["v"], do, sv["tab"])
        dk = jnp.transpose(dk4, (2, 0, 1)).reshape(seq, D_SB)
        dv = jnp.transpose(dv4, (2, 0, 1)).reshape(seq, D_SB)
        dxo, dp, dn1, dcw = _pre_bwd(dx1, dhg, sv["x0"], du, dq, dk, dv, dcb, dcv, sv["c3"], w["n1"], w["w_in"], w["conv_w"])
        grads["w_in"][l] = _tn_matmul(hb, dp, "grad_w_in")
        grads["norm_mix_pre"][l] = dn1[0]
        grads["conv_w"][l] = dcw[0:3]
    grad_x = dxo.reshape(x.shape)
    gfull = {n: jnp.stack(grads[n]) for n in WEIGHTS}

    rep_parts = _pack([gfull[n] for n in REPLICATED])
    sh_parts = jnp.stack([_pack([_to_shards(gfull[n], SHARD_AXIS[n])[d] for n in SHARDED]) for d in range(N_DEV)])
    rows = sh_parts.shape[1] + rep_parts.shape[0]
    total_rows = -(-rows // 512) * 512
    parts = jnp.concatenate([sh_parts, jnp.broadcast_to(rep_parts, (N_DEV,) + rep_parts.shape),
                             jnp.zeros((N_DEV, total_rows - rows, LANES), F32)], axis=1)
    recv = _all_to_all(parts)

    def mine(d):
        sh = _pack([d[n] for n in SHARDED])
        rp = _pack([d[n] for n in REPLICATED])
        return jnp.concatenate([sh, rp, jnp.zeros((total_rows - rows, LANES), F32)], axis=0)

    flat_out = _adamw(recv, mine(wts), mine(mom), mine(vel))
    n_sh_rows = sh_parts.shape[1]
    results = []
    for fo in flat_out:
        sh = _unpack(fo[:n_sh_rows], [wts[n].shape for n in SHARDED])
        rp = _unpack(fo[n_sh_rows:rows], [wts[n].shape for n in REPLICATED])
        by_name = dict(zip(SHARDED, sh))
        by_name.update(zip(REPLICATED, rp))
        results.append([by_name[n] for n in WEIGHTS])
    g_out, d_out, m_out, v_out = results
    return (loss, grad_x, *g_out, *d_out, *m_out, *v_out)
```

```python
import functools
import math

import jax
import jax.numpy as jnp
from jax import lax
from jax.experimental import pallas as pl
from jax.experimental.pallas import tpu as pltpu

F32 = jnp.float32
BF16 = jnp.bfloat16

N_DEV = 8
D_MODEL = 1024
DEPTH = 2
D_SSM = 256
N_GROUPS = 16
GROUP_CH = 16
N_STATE = 64
D_STATE = N_GROUPS * N_STATE
D_SB = 512
HEAD_DIM = 64
D_CONV = 256
D_IN = 2560
D_FF = 4096
EPS = 1e-6
Q_SCALE = HEAD_DIM ** -0.5
LOG2E = math.log2(math.e)
DEAD_LOG2 = -160.0

ADAM_LR = 0.001
ADAM_B1 = 0.9
ADAM_B2 = 0.999
ADAM_EPS = 1e-08
ADAM_WD = 0.01
ADAM_STEP = 10

LANES = 128
PIECE_ALIGN = 2048
VMEM_LIMIT = 56 * 1024 * 1024

WEIGHTS = ['norm_mix_pre', 'norm_mix_post', 'w_in', 'w_gate', 'b_gate', 'ssm_a_re', 'ssm_a_im', 'ssm_log_dt',
           'ssm_b_re', 'ssm_b_im', 'ssm_c_re', 'ssm_c_im', 'ssm_d', 'w_glu_val', 'w_glu_gate', 'w_attn_out',
           'conv_w', 'w_conv_out', 'w_mix_out', 'norm_ffn_pre', 'norm_ffn_post', 'w_ffn_up', 'w_ffn_down']
SHARD_AXIS = {'w_in': 2, 'w_gate': 2, 'w_glu_val': 2, 'w_glu_gate': 2, 'w_attn_out': 2, 'conv_w': 2,
              'w_conv_out': 2, 'w_mix_out': 1, 'w_ffn_up': 2, 'w_ffn_down': 1}
SHARDED = [n for n in WEIGHTS if n in SHARD_AXIS]
REPLICATED = [n for n in WEIGHTS if n not in SHARD_AXIS]


def _cparams(n_grid):
    return pltpu.CompilerParams(dimension_semantics=("arbitrary",) * n_grid, vmem_limit_bytes=VMEM_LIMIT)


def _dot(a, b):
    return jnp.dot(a.astype(BF16), b.astype(BF16), preferred_element_type=F32)


def _dot_nt(a, b):
    return lax.dot_general(a.astype(BF16), b.astype(BF16), (((1,), (1,)), ((), ())), preferred_element_type=F32)


def _dot_tn(a, b):
    return lax.dot_general(a.astype(BF16), b.astype(BF16), (((0,), (0,)), ((), ())), preferred_element_type=F32)


def _split_dot(a, b):
    hi = a.astype(BF16)
    lo = (a - hi.astype(F32)).astype(BF16)
    return jnp.dot(hi, b, preferred_element_type=F32) + jnp.dot(lo, b, preferred_element_type=F32)


def _rms_fwd(x, g):
    r = lax.rsqrt(jnp.mean(x * x, axis=-1, keepdims=True) + EPS)
    return x * r * g


def _rms_bwd(dy, x, g):
    r = lax.rsqrt(jnp.mean(x * x, axis=-1, keepdims=True) + EPS)
    dyg = dy * g
    dx = r * dyg - x * (r * r * r) * jnp.mean(dyg * x, axis=-1, keepdims=True)
    dg = jnp.sum(dy * x * r, axis=0, keepdims=True)
    return dx, dg


def _sigmoid(x):
    return 1.0 / (1.0 + jnp.exp(-x))


_GELU_C = math.sqrt(2.0 / math.pi)


def _gelu(y):
    return 0.5 * y * (1.0 + jnp.tanh(_GELU_C * (y + 0.044715 * y * y * y)))


def _gelu_grad(y):
    t = jnp.tanh(_GELU_C * (y + 0.044715 * y * y * y))
    return 0.5 * (1.0 + t) + 0.5 * y * (1.0 - t * t) * _GELU_C * (1.0 + 3.0 * 0.044715 * y * y)


def _full(shape):
    return pl.BlockSpec(shape, lambda *_: (0,) * len(shape))


def _peer(x, y, c, k):
    px = 1 - x if (k >> 2) & 1 else x
    py = 1 - y if (k >> 1) & 1 else y
    pc = 1 - c if k & 1 else c
    return px, py, pc


def _all_gather(shard):
    rows, lanes = shard.shape

    def body(src_ref, out_ref, send_sems, recv_sems, local_sem):
        x, y, c = lax.axis_index("x"), lax.axis_index("y"), lax.axis_index("c")
        me = 4 * x + 2 * y + c
        mine = pltpu.make_async_copy(src_ref, out_ref.at[me], local_sem)
        mine.start()
        copies = []
        for k in range(1, N_DEV):
            cp = pltpu.make_async_remote_copy(
                src_ref=src_ref, dst_ref=out_ref.at[me],
                send_sem=send_sems.at[k - 1], recv_sem=recv_sems.at[k - 1],
                device_id=_peer(x, y, c, k), device_id_type=pl.DeviceIdType.MESH)
            cp.start()
            copies.append(cp)
        for cp in copies:
            cp.wait()
        mine.wait()

    return pl.pallas_call(
        body, name="weights_all_gather",
        out_shape=jax.ShapeDtypeStruct((N_DEV, rows, lanes), shard.dtype),
        in_specs=[pl.BlockSpec(memory_space=pl.ANY)],
        out_specs=pl.BlockSpec(memory_space=pl.ANY),
        scratch_shapes=[pltpu.SemaphoreType.DMA((N_DEV - 1,)), pltpu.SemaphoreType.DMA((N_DEV - 1,)),
                        pltpu.SemaphoreType.DMA],
    )(shard)


def _all_to_all(parts):
    n, rows, lanes = parts.shape

    def body(src_ref, out_ref, send_sems, recv_sems, local_sem):
        x, y, c = lax.axis_index("x"), lax.axis_index("y"), lax.axis_index("c")
        me = 4 * x + 2 * y + c
        mine = pltpu.make_async_copy(src_ref.at[me], out_ref.at[me], local_sem)
        mine.start()
        copies = []
        for k in range(1, N_DEV):
            px, py, pc = _peer(x, y, c, k)
            cp = pltpu.make_async_remote_copy(
                src_ref=src_ref.at[4 * px + 2 * py + pc], dst_ref=out_ref.at[me],
                send_sem=send_sems.at[k - 1], recv_sem=recv_sems.at[k - 1],
                device_id=(px, py, pc), device_id_type=pl.DeviceIdType.MESH)
            cp.start()
            copies.append(cp)
        for cp in copies:
            cp.wait()
        mine.wait()

    return pl.pallas_call(
        body, name="grads_all_to_all",
        out_shape=jax.ShapeDtypeStruct((N_DEV, rows, lanes), parts.dtype),
        in_specs=[pl.BlockSpec(memory_space=pl.ANY)],
        out_specs=pl.BlockSpec(memory_space=pl.ANY),
        scratch_shapes=[pltpu.SemaphoreType.DMA((N_DEV - 1,)), pltpu.SemaphoreType.DMA((N_DEV - 1,)),
                        pltpu.SemaphoreType.DMA],
    )(parts)


def _adamw(recv, w, m, v):
    rows = w.shape[0]
    tr = 512
    assert rows % tr == 0
    c1 = 1.0 / (1.0 - ADAM_B1 ** ADAM_STEP)
    c2 = 1.0 / (1.0 - ADAM_B2 ** ADAM_STEP)

    def body(r_ref, w_ref, m_ref, v_ref, g_ref, d_ref, mo_ref, vo_ref):
        g = r_ref[0]
        for s in range(1, N_DEV):
            g = g + r_ref[s]
        mn = ADAM_B1 * m_ref[...] + (1.0 - ADAM_B1) * g
        vn = ADAM_B2 * v_ref[...] + (1.0 - ADAM_B2) * (g * g)
        upd = (mn * c1) / (jnp.sqrt(vn * c2) + ADAM_EPS) + ADAM_WD * w_ref[...]
        g_ref[...] = g
        d_ref[...] = -ADAM_LR * upd
        mo_ref[...] = mn
        vo_ref[...] = vn

    spec = pl.BlockSpec((tr, LANES), lambda i: (i, 0))
    return pl.pallas_call(
        body, name="adamw", grid=(rows // tr,),
        out_shape=[jax.ShapeDtypeStruct((rows, LANES), F32)] * 4,
        in_specs=[pl.BlockSpec((N_DEV, tr, LANES), lambda i: (0, i, 0)), spec, spec, spec],
        out_specs=[spec] * 4,
        compiler_params=_cparams(1),
    )(recv, w, m, v)


def _tn_matmul(a, b, name):
    s, m = a.shape
    n = b.shape[1]
    tk = min(512, s)
    tm = min(1024, m)
    tn = max(c for c in range(LANES, 1280 + 1, LANES) if n % c == 0)
    assert m % tm == 0 and s % tk == 0
    nk = s // tk

    def body(a_ref, b_ref, o_ref):
        k = pl.program_id(2)

        @pl.when(k == 0)
        def _():
            o_ref[...] = jnp.zeros_like(o_ref)

        o_ref[...] += _dot_tn(a_ref[...], b_ref[...])

    return pl.pallas_call(
        body, name=name, grid=(m // tm, n // tn, nk),
        out_shape=jax.ShapeDtypeStruct((m, n), F32),
        in_specs=[pl.BlockSpec((tk, tm), lambda i, j, k: (k, i)), pl.BlockSpec((tk, tn), lambda i, j, k: (k, j))],
        out_specs=pl.BlockSpec((tm, tn), lambda i, j, k: (i, j)),
        compiler_params=_cparams(3),
    )(a, b)


def _token_tile(s):
    return min(256, s)


def _pre_fwd(x, n1, w_in):
    s = x.shape[0]
    tb = _token_tile(s)

    def body(x_ref, n_ref, w_ref, u_ref, q_ref, k_ref, v_ref, c_ref):
        h = _rms_fwd(x_ref[...], n_ref[...])
        p = _dot(h, w_ref[...])
        u_ref[...] = p[:, 0:256]
        q_ref[...] = (p[:, 256:768] * (Q_SCALE * LOG2E)).astype(BF16)
        k_ref[...] = p[:, 768:1280].astype(BF16)
        v_ref[...] = p[:, 1280:1792].astype(BF16)
        c_ref[...] = p[:, 1792:2560]

    def tok(w):
        return pl.BlockSpec((tb, w), lambda i: (i, 0))

    return pl.pallas_call(
        body, name="pre_fwd", grid=(s // tb,),
        out_shape=[jax.ShapeDtypeStruct((s, D_SSM), F32), jax.ShapeDtypeStruct((s, D_SB), BF16),
                   jax.ShapeDtypeStruct((s, D_SB), BF16), jax.ShapeDtypeStruct((s, D_SB), BF16),
                   jax.ShapeDtypeStruct((s, 3 * D_CONV), F32)],
        in_specs=[tok(D_MODEL), _full((1, D_MODEL)), _full((D_MODEL, D_IN))],
        out_specs=[tok(D_SSM), tok(D_SB), tok(D_SB), tok(D_SB), tok(3 * D_CONV)],
        compiler_params=_cparams(1),
    )(x, n1, w_in)


def _ssm_discretize(lr, li, ldt, br, bi):
    dt = jnp.exp(ldt)
    mag = jnp.exp(lr * dt)
    ab_re = mag * jnp.cos(li * dt)
    ab_im = mag * jnp.sin(li * dt)
    den = lr * lr + li * li
    xr = ab_re - 1.0
    coef_re = (xr * lr + ab_im * li) / den
    coef_im = (ab_im * lr - xr * li) / den
    bb_re = coef_re * br - coef_im * bi
    bb_im = coef_re * bi + coef_im * br
    return ab_re, ab_im, bb_re, bb_im


def _ssm_params_fwd(lr, li, ldt, br, bi):
    def body(lr_ref, li_ref, ldt_ref, br_ref, bi_ref, ar_ref, ai_ref, bbr_ref, bbi_ref):
        ar, ai, bbr, bbi = _ssm_discretize(lr_ref[...], li_ref[...], ldt_ref[...], br_ref[...], bi_ref[...])
        ar_ref[...] = ar
        ai_ref[...] = ai
        bbr_ref[...] = bbr
        bbi_ref[...] = bbi

    row = jax.ShapeDtypeStruct((1, D_STATE), F32)
    mat = jax.ShapeDtypeStruct((GROUP_CH, D_STATE), F32)
    return pl.pallas_call(body, name="ssm_params_fwd", out_shape=[row, row, mat, mat])(lr, li, ldt, br, bi)


def _ssm_params_bwd(lr, li, ldt, br, bi, d_ar, d_ai, d_bbr, d_bbi):
    def body(lr_ref, li_ref, ldt_ref, br_ref, bi_ref, dar_ref, dai_ref, dbbr_ref, dbbi_ref,
             glr_ref, gli_ref, gdt_ref, gbr_ref, gbi_ref):
        _, vjp = jax.vjp(_ssm_discretize, lr_ref[...], li_ref[...], ldt_ref[...], br_ref[...], bi_ref[...])
        glr, gli, gdt, gbr, gbi = vjp((dar_ref[...], dai_ref[...], dbbr_ref[...], dbbi_ref[...]))
        glr_ref[...] = glr
        gli_ref[...] = gli
        gbr_ref[...] = gbr
        gbi_ref[...] = gbi
        grp = lax.broadcasted_iota(jnp.int32, (N_GROUPS, D_STATE), 0)
        col = lax.broadcasted_iota(jnp.int32, (N_GROUPS, D_STATE), 1)
        own = (col // N_STATE) == grp
        per_group = jnp.sum(jnp.where(own, jnp.broadcast_to(gdt, (N_GROUPS, D_STATE)), 0.0), axis=1, keepdims=True)
        gdt_ref[...] = jnp.broadcast_to(per_group, (N_GROUPS, LANES))

    row = jax.ShapeDtypeStruct((1, D_STATE), F32)
    mat = jax.ShapeDtypeStruct((GROUP_CH, D_STATE), F32)
    return pl.pallas_call(
        body, name="ssm_params_bwd",
        out_shape=[row, row, jax.ShapeDtypeStruct((N_GROUPS, LANES), F32), mat, mat],
    )(lr, li, ldt, br, bi, d_ar, d_ai, d_bbr, d_bbi)


def _ssm_fwd(u, ab_re, ab_im, bb_re, bb_im, cc_re, cc_im, d_skip):
    s = u.shape[0]
    ts = _token_tile(s)
    nt = s // ts

    def body(u_ref, ar_ref, ai_ref, bbr_ref, bbi_ref, ccr_ref, cci_ref, d_ref, y_ref, ypre_ref, st_ref,
             bur, bui, hr_s, hi_s, cr, ci):
        @pl.when(pl.program_id(0) == 0)
        def _():
            cr[...] = jnp.zeros_like(cr)
            ci[...] = jnp.zeros_like(ci)

        u = u_ref[...]
        bur[...] = _dot(u, bbr_ref[...])
        bui[...] = _dot(u, bbi_ref[...])
        st_ref[0, 0:1, :] = cr[...]
        st_ref[0, 1:2, :] = ci[...]
        ar = ar_ref[...]
        ai = ai_ref[...]

        def step(t, carry):
            hr, hi = carry
            nr = ar * hr - ai * hi + bur[pl.ds(t, 1), :]
            ni = ar * hi + ai * hr + bui[pl.ds(t, 1), :]
            hr_s[pl.ds(t, 1), :] = nr
            hi_s[pl.ds(t, 1), :] = ni
            return nr, ni

        hr, hi = lax.fori_loop(0, ts, step, (cr[...], ci[...]), unroll=8)
        cr[...] = hr
        ci[...] = hi
        y = _dot_nt(hr_s[...], ccr_ref[...]) - _dot_nt(hi_s[...], cci_ref[...]) + d_ref[...] * u
        ypre_ref[...] = y
        y_ref[...] = _gelu(y).astype(BF16)

    tok = pl.BlockSpec((ts, D_SSM), lambda i: (i, 0))
    row = _full((1, D_STATE))
    mat = _full((D_SSM, D_STATE))
    return pl.pallas_call(
        body, name="ssm_fwd", grid=(nt,),
        out_shape=[jax.ShapeDtypeStruct((s, D_SSM), BF16), jax.ShapeDtypeStruct((s, D_SSM), F32),
                   jax.ShapeDtypeStruct((nt, 2, D_STATE), F32)],
        in_specs=[tok, row, row, mat, mat, mat, mat, _full((1, D_SSM))],
        out_specs=[tok, tok, pl.BlockSpec((1, 2, D_STATE), lambda i: (i, 0, 0))],
        scratch_shapes=[pltpu.VMEM((ts, D_STATE), F32)] * 4 + [pltpu.VMEM((1, D_STATE), F32)] * 2,
        compiler_params=_cparams(1),
    )(u, ab_re, ab_im, bb_re, bb_im, cc_re, cc_im, d_skip)


def _attn_tile(s):
    return min(256, s)


def _sb_block(z, valid):
    lp = jnp.log2(1.0 + jnp.exp2(-jnp.abs(z)))
    ls = jnp.minimum(z, 0.0) - lp
    l1m = ls - z
    if valid is not None:
        l1m = jnp.where(valid, l1m, 0.0)
    return ls, l1m


def _attn_fwd(q, k, v):
    s = q.shape[0]
    t = _attn_tile(s)
    nq = s // t
    assert nq <= LANES

    def body(q_ref, k_ref, v_ref, o_ref, tab_ref, z_s, ls_s, lg_s, q_s, tri_s, acc_s):
        i = pl.program_id(1)
        lane = lax.broadcasted_iota(jnp.int32, (t, LANES), 1)
        first = lane < HEAD_DIM
        row = lax.broadcasted_iota(jnp.int32, (t, t), 0)
        col = lax.broadcasted_iota(jnp.int32, (t, t), 1)
        tri_s[...] = (row > col).astype(BF16)
        causal = col < row
        q2 = q_ref[...]
        zero = jnp.zeros_like(q2)
        q_s[0] = jnp.where(first, q2, zero)
        q_s[1] = jnp.where(first, zero, q2)
        acc_s[...] = jnp.zeros_like(acc_s)
        tab_ref[...] = jnp.full_like(tab_ref, -1e30)

        def scores(j):
            k2 = k_ref[pl.ds(pl.multiple_of(jnp.maximum(j, 0) * t, t), t), :]
            for a in range(2):
                z_s[a] = _dot_nt(q_s[a], k2)

        def logits(j, carries, valid):
            out = []
            for a in range(2):
                c1 = carries[a]
                ls, l1m = _sb_block(z_s[a], valid)
                ls_s[a] = ls + c1
                suffix = _dot(l1m, tri_s[...])
                lg_s[a] = ls_s[a] + suffix
                cols = slice(a * LANES, (a + 1) * LANES)
                tab_ref[:, cols] = jnp.where(lane == j, c1, tab_ref[:, cols])
                out.append(c1 + suffix[:, 0:1] + l1m[:, 0:1])
            return tuple(out)

        def accumulate(j, valid):
            v2 = v_ref[pl.ds(pl.multiple_of(jnp.maximum(j, 0) * t, t), t), :]
            for a in range(2):
                w = jnp.exp2(lg_s[a])
                if valid is not None:
                    w = jnp.where(valid, w, 0.0)
                acc_s[a] += _dot(w, v2)

        carries = tuple(jnp.zeros((t, 1), F32) for _ in range(2))
        scores(i)
        carries = logits(i, carries, causal)
        scores(i - 1)
        accumulate(i, causal)
        carries = logits(i - 1, carries, None)
        scores(i - 2)

        def trip(n, carries):
            accumulate(i - n + 2, None)
            carries = logits(i - n + 1, carries, None)
            scores(i - n)
            return carries

        def alive(c):
            n, carries = c
            return (n <= i + 1) & (jnp.max(jnp.maximum(carries[0], carries[1])) >= DEAD_LOG2)

        n_end, _ = lax.while_loop(alive, lambda c: (c[0] + 1, trip(c[0], c[1])), (jnp.int32(3), carries))

        @pl.when(i - n_end + 2 >= 0)
        def _():
            accumulate(i - n_end + 2, None)

        o_ref[...] = jnp.where(first, acc_s[0], acc_s[1]).astype(BF16)

    return pl.pallas_call(
        body, name="attn_fwd", grid=(D_SB // LANES, nq),
        out_shape=[jax.ShapeDtypeStruct((s, D_SB), BF16), jax.ShapeDtypeStruct((s, 2 * D_SB), F32)],
        in_specs=[pl.BlockSpec((t, LANES), lambda p, i: (i, p)), pl.BlockSpec((s, LANES), lambda p, i: (0, p)),
                  pl.BlockSpec((s, LANES), lambda p, i: (0, p))],
        out_specs=[pl.BlockSpec((t, LANES), lambda p, i: (i, p)), pl.BlockSpec((t, 2 * LANES), lambda p, i: (i, p))],
        scratch_shapes=[pltpu.VMEM((2, t, t), F32)] * 3 + [pltpu.VMEM((2, t, LANES), BF16), pltpu.VMEM((t, t), BF16),
                                                           pltpu.VMEM((2, t, LANES), F32)],
        compiler_params=_cparams(2),
    )(q, k, v)


def _conv_taps(z, halo, rowi):
    z1 = jnp.where(rowi == 0, halo[7:8, :], pltpu.roll(z, 1, 0))
    z2 = jnp.where(rowi == 0, halo[6:7, :], jnp.where(rowi == 1, halo[7:8, :], pltpu.roll(z, 2, 0)))
    return z1, z2


def _merge_branches(x_ref, y_ref, o_ref, c_ref, ch_ref, n1_ref, wg_ref, bg_ref, wv_ref, wsg_ref, wao_ref,
                    cw_ref, wc_ref, tb):
    i = pl.program_id(0)
    h = _rms_fwd(x_ref[...], n1_ref[...])
    g = _sigmoid(_dot(h, wg_ref[...]) + bg_ref[...])
    y256 = y_ref[...]
    val = _dot(y256, wv_ref[...])
    sg = _sigmoid(_dot(y256, wsg_ref[...]))
    ysb = _dot(o_ref[...], wao_ref[...])
    c3 = c_ref[...]
    cb, cc, cx = c3[:, 0:256], c3[:, 256:512], c3[:, 512:768]
    ch = ch_ref[...]
    z = cc * cx
    zh = jnp.where(i > 0, ch[:, 256:512] * ch[:, 512:768], 0.0)
    rowi = lax.broadcasted_iota(jnp.int32, (tb, D_CONV), 0)
    z1, z2 = _conv_taps(z, zh, rowi)
    cw = cw_ref[...]
    cv = cw[0:1, :] * z2 + cw[1:2, :] * z1 + cw[2:3, :] * z
    yc = _dot(cb * cv, wc_ref[...])
    return dict(h=h, g=g, val=val, sg=sg, ysb=ysb, cb=cb, cv=cv, yc=yc)


def _merge_fwd(x, y256, o, c3, w):
    s = x.shape[0]
    tb = _token_tile(s)

    def body(x_ref, y_ref, o_ref, c_ref, ch_ref, n1_ref, wg_ref, bg_ref, wv_ref, wsg_ref, wao_ref, cw_ref, wc_ref,
             wm_ref, n2_ref, x1_ref, m2_ref, mg_ref):
        br = _merge_branches(x_ref, y_ref, o_ref, c_ref, ch_ref, n1_ref, wg_ref, bg_ref, wv_ref, wsg_ref, wao_ref,
                             cw_ref, wc_ref, tb)
        g = br["g"]
        merged = (g[:, 0:1024] * (br["val"] * br["sg"]) + g[:, 1024:2048] * br["ysb"] + g[:, 2048:3072] * br["yc"])
        mg_ref[...] = merged.astype(BF16)
        m2 = _dot(merged, wm_ref[...])
        m2_ref[...] = m2
        x1_ref[...] = x_ref[...] + _rms_fwd(m2, n2_ref[...])

    def tok(wd):
        return pl.BlockSpec((tb, wd), lambda i: (i, 0))

    halo = pl.BlockSpec((8, 3 * D_CONV), lambda i: (jnp.maximum(i * (tb // 8) - 1, 0), 0))
    return pl.pallas_call(
        body, name="merge_fwd", grid=(s // tb,),
        out_shape=[jax.ShapeDtypeStruct((s, D_MODEL), F32), jax.ShapeDtypeStruct((s, D_MODEL), F32),
                   jax.ShapeDtypeStruct((s, D_MODEL), BF16)],
        in_specs=[tok(D_MODEL), tok(D_SSM), tok(D_SB), tok(3 * D_CONV), halo,
                  _full((1, D_MODEL)), _full((D_MODEL, 3 * D_MODEL)), _full((1, 3 * D_MODEL)),
                  _full((D_SSM, D_MODEL)), _full((D_SSM, D_MODEL)), _full((D_SB, D_MODEL)),
                  _full((8, D_CONV)), _full((D_CONV, D_MODEL)), _full((D_MODEL, D_MODEL)), _full((1, D_MODEL))],
        out_specs=[tok(D_MODEL), tok(D_MODEL), tok(D_MODEL)],
        compiler_params=_cparams(1),
    )(x, y256, o, c3, c3, w["n1"], w["w_gate"], w["b_gate"], w["w_glu_val"], w["w_glu_gate"], w["w_attn_out"],
      w["conv_w"], w["w_conv_out"], w["w_mix_out"], w["n2"])


FF_CHUNK = 1024


def _ffn_fwd(x1, n3, w_up, w_dn, n4):
    s = x1.shape[0]
    tb = min(512, s)
    nh = D_FF // FF_CHUNK

    def body(x_ref, n3_ref, wu_ref, wd_ref, n4_ref, x2_ref, f_ref, h_s, acc):
        j = pl.program_id(1)

        @pl.when(j == 0)
        def _():
            h_s[...] = _rms_fwd(x_ref[...], n3_ref[...]).astype(BF16)
            acc[...] = jnp.zeros_like(acc)

        a = jnp.maximum(_dot(h_s[...], wu_ref[...]), 0.0)
        acc[...] += _dot(a * a, wd_ref[...])

        @pl.when(j == nh - 1)
        def _():
            f = acc[...]
            f_ref[...] = f
            x2_ref[...] = x_ref[...] + _rms_fwd(f, n4_ref[...])

    tok = pl.BlockSpec((tb, D_MODEL), lambda i, j: (i, 0))
    return pl.pallas_call(
        body, name="ffn_fwd", grid=(s // tb, nh),
        out_shape=[jax.ShapeDtypeStruct((s, D_MODEL), F32)] * 2,
        in_specs=[tok, _full((1, D_MODEL)), pl.BlockSpec((D_MODEL, FF_CHUNK), lambda i, j: (0, j)),
                  pl.BlockSpec((FF_CHUNK, D_MODEL), lambda i, j: (j, 0)), _full((1, D_MODEL))],
        out_specs=[tok, tok],
        scratch_shapes=[pltpu.VMEM((tb, D_MODEL), BF16), pltpu.VMEM((tb, D_MODEL), F32)],
        compiler_params=_cparams(2),
    )(x1, n3, w_up, w_dn, n4)


def _loss_head(y, target):
    s = y.shape[0]
    tb = _token_tile(s)

    def body(y_ref, t_ref, dy_ref, l_ref):
        @pl.when(pl.program_id(0) == 0)
        def _():
            l_ref[...] = jnp.zeros_like(l_ref)

        err = y_ref[...] - t_ref[...]
        dy_ref[...] = err * (1.0 / D_MODEL)
        l_ref[...] += 0.5 * jnp.sum(jnp.mean(err * err, axis=-1, keepdims=True), axis=0, keepdims=True)

    tok = pl.BlockSpec((tb, D_MODEL), lambda i: (i, 0))
    return pl.pallas_call(
        body, name="loss_head", grid=(s // tb,),
        out_shape=[jax.ShapeDtypeStruct((s, D_MODEL), F32), jax.ShapeDtypeStruct((8, LANES), F32)],
        in_specs=[tok, tok], out_specs=[tok, _full((8, LANES))],
        compiler_params=_cparams(1),
    )(y, target)


def _ffn_bwd(dx2, x1, f, n3, w_up, w_dn, n4):
    s = x1.shape[0]
    tb = min(512, s)
    nh = D_FF // FF_CHUNK
    nt = s // tb

    def body(dx2_ref, x_ref, f_ref, n3_ref, wu_ref, wd_ref, n4_ref,
             dx1_ref, r_ref, da_ref, h_ref, df_ref, dn3_ref, dn4_ref, acc):
        i = pl.program_id(0)
        j = pl.program_id(1)

        @pl.when((i == 0) & (j == 0))
        def _():
            dn3_ref[...] = jnp.zeros_like(dn3_ref)
            dn4_ref[...] = jnp.zeros_like(dn4_ref)

        @pl.when(j == 0)
        def _():
            h_ref[...] = _rms_fwd(x_ref[...], n3_ref[...]).astype(BF16)
            df, dn4 = _rms_bwd(dx2_ref[...], f_ref[...], n4_ref[...])
            df_ref[...] = df.astype(BF16)
            dn4_ref[...] += dn4
            acc[...] = jnp.zeros_like(acc)

        a = jnp.maximum(_dot(h_ref[...], wu_ref[...]), 0.0)
        r_ref[...] = (a * a).astype(BF16)
        da = (_dot_nt(df_ref[...], wd_ref[...]) * (2.0 * a)).astype(BF16)
        da_ref[...] = da
        acc[...] += _dot_nt(da, wu_ref[...])

        @pl.when(j == nh - 1)
        def _():
            dx, dn3 = _rms_bwd(acc[...], x_ref[...], n3_ref[...])
            dx1_ref[...] = dx2_ref[...] + dx
            dn3_ref[...] += dn3

    tok = pl.BlockSpec((tb, D_MODEL), lambda i, j: (i, 0))
    hid = pl.BlockSpec((tb, FF_CHUNK), lambda i, j: (i, j))
    tok_b = jax.ShapeDtypeStruct((s, D_MODEL), BF16)
    hid_b = jax.ShapeDtypeStruct((s, D_FF), BF16)
    row = jax.ShapeDtypeStruct((1, D_MODEL), F32)
    return pl.pallas_call(
        body, name="ffn_bwd", grid=(nt, nh),
        out_shape=[jax.ShapeDtypeStruct((s, D_MODEL), F32), hid_b, hid_b, tok_b, tok_b, row, row],
        in_specs=[tok, tok, tok, _full((1, D_MODEL)), pl.BlockSpec((D_MODEL, FF_CHUNK), lambda i, j: (0, j)),
                  pl.BlockSpec((FF_CHUNK, D_MODEL), lambda i, j: (j, 0)), _full((1, D_MODEL))],
        out_specs=[tok, hid, hid, tok, tok, _full((1, D_MODEL)), _full((1, D_MODEL))],
        scratch_shapes=[pltpu.VMEM((tb, D_MODEL), F32)],
        compiler_params=_cparams(2),
    )(dx2, x1, f, n3, w_up, w_dn, n4)


def _merge_bwd(dx1, x, m2, y256, o, c3, w):
    s = x.shape[0]
    tb = _token_tile(s)

    def body(dx1_ref, m2_ref, x_ref, y_ref, o_ref, c_ref, ch_ref, n1_ref, wg_ref, bg_ref, wv_ref, wsg_ref, wao_ref,
             cw_ref, wc_ref, wm_ref, n2_ref,
             dhg_ref, h_ref, dgz_ref, dvs_ref, dy_ref, do_ref, dyb_ref, dyc_ref, bcv_ref, dcv_ref, dcb_ref, dm2_ref,
             dbg_ref, dn2_ref):
        @pl.when(pl.program_id(0) == 0)
        def _():
            dbg_ref[...] = jnp.zeros_like(dbg_ref)
            dn2_ref[...] = jnp.zeros_like(dn2_ref)

        dm2, dn2 = _rms_bwd(dx1_ref[...], m2_ref[...], n2_ref[...])
        dn2_ref[...] += dn2
        dm2_ref[...] = dm2.astype(BF16)
        dmg = _dot_nt(dm2, wm_ref[...])
        br = _merge_branches(x_ref, y_ref, o_ref, c_ref, ch_ref, n1_ref, wg_ref, bg_ref, wv_ref, wsg_ref, wao_ref,
                             cw_ref, wc_ref, tb)
        g = br["g"]
        g1, g2, g3 = g[:, 0:1024], g[:, 1024:2048], g[:, 2048:3072]
        val, sg = br["val"], br["sg"]
        h_ref[...] = br["h"].astype(BF16)
        dgz = jnp.concatenate([dmg * (val * sg) * g1 * (1.0 - g1), dmg * br["ysb"] * g2 * (1.0 - g2),
                               dmg * br["yc"] * g3 * (1.0 - g3)], axis=1)
        dbg_ref[...] += jnp.sum(dgz, axis=0, keepdims=True)
        dgz_ref[...] = dgz.astype(BF16)
        dhg_ref[...] = _dot_nt(dgz, wg_ref[...])
        dys = dmg * g1
        dval = dys * sg
        dsg = dys * val * sg * (1.0 - sg)
        dvs_ref[:, 0:1024] = dval.astype(BF16)
        dvs_ref[:, 1024:2048] = dsg.astype(BF16)
        dy_ref[...] = _dot_nt(dval, wv_ref[...]) + _dot_nt(dsg, wsg_ref[...])
        dyb = (dmg * g2).astype(BF16)
        dyb_ref[...] = dyb
        do_ref[...] = _dot_nt(dyb, wao_ref[...]).astype(BF16)
        dyc = (dmg * g3).astype(BF16)
        dyc_ref[...] = dyc
        dcq = _dot_nt(dyc, wc_ref[...])
        bcv_ref[...] = (br["cb"] * br["cv"]).astype(BF16)
        dcb_ref[...] = dcq * br["cv"]
        dcv_ref[...] = dcq * br["cb"]

    def tok(wd):
        return pl.BlockSpec((tb, wd), lambda i: (i, 0))

    def out(wd, dt):
        return jax.ShapeDtypeStruct((s, wd), dt)

    halo = pl.BlockSpec((8, 3 * D_CONV), lambda i: (jnp.maximum(i * (tb // 8) - 1, 0), 0))
    return pl.pallas_call(
        body, name="merge_bwd", grid=(s // tb,),
        out_shape=[out(D_MODEL, F32), out(D_MODEL, BF16), out(3 * D_MODEL, BF16), out(2 * D_MODEL, BF16),
                   out(D_SSM, F32), out(D_SB, BF16), out(D_MODEL, BF16), out(D_MODEL, BF16), out(D_CONV, BF16),
                   out(D_CONV, F32), out(D_CONV, F32), out(D_MODEL, BF16),
                   jax.ShapeDtypeStruct((1, 3 * D_MODEL), F32), jax.ShapeDtypeStruct((1, D_MODEL), F32)],
        in_specs=[tok(D_MODEL), tok(D_MODEL), tok(D_MODEL), tok(D_SSM), tok(D_SB), tok(3 * D_CONV), halo,
                  _full((1, D_MODEL)), _full((D_MODEL, 3 * D_MODEL)), _full((1, 3 * D_MODEL)),
                  _full((D_SSM, D_MODEL)), _full((D_SSM, D_MODEL)), _full((D_SB, D_MODEL)),
                  _full((8, D_CONV)), _full((D_CONV, D_MODEL)), _full((D_MODEL, D_MODEL)), _full((1, D_MODEL))],
        out_specs=[tok(D_MODEL), tok(D_MODEL), tok(3 * D_MODEL), tok(2 * D_MODEL), tok(D_SSM), tok(D_SB),
                   tok(D_MODEL), tok(D_MODEL), tok(D_CONV), tok(D_CONV), tok(D_CONV), tok(D_MODEL),
                   _full((1, 3 * D_MODEL)), _full((1, D_MODEL))],
        compiler_params=_cparams(1),
    )(dx1, m2, x, y256, o, c3, c3, w["n1"], w["w_gate"], w["b_gate"], w["w_glu_val"], w["w_glu_gate"],
      w["w_attn_out"], w["conv_w"], w["w_conv_out"], w["w_mix_out"], w["n2"])


def _ssm_bwd(u, ypre, dy, states, ab_re, ab_im, bb_re, bb_im, cc_re, cc_im, d_skip):
    s = u.shape[0]
    ts = _token_tile(s)
    nt = s // ts

    def body(u_ref, yp_ref, dy_ref, st_ref, ar_ref, ai_ref, bbr_ref, bbi_ref, ccr_ref, cci_ref, d_ref,
             du_ref, dar_ref, dai_ref, dbbr_ref, dbbi_ref, dccr_ref, dcci_ref, dd_ref,
             bur, bui, hr_s, hi_s, pr_s, pi_s, lr_s, li_s, cr, ci):
        @pl.when(pl.program_id(0) == 0)
        def _():
            cr[...] = jnp.zeros_like(cr)
            ci[...] = jnp.zeros_like(ci)
            for ref in (dar_ref, dai_ref, dbbr_ref, dbbi_ref, dccr_ref, dcci_ref, dd_ref):
                ref[...] = jnp.zeros_like(ref)

        u = u_ref[...]
        ub = u.astype(BF16)
        bur[...] = _dot(ub, bbr_ref[...])
        bui[...] = _dot(ub, bbi_ref[...])
        ar = ar_ref[...]
        ai = ai_ref[...]

        def fwd_step(t, carry):
            hr, hi = carry
            pr_s[pl.ds(t, 1), :] = hr
            pi_s[pl.ds(t, 1), :] = hi
            nr = ar * hr - ai * hi + bur[pl.ds(t, 1), :]
            ni = ar * hi + ai * hr + bui[pl.ds(t, 1), :]
            hr_s[pl.ds(t, 1), :] = nr
            hi_s[pl.ds(t, 1), :] = ni
            return nr, ni

        lax.fori_loop(0, ts, fwd_step, (st_ref[0, 0:1, :], st_ref[0, 1:2, :]), unroll=8)

        dyp = dy_ref[...] * _gelu_grad(yp_ref[...])
        dypb = dyp.astype(BF16)
        lr_s[...] = _dot(dypb, ccr_ref[...])
        li_s[...] = -_dot(dypb, cci_ref[...])

        def bwd_step(tt, carry):
            t = ts - 1 - tt
            nr, ni = carry
            qr = lr_s[pl.ds(t, 1), :] + ar * nr + ai * ni
            qi = li_s[pl.ds(t, 1), :] + ar * ni - ai * nr
            lr_s[pl.ds(t, 1), :] = qr
            li_s[pl.ds(t, 1), :] = qi
            return qr, qi

        nr, ni = lax.fori_loop(0, ts, bwd_step, (cr[...], ci[...]), unroll=8)
        cr[...] = nr
        ci[...] = ni
        lam_r = lr_s[...]
        lam_i = li_s[...]
        pr = pr_s[...]
        pi = pi_s[...]
        dar_ref[...] += jnp.sum(lam_r * pr + lam_i * pi, axis=0, keepdims=True)
        dai_ref[...] += jnp.sum(lam_i * pr - lam_r * pi, axis=0, keepdims=True)
        lrb = lam_r.astype(BF16)
        lib = lam_i.astype(BF16)
        du_ref[...] = _dot_nt(lrb, bbr_ref[...]) + _dot_nt(lib, bbi_ref[...]) + d_ref[...] * dyp
        dbbr_ref[...] += _dot_tn(ub, lrb)
        dbbi_ref[...] += _dot_tn(ub, lib)
        dccr_ref[...] += _dot_tn(dypb, hr_s[...])
        dcci_ref[...] -= _dot_tn(dypb, hi_s[...])
        dd_ref[...] += jnp.sum(dyp * u, axis=0, keepdims=True)

    tok = pl.BlockSpec((ts, D_SSM), lambda i: (nt - 1 - i, 0))
    row = _full((1, D_STATE))
    mat = _full((D_SSM, D_STATE))
    row_o = jax.ShapeDtypeStruct((1, D_STATE), F32)
    mat_o = jax.ShapeDtypeStruct((D_SSM, D_STATE), F32)
    return pl.pallas_call(
        body, name="ssm_bwd", grid=(nt,),
        out_shape=[jax.ShapeDtypeStruct((s, D_SSM), F32), row_o, row_o, mat_o, mat_o, mat_o, mat_o,
                   jax.ShapeDtypeStruct((1, D_SSM), F32)],
        in_specs=[tok, tok, tok, pl.BlockSpec((1, 2, D_STATE), lambda i: (nt - 1 - i, 0, 0)),
                  row, row, mat, mat, mat, mat, _full((1, D_SSM))],
        out_specs=[tok, row, row, mat, mat, mat, mat, _full((1, D_SSM))],
        scratch_shapes=[pltpu.VMEM((ts, D_STATE), F32)] * 8 + [pltpu.VMEM((1, D_STATE), F32)] * 2,
        compiler_params=_cparams(1),
    )(u, ypre, dy, states, ab_re, ab_im, bb_re, bb_im, cc_re, cc_im, d_skip)


def _attn_bwd(q, k, v, do, tab):
    s = q.shape[0]
    t = _attn_tile(s)
    nq = s // t
    n_pairs = D_SB // LANES

    def body(q_ref, do_ref, tab_ref, k_ref, v_ref, dq_ref, dk_hbm, dv_hbm, dk_s, dv_s, sems,
             z_s, ls_s, suf_s, dw_s, dl_s, beta_s, pre_s, wb_s, dq_s, q_s, do_s, qt_s, dot_s, tri_s):
        p = pl.program_id(0)
        i = pl.program_id(1)

        @pl.when(i == 0)
        def _():
            dk_s[...] = jnp.zeros_like(dk_s)
            dv_s[...] = jnp.zeros_like(dv_s)

        lane = lax.broadcasted_iota(jnp.int32, (t, LANES), 1)
        first = lane < HEAD_DIM
        row = lax.broadcasted_iota(jnp.int32, (t, t), 0)
        col = lax.broadcasted_iota(jnp.int32, (t, t), 1)
        tri_s[0] = (row > col).astype(BF16)
        tri_s[1] = (row < col).astype(BF16)
        causal = col < row
        q2 = q_ref[...]
        do2 = do_ref[...]
        zero = jnp.zeros_like(q2)
        for a in range(2):
            mine = first if a == 0 else jnp.logical_not(first)
            q_s[a] = jnp.where(mine, q2, zero)
            do_s[a] = jnp.where(mine, do2, zero)
            qt_s[a] = jnp.where(mine, q2, zero).astype(F32).T.astype(BF16)
            dot_s[a] = jnp.where(mine, do2, zero).astype(F32).T.astype(BF16)
        z_s[...] = jnp.full_like(z_s, -1e30)
        ls_s[...] = jnp.full_like(ls_s, -1e30)
        for ref in (suf_s, dw_s, dl_s, beta_s, pre_s, wb_s, dq_s):
            ref[...] = jnp.zeros_like(ref)

        def tile_rows(j):
            return pl.ds(pl.multiple_of(jnp.maximum(j, 0) * t, t), t)

        def scores(j):
            k2 = k_ref[tile_rows(j), :]
            for a in range(2):
                z_s[a] = _dot_nt(q_s[a], k2)

        def spread(j, valid):
            v2 = v_ref[tile_rows(j), :]
            for a in range(2):
                ls, l1m = _sb_block(z_s[a], valid)
                ls_s[a] = ls
                suf_s[a] = _dot(l1m, tri_s[0])
                dw_s[a] = _dot_nt(do_s[a], v2)

        def weights(j, valid):
            for a in range(2):
                c1 = jnp.sum(jnp.where(lane == j, tab_ref[:, a * LANES:(a + 1) * LANES], 0.0), axis=1, keepdims=True)
                ls = ls_s[a]
                w = jnp.exp2(ls + suf_s[a] + c1)
                if valid is not None:
                    w = jnp.where(valid, w, 0.0)
                dl = w * dw_s[a]
                dl_s[a] = dl
                beta_s[a] = jnp.exp2(ls)
                wb_s[a] = w.astype(BF16)
                pre_s[a] = _dot(dl, tri_s[1])

        def grads(j, c2s, valid):
            rows = tile_rows(j)
            k2 = k_ref[rows, :]
            out = []
            dkt = dvt = None
            for a in range(2):
                dl = dl_s[a]
                pre = pre_s[a]
                dz = dl - beta_s[a] * (dl + pre + c2s[a])
                if valid is not None:
                    dz = jnp.where(valid, dz, 0.0)
                dzb = dz.astype(BF16)
                dq_s[a] += _dot(dzb, k2)
                dk_a = jnp.dot(qt_s[a], dzb, preferred_element_type=F32)
                dv_a = jnp.dot(dot_s[a], wb_s[a], preferred_element_type=F32)
                dkt = dk_a if dkt is None else dkt + dk_a
                dvt = dv_a if dvt is None else dvt + dv_a
                out.append(c2s[a] + pre[:, t - 1:t] + dl[:, t - 1:t])
            dk_s[:, rows] += dkt
            dv_s[:, rows] += dvt
            return tuple(out)

        def trip(n, c2s):
            c2s = grads(n - 3, c2s, None)
            weights(n - 2, None)
            spread(n - 1, None)
            scores(n)
            return c2s

        reach = jnp.max(jnp.maximum(tab_ref[:, 0:LANES], tab_ref[:, LANES:2 * LANES]), axis=0, keepdims=True)
        tile_id = lax.broadcasted_iota(jnp.int32, (1, LANES), 1)
        j_min = jnp.min(jnp.where(reach >= DEAD_LOG2, tile_id, i))
        c2s = lax.fori_loop(j_min, i + 1, trip, tuple(jnp.zeros((t, 1), F32) for _ in range(2)))
        c2s = grads(i - 2, c2s, None)
        weights(i - 1, None)
        spread(i, causal)
        c2s = grads(i - 1, c2s, None)
        weights(i, causal)
        grads(i, c2s, causal)
        dq_ref[...] = Q_SCALE * jnp.where(first, dq_s[0], dq_s[1])

        @pl.when(i == nq - 1)
        def _():
            dk_s[...] = dk_s[...] * (1.0 / LOG2E)
            ck = pltpu.make_async_copy(dk_s, dk_hbm.at[p], sems.at[0])
            cv = pltpu.make_async_copy(dv_s, dv_hbm.at[p], sems.at[1])
            ck.start()
            cv.start()
            ck.wait()
            cv.wait()

    blk = pl.BlockSpec((t, LANES), lambda p, i: (i, p))
    seq = pl.BlockSpec((s, LANES), lambda p, i: (0, p))
    pairs = jax.ShapeDtypeStruct((n_pairs, LANES, s), F32)
    stage = pltpu.VMEM((2, t, t), F32)
    return pl.pallas_call(
        body, name="attn_bwd", grid=(n_pairs, nq),
        out_shape=[jax.ShapeDtypeStruct((s, D_SB), F32), pairs, pairs],
        in_specs=[blk, blk, pl.BlockSpec((t, 2 * LANES), lambda p, i: (i, p)), seq, seq],
        out_specs=[blk, pl.BlockSpec(memory_space=pl.ANY), pl.BlockSpec(memory_space=pl.ANY)],
        scratch_shapes=[pltpu.VMEM((LANES, s), F32), pltpu.VMEM((LANES, s), F32), pltpu.SemaphoreType.DMA((2,))]
        + [stage] * 7 + [pltpu.VMEM((2, t, t), BF16), pltpu.VMEM((2, t, LANES), F32),
                         pltpu.VMEM((2, t, LANES), BF16), pltpu.VMEM((2, t, LANES), BF16),
                         pltpu.VMEM((2, LANES, t), BF16), pltpu.VMEM((2, LANES, t), BF16),
                         pltpu.VMEM((2, t, t), BF16)],
        compiler_params=_cparams(2),
    )(q, do, tab, k, v)


def _pre_bwd(dres, dhg, x, du, dq, dk, dv, dcb, dcv, c3, n1, w_in, conv_w):
    s = x.shape[0]
    tb = _token_tile(s)
    nt = s // tb

    def body(dres_ref, dhg_ref, x_ref, du_ref, dq_ref, dk_ref, dv_ref, dcb_ref, dcv_ref, dnext_ref, c_ref, ch_ref,
             n1_ref, w_ref, cw_ref, dx_ref, dp_ref, dn1_ref, dcw_ref):
        i = pl.program_id(0)

        @pl.when(i == 0)
        def _():
            dn1_ref[...] = jnp.zeros_like(dn1_ref)
            dcw_ref[...] = jnp.zeros_like(dcw_ref)

        c3 = c_ref[...]
        cc, cx = c3[:, 256:512], c3[:, 512:768]
        ch = ch_ref[...]
        z = cc * cx
        zh = jnp.where(i > 0, ch[:, 256:512] * ch[:, 512:768], 0.0)
        rowi = lax.broadcasted_iota(jnp.int32, (tb, D_CONV), 0)
        z1, z2 = _conv_taps(z, zh, rowi)
        dcv = dcv_ref[...]
        nxt = jnp.where(i < nt - 1, dnext_ref[...], 0.0)
        d1 = jnp.where(rowi == tb - 1, nxt[0:1, :], pltpu.roll(dcv, tb - 1, 0))
        d2 = jnp.where(rowi == tb - 2, nxt[0:1, :], jnp.where(rowi == tb - 1, nxt[1:2, :], pltpu.roll(dcv, tb - 2, 0)))
        cw = cw_ref[...]
        dz = cw[2:3, :] * dcv + cw[1:2, :] * d1 + cw[0:1, :] * d2
        dcw_ref[0:1, :] += jnp.sum(dcv * z2, axis=0, keepdims=True)
        dcw_ref[1:2, :] += jnp.sum(dcv * z1, axis=0, keepdims=True)
        dcw_ref[2:3, :] += jnp.sum(dcv * z, axis=0, keepdims=True)
        dp_ref[:, 0:256] = du_ref[...].astype(BF16)
        dp_ref[:, 256:768] = dq_ref[...].astype(BF16)
        dp_ref[:, 768:1280] = dk_ref[...].astype(BF16)
        dp_ref[:, 1280:1792] = dv_ref[...].astype(BF16)
        dp_ref[:, 1792:2048] = dcb_ref[...].astype(BF16)
        dp_ref[:, 2048:2304] = (dz * cx).astype(BF16)
        dp_ref[:, 2304:2560] = (dz * cc).astype(BF16)
        dh = dhg_ref[...] + _dot_nt(dp_ref[...], w_ref[...])
        dx, dn1 = _rms_bwd(dh, x_ref[...], n1_ref[...])
        dx_ref[...] = dres_ref[...] + dx
        dn1_ref[...] += dn1

    def tok(wd):
        return pl.BlockSpec((tb, wd), lambda i: (i, 0))

    halo_prev = pl.BlockSpec((8, 3 * D_CONV), lambda i: (jnp.maximum(i * (tb // 8) - 1, 0), 0))
    halo_next = pl.BlockSpec((8, D_CONV), lambda i: (jnp.minimum((i + 1) * (tb // 8), s // 8 - 1), 0))
    return pl.pallas_call(
        body, name="pre_bwd", grid=(nt,),
        out_shape=[jax.ShapeDtypeStruct((s, D_MODEL), F32), jax.ShapeDtypeStruct((s, D_IN), BF16),
                   jax.ShapeDtypeStruct((1, D_MODEL), F32), jax.ShapeDtypeStruct((8, D_CONV), F32)],
        in_specs=[tok(D_MODEL), tok(D_MODEL), tok(D_MODEL), tok(D_SSM), tok(D_SB), tok(D_SB), tok(D_SB),
                  tok(D_CONV), tok(D_CONV), halo_next, tok(3 * D_CONV), halo_prev,
                  _full((1, D_MODEL)), _full((D_MODEL, D_IN)), _full((8, D_CONV))],
        out_specs=[tok(D_MODEL), tok(D_IN), _full((1, D_MODEL)), _full((8, D_CONV))],
        compiler_params=_cparams(1),
    )(dres, dhg, x, du, dq, dk, dv, dcb, dcv, dcv, c3, c3, n1, w_in, conv_w)


def _padded(n):
    return -(-n // PIECE_ALIGN) * PIECE_ALIGN


def _pack(pieces, total_rows=None):
    flat = []
    for a in pieces:
        a = a.reshape(-1)
        flat.append(jnp.pad(a, (0, _padded(a.size) - a.size)))
    out = jnp.concatenate(flat)
    if total_rows is not None:
        out = jnp.pad(out, (0, total_rows * LANES - out.size))
    return out.reshape(-1, LANES)


def _unpack(flat, shapes, lead=()):
    flat = flat.reshape(lead + (-1,))
    out, off = [], 0
    for shp in shapes:
        n = math.prod(shp)
        out.append(flat[..., off:off + n].reshape(lead + tuple(shp)))
        off += _padded(n)
    return out


def _to_shards(full, axis):
    shp = full.shape
    cut = shp[:axis] + (N_DEV, shp[axis] // N_DEV) + shp[axis + 1:]
    return jnp.moveaxis(full.reshape(cut), axis, 0)


def _from_shards(shards, axis):
    moved = jnp.moveaxis(shards, 0, axis)
    shp = moved.shape
    return moved.reshape(shp[:axis] + (shp[axis] * shp[axis + 1],) + shp[axis + 2:])


def _expand_groups(compact):
    rows = lax.broadcasted_iota(jnp.int32, (D_SSM, D_STATE), 0) // GROUP_CH
    cols = lax.broadcasted_iota(jnp.int32, (D_SSM, D_STATE), 1) // N_STATE
    return jnp.where(rows == cols, jnp.tile(compact, (N_GROUPS, 1)), 0.0)


def _collect_groups(dense):
    rows = lax.broadcasted_iota(jnp.int32, (D_SSM, D_STATE), 0) // GROUP_CH
    cols = lax.broadcasted_iota(jnp.int32, (D_SSM, D_STATE), 1) // N_STATE
    return jnp.where(rows == cols, dense, 0.0).reshape(N_GROUPS, GROUP_CH, D_STATE).sum(axis=0)


def kernel(x, norm_mix_pre, norm_mix_post, w_in, w_gate, b_gate, ssm_a_re, ssm_a_im, ssm_log_dt, ssm_b_re, ssm_b_im, ssm_c_re, ssm_c_im, ssm_d, w_glu_val, w_glu_gate, w_attn_out, conv_w, w_conv_out, w_mix_out, norm_ffn_pre, norm_ffn_post, w_ffn_up, w_ffn_down, loss_target, m_norm_mix_pre, m_norm_mix_post, m_w_in, m_w_gate, m_b_gate, m_ssm_a_re, m_ssm_a_im, m_ssm_log_dt, m_ssm_b_re, m_ssm_b_im, m_ssm_c_re, m_ssm_c_im, m_ssm_d, m_w_glu_val, m_w_glu_gate, m_w_attn_out, m_conv_w, m_w_conv_out, m_w_mix_out, m_norm_ffn_pre, m_norm_ffn_post, m_w_ffn_up, m_w_ffn_down, v_norm_mix_pre, v_norm_mix_post, v_w_in, v_w_gate, v_b_gate, v_ssm_a_re, v_ssm_a_im, v_ssm_log_dt, v_ssm_b_re, v_ssm_b_im, v_ssm_c_re, v_ssm_c_im, v_ssm_d, v_w_glu_val, v_w_glu_gate, v_w_attn_out, v_conv_w, v_w_conv_out, v_w_mix_out, v_norm_ffn_pre, v_norm_ffn_post, v_w_ffn_up, v_w_ffn_down):
    args = dict(locals())
    wts = {n: args[n] for n in WEIGHTS}
    mom = {n: args["m_" + n] for n in WEIGHTS}
    vel = {n: args["v_" + n] for n in WEIGHTS}
    seq = x.shape[1]
    x0 = x.reshape(seq, D_MODEL)
    target = loss_target.reshape(seq, D_MODEL)

    pieces = []
    for n in SHARDED:
        if n == "conv_w":
            hi = wts[n].astype(BF16)
            pieces += [hi, (wts[n] - hi.astype(F32)).astype(BF16)]
        else:
            pieces.append(wts[n].astype(BF16))
    shard_shapes = [p.shape for p in pieces]
    gathered = _unpack(_all_gather(_pack(pieces)), shard_shapes, lead=(N_DEV,))
    full = {}
    it = iter(gathered)
    for n in SHARDED:
        if n == "conv_w":
            full[n] = _from_shards(next(it).astype(F32) + next(it).astype(F32), SHARD_AXIS[n])
        else:
            full[n] = _from_shards(next(it), SHARD_AXIS[n])

    def layer_weights(l):
        return dict(
            n1=norm_mix_pre[l][None], n2=norm_mix_post[l][None], n3=norm_ffn_pre[l][None], n4=norm_ffn_post[l][None],
            w_in=full["w_in"][l], w_gate=full["w_gate"][l], b_gate=b_gate[l][None],
            w_glu_val=full["w_glu_val"][l], w_glu_gate=full["w_glu_gate"][l], w_attn_out=full["w_attn_out"][l],
            conv_w=jnp.pad(full["conv_w"][l], ((0, 5), (0, 0))), w_conv_out=full["w_conv_out"][l],
            w_mix_out=full["w_mix_out"][l], w_ffn_up=full["w_ffn_up"][l], w_ffn_down=full["w_ffn_down"][l],
            lr=ssm_a_re[l].reshape(1, D_STATE), li=ssm_a_im[l].reshape(1, D_STATE),
            ldt=jnp.repeat(ssm_log_dt[l], N_STATE).reshape(1, D_STATE),
            br=jnp.transpose(ssm_b_re[l], (2, 0, 1)).reshape(GROUP_CH, D_STATE),
            bi=jnp.transpose(ssm_b_im[l], (2, 0, 1)).reshape(GROUP_CH, D_STATE),
            cc_re=_expand_groups(jnp.transpose(ssm_c_re[l], (1, 0, 2)).reshape(GROUP_CH, D_STATE)).astype(BF16),
            cc_im=_expand_groups(jnp.transpose(ssm_c_im[l], (1, 0, 2)).reshape(GROUP_CH, D_STATE)).astype(BF16),
            d_skip=ssm_d[l][None],
        )

    saved = []
    xin = x0
    for l in range(DEPTH):
        w = layer_weights(l)
        ab_re, ab_im, bbr, bbi = _ssm_params_fwd(w["lr"], w["li"], w["ldt"], w["br"], w["bi"])
        w.update(ab_re=ab_re, ab_im=ab_im, bb_re=_expand_groups(bbr).astype(BF16), bb_im=_expand_groups(bbi).astype(BF16))
        u, q, k, v, c3 = _pre_fwd(xin, w["n1"], w["w_in"])
        y256, ypre, states = _ssm_fwd(u, ab_re, ab_im, w["bb_re"], w["bb_im"], w["cc_re"], w["cc_im"], w["d_skip"])
        o, tab = _attn_fwd(q, k, v)
        x1, m2, merged = _merge_fwd(xin, y256, o, c3, w)
        x2, f = _ffn_fwd(x1, w["n3"], w["w_ffn_up"], w["w_ffn_down"], w["n4"])
        saved.append(dict(w=w, x0=xin, u=u, q=q, k=k, v=v, c3=c3, y256=y256, ypre=ypre, states=states, o=o, tab=tab, x1=x1,
                          m2=m2, merged=merged, f=f))
        xin = x2

    dxo, loss_part = _loss_head(xin, target)
    loss = lax.psum(loss_part[0, 0], ("x", "y", "c"))

    grads = {n: [None] * DEPTH for n in WEIGHTS}
    for l in reversed(range(DEPTH)):
        sv = saved[l]
        w = sv["w"]
        dx1, r, da, h2, df, dn3, dn4 = _ffn_bwd(dxo, sv["x1"], sv["f"], w["n3"], w["w_ffn_up"], w["w_ffn_down"], w["n4"])
        grads["w_ffn_down"][l] = _tn_matmul(r, df, "grad_w_ffn_down")
        grads["w_ffn_up"][l] = _tn_matmul(h2, da, "grad_w_ffn_up")
        grads["norm_ffn_pre"][l] = dn3[0]
        grads["norm_ffn_post"][l] = dn4[0]
        (dhg, hb, dgz, dvs, dy256, do, dyb, dyc, bcv, dcv, dcb, dm2, dbg, dn2) = _merge_bwd(
            dx1, sv["x0"], sv["m2"], sv["y256"], sv["o"], sv["c3"], w)
        grads["w_mix_out"][l] = _tn_matmul(sv["merged"], dm2, "grad_w_mix_out")
        grads["w_gate"][l] = _tn_matmul(hb, dgz, "grad_w_gate")
        dglu = _tn_matmul(sv["y256"], dvs, "grad_w_glu")
        grads["w_glu_val"][l] = dglu[:, :D_MODEL]
        grads["w_glu_gate"][l] = dglu[:, D_MODEL:]
        grads["w_attn_out"][l] = _tn_matmul(sv["o"], dyb, "grad_w_attn_out")
        grads["w_conv_out"][l] = _tn_matmul(bcv, dyc, "grad_w_conv_out")
        grads["b_gate"][l] = dbg[0]
        grads["norm_mix_post"][l] = dn2[0]
        du, dar, dai, dbbr, dbbi, dccr, dcci, dd = _ssm_bwd(
            sv["u"], sv["ypre"], dy256, sv["states"], w["ab_re"], w["ab_im"], w["bb_re"], w["bb_im"],
            w["cc_re"], w["cc_im"], w["d_skip"])
        glr, gli, gdt, gbr, gbi = _ssm_params_bwd(w["lr"], w["li"], w["ldt"], w["br"], w["bi"], dar, dai,
                                                   _collect_groups(dbbr), _collect_groups(dbbi))
        grads["ssm_a_re"][l] = glr.reshape(N_GROUPS, N_STATE)
        grads["ssm_a_im"][l] = gli.reshape(N_GROUPS, N_STATE)
        grads["ssm_log_dt"][l] = gdt[:, 0]
        grads["ssm_b_re"][l] = jnp.transpose(gbr.reshape(GROUP_CH, N_GROUPS, N_STATE), (1, 2, 0))
        grads["ssm_b_im"][l] = jnp.transpose(gbi.reshape(GROUP_CH, N_GROUPS, N_STATE), (1, 2, 0))
        grads["ssm_c_re"][l] = jnp.transpose(_collect_groups(dccr).reshape(GROUP_CH, N_GROUPS, N_STATE), (1, 0, 2))
        grads["ssm_c_im"][l] = jnp.transpose(_collect_groups(dcci).reshape(GROUP_CH, N_GROUPS, N_STATE), (1, 0, 2))
        grads["ssm_d"][l] = dd[0]
        dq, dk4, dv4 = _attn_bwd(sv["q"], sv["k"], sv["v"], do, sv["tab"])
        dk = jnp.transpose(dk4, (2, 0, 1)).reshape(seq, D_SB)
        dv = jnp.transpose(dv4, (2, 0, 1)).reshape(seq, D_SB)
        dxo, dp, dn1, dcw = _pre_bwd(dx1, dhg, sv["x0"], du, dq, dk, dv, dcb, dcv, sv["c3"], w["n1"], w["w_in"], w["conv_w"])
        grads["w_in"][l] = _tn_matmul(hb, dp, "grad_w_in")
        grads["norm_mix_pre"][l] = dn1[0]
        grads["conv_w"][l] = dcw[0:3]
    grad_x = dxo.reshape(x.shape)
    gfull = {n: jnp.stack(grads[n]) for n in WEIGHTS}

    rep_parts = _pack([gfull[n] for n in REPLICATED])
    sh_parts = jnp.stack([_pack([_to_shards(gfull[n], SHARD_AXIS[n])[d] for n in SHARDED]) for d in range(N_DEV)])
    rows = sh_parts.shape[1] + rep_parts.shape[0]
    total_rows = -(-rows // 512) * 512
    parts = jnp.concatenate([sh_parts, jnp.broadcast_to(rep_parts, (N_DEV,) + rep_parts.shape),
                             jnp.zeros((N_DEV, total_rows - rows, LANES), F32)], axis=1)
    recv = _all_to_all(parts)

    def mine(d):
        sh = _pack([d[n] for n in SHARDED])
        rp = _pack([d[n] for n in REPLICATED])
        return jnp.concatenate([sh, rp, jnp.zeros((total_rows - rows, LANES), F32)], axis=0)

    flat_out = _adamw(recv, mine(wts), mine(mom), mine(vel))
    n_sh_rows = sh_parts.shape[1]
    results = []
    for fo in flat_out:
        sh = _unpack(fo[:n_sh_rows], [wts[n].shape for n in SHARDED])
        rp = _unpack(fo[n_sh_rows:rows], [wts[n].shape for n in REPLICATED])
        by_name = dict(zip(SHARDED, sh))
        by_name.update(zip(REPLICATED, rp))
        results.append([by_name[n] for n in WEIGHTS])
    g_out, d_out, m_out, v_out = results
    return (loss, grad_x, *g_out, *d_out, *m_out, *v_out)
```

```python
import functools
import math

import jax
import jax.numpy as jnp
from jax import lax
from jax.experimental import pallas as pl
from jax.experimental.pallas import tpu as pltpu

F32 = jnp.float32
BF16 = jnp.bfloat16

N_DEV = 8
D_MODEL = 1024
DEPTH = 2
D_SSM = 256
N_GROUPS = 16
GROUP_CH = 16
N_STATE = 64
D_STATE = N_GROUPS * N_STATE
D_SB = 512
HEAD_DIM = 64
D_CONV = 256
D_IN = 2560
D_FF = 4096
EPS = 1e-6
Q_SCALE = HEAD_DIM ** -0.5
LOG2E = math.log2(math.e)
DEAD_LOG2 = -160.0

ADAM_LR = 0.001
ADAM_B1 = 0.9
ADAM_B2 = 0.999
ADAM_EPS = 1e-08
ADAM_WD = 0.01
ADAM_STEP = 10

LANES = 128
PIECE_ALIGN = 2048
VMEM_LIMIT = 56 * 1024 * 1024

WEIGHTS = ['norm_mix_pre', 'norm_mix_post', 'w_in', 'w_gate', 'b_gate', 'ssm_a_re', 'ssm_a_im', 'ssm_log_dt',
           'ssm_b_re', 'ssm_b_im', 'ssm_c_re', 'ssm_c_im', 'ssm_d', 'w_glu_val', 'w_glu_gate', 'w_attn_out',
           'conv_w', 'w_conv_out', 'w_mix_out', 'norm_ffn_pre', 'norm_ffn_post', 'w_ffn_up', 'w_ffn_down']
SHARD_AXIS = {'w_in': 2, 'w_gate': 2, 'w_glu_val': 2, 'w_glu_gate': 2, 'w_attn_out': 2, 'conv_w': 2,
              'w_conv_out': 2, 'w_mix_out': 1, 'w_ffn_up': 2, 'w_ffn_down': 1}
SHARDED = [n for n in WEIGHTS if n in SHARD_AXIS]
REPLICATED = [n for n in WEIGHTS if n not in SHARD_AXIS]


def _cparams(n_grid):
    return pltpu.CompilerParams(dimension_semantics=("arbitrary",) * n_grid, vmem_limit_bytes=VMEM_LIMIT)


def _dot(a, b):
    return jnp.dot(a.astype(BF16), b.astype(BF16), preferred_element_type=F32)


def _dot_nt(a, b):
    return lax.dot_general(a.astype(BF16), b.astype(BF16), (((1,), (1,)), ((), ())), preferred_element_type=F32)


def _dot_tn(a, b):
    return lax.dot_general(a.astype(BF16), b.astype(BF16), (((0,), (0,)), ((), ())), preferred_element_type=F32)


def _split_dot(a, b):
    hi = a.astype(BF16)
    lo = (a - hi.astype(F32)).astype(BF16)
    return jnp.dot(hi, b, preferred_element_type=F32) + jnp.dot(lo, b, preferred_element_type=F32)


def _rms_fwd(x, g):
    r = lax.rsqrt(jnp.mean(x * x, axis=-1, keepdims=True) + EPS)
    return x * r * g


def _rms_bwd(dy, x, g):
    r = lax.rsqrt(jnp.mean(x * x, axis=-1, keepdims=True) + EPS)
    dyg = dy * g
    dx = r * dyg - x * (r * r * r) * jnp.mean(dyg * x, axis=-1, keepdims=True)
    dg = jnp.sum(dy * x * r, axis=0, keepdims=True)
    return dx, dg


def _sigmoid(x):
    return 1.0 / (1.0 + jnp.exp(-x))


_GELU_C = math.sqrt(2.0 / math.pi)


def _gelu(y):
    return 0.5 * y * (1.0 + jnp.tanh(_GELU_C * (y + 0.044715 * y * y * y)))


def _gelu_grad(y):
    t = jnp.tanh(_GELU_C * (y + 0.044715 * y * y * y))
    return 0.5 * (1.0 + t) + 0.5 * y * (1.0 - t * t) * _GELU_C * (1.0 + 3.0 * 0.044715 * y * y)


def _full(shape):
    return pl.BlockSpec(shape, lambda *_: (0,) * len(shape))


N_CHIPS = 4


def _chip_peer(x, y, k):
    return (1 - x if (k >> 1) & 1 else x), (1 - y if k & 1 else y)


def _over_ici(src, to_slot, name):
    rows, lanes = src.shape[-2:]

    def body(src_ref, out_ref, send_sems, recv_sems, local_sem):
        x, y, c = lax.axis_index("x"), lax.axis_index("y"), lax.axis_index("c")
        chip = 2 * x + y
        mine = pltpu.make_async_copy(src_ref.at[chip] if to_slot else src_ref, out_ref.at[chip], local_sem)
        mine.start()
        copies = []
        for k in range(1, N_CHIPS):
            px, py = _chip_peer(x, y, k)
            cp = pltpu.make_async_remote_copy(
                src_ref=src_ref.at[2 * px + py] if to_slot else src_ref, dst_ref=out_ref.at[chip],
                send_sem=send_sems.at[k - 1], recv_sem=recv_sems.at[k - 1],
                device_id=(px, py, c), device_id_type=pl.DeviceIdType.MESH)
            cp.start()
            copies.append(cp)
        for cp in copies:
            cp.wait()
        mine.wait()

    return pl.pallas_call(
        body, name=name,
        out_shape=jax.ShapeDtypeStruct((N_CHIPS, rows, lanes), src.dtype),
        in_specs=[pl.BlockSpec(memory_space=pl.ANY)],
        out_specs=pl.BlockSpec(memory_space=pl.ANY),
        scratch_shapes=[pltpu.SemaphoreType.DMA((N_CHIPS - 1,)), pltpu.SemaphoreType.DMA((N_CHIPS - 1,)),
                        pltpu.SemaphoreType.DMA],
    )(src)


def _between_cores(src, mine_too, name):
    rows, lanes = src.shape[-2:]

    def body(src_ref, out_ref, send_sem, recv_sem, local_sem):
        x, y, c = lax.axis_index("x"), lax.axis_index("y"), lax.axis_index("c")
        if mine_too:
            keep, give, land = src_ref.at[:, c], src_ref.at[:, 1 - c], out_ref.at[:, c]
        else:
            keep, give, land = src_ref, src_ref, out_ref.at[:, c]
        mine = pltpu.make_async_copy(keep, land, local_sem)
        mine.start()
        cp = pltpu.make_async_remote_copy(
            src_ref=give, dst_ref=land, send_sem=send_sem, recv_sem=recv_sem,
            device_id=(x, y, 1 - c), device_id_type=pl.DeviceIdType.MESH)
        cp.start()
        cp.wait()
        mine.wait()

    return pl.pallas_call(
        body, name=name,
        out_shape=jax.ShapeDtypeStruct((N_CHIPS, 2, rows, lanes), src.dtype),
        in_specs=[pl.BlockSpec(memory_space=pl.ANY)],
        out_specs=pl.BlockSpec(memory_space=pl.ANY),
        scratch_shapes=[pltpu.SemaphoreType.DMA, pltpu.SemaphoreType.DMA, pltpu.SemaphoreType.DMA],
    )(src)


def _pair_sum(both):
    n, _, rows, lanes = both.shape
    tr = 512
    assert rows % tr == 0

    def body(b_ref, o_ref):
        o_ref[...] = b_ref[:, 0] + b_ref[:, 1]

    return pl.pallas_call(
        body, name="grads_pair_sum", grid=(rows // tr,),
        out_shape=jax.ShapeDtypeStruct((n, rows, lanes), both.dtype),
        in_specs=[pl.BlockSpec((n, 2, tr, lanes), lambda i: (0, 0, i, 0))],
        out_specs=pl.BlockSpec((n, tr, lanes), lambda i: (0, i, 0)),
        compiler_params=_cparams(1),
    )(both)


def _all_gather(shard):
    chips = _over_ici(shard, False, "weights_gather_chips")
    both = _between_cores(chips, False, "weights_gather_cores")
    return both.reshape((N_DEV,) + shard.shape)


def _reduce_scatter(parts):
    both = _between_cores(parts.reshape((N_CHIPS, 2) + parts.shape[1:]), True, "grads_swap_cores")
    return _over_ici(_pair_sum(both), True, "grads_scatter_chips")


def _adamw(recv, w, m, v):
    rows = w.shape[0]
    n_src = recv.shape[0]
    tr = 512
    assert rows % tr == 0
    c1 = 1.0 / (1.0 - ADAM_B1 ** ADAM_STEP)
    c2 = 1.0 / (1.0 - ADAM_B2 ** ADAM_STEP)

    def body(r_ref, w_ref, m_ref, v_ref, g_ref, d_ref, mo_ref, vo_ref):
        g = r_ref[0]
        for s in range(1, n_src):
            g = g + r_ref[s]
        mn = ADAM_B1 * m_ref[...] + (1.0 - ADAM_B1) * g
        vn = ADAM_B2 * v_ref[...] + (1.0 - ADAM_B2) * (g * g)
        upd = (mn * c1) / (jnp.sqrt(vn * c2) + ADAM_EPS) + ADAM_WD * w_ref[...]
        g_ref[...] = g
        d_ref[...] = -ADAM_LR * upd
        mo_ref[...] = mn
        vo_ref[...] = vn

    spec = pl.BlockSpec((tr, LANES), lambda i: (i, 0))
    return pl.pallas_call(
        body, name="adamw", grid=(rows // tr,),
        out_shape=[jax.ShapeDtypeStruct((rows, LANES), F32)] * 4,
        in_specs=[pl.BlockSpec((n_src, tr, LANES), lambda i: (0, i, 0)), spec, spec, spec],
        out_specs=[spec] * 4,
        compiler_params=_cparams(1),
    )(recv, w, m, v)


def _tn_matmul(a, b, name):
    s, m = a.shape
    n = b.shape[1]
    tk = min(1024, s)
    tm = min(1024, m)
    tn = max(c for c in range(LANES, 1280 + 1, LANES) if n % c == 0)
    assert m % tm == 0 and s % tk == 0
    nk = s // tk

    def body(a_ref, b_ref, o_ref):
        k = pl.program_id(2)

        @pl.when(k == 0)
        def _():
            o_ref[...] = jnp.zeros_like(o_ref)

        o_ref[...] += _dot_tn(a_ref[...], b_ref[...])

    return pl.pallas_call(
        body, name=name, grid=(m // tm, n // tn, nk),
        out_shape=jax.ShapeDtypeStruct((m, n), F32),
        in_specs=[pl.BlockSpec((tk, tm), lambda i, j, k: (k, i)), pl.BlockSpec((tk, tn), lambda i, j, k: (k, j))],
        out_specs=pl.BlockSpec((tm, tn), lambda i, j, k: (i, j)),
        compiler_params=_cparams(3),
    )(a, b)


def _token_tile(s):
    return min(256, s)


def _pre_fwd(x, n1, w_in):
    s = x.shape[0]
    tb = _token_tile(s)

    def body(x_ref, n_ref, w_ref, u_ref, q_ref, k_ref, v_ref, c_ref):
        h = _rms_fwd(x_ref[...], n_ref[...])
        p = _dot(h, w_ref[...])
        u_ref[...] = p[:, 0:256]
        q_ref[...] = (p[:, 256:768] * (Q_SCALE * LOG2E)).astype(BF16)
        k_ref[...] = p[:, 768:1280].astype(BF16)
        v_ref[...] = p[:, 1280:1792].astype(BF16)
        c_ref[...] = p[:, 1792:2560]

    def tok(w):
        return pl.BlockSpec((tb, w), lambda i: (i, 0))

    return pl.pallas_call(
        body, name="pre_fwd", grid=(s // tb,),
        out_shape=[jax.ShapeDtypeStruct((s, D_SSM), F32), jax.ShapeDtypeStruct((s, D_SB), BF16),
                   jax.ShapeDtypeStruct((s, D_SB), BF16), jax.ShapeDtypeStruct((s, D_SB), BF16),
                   jax.ShapeDtypeStruct((s, 3 * D_CONV), F32)],
        in_specs=[tok(D_MODEL), _full((1, D_MODEL)), _full((D_MODEL, D_IN))],
        out_specs=[tok(D_SSM), tok(D_SB), tok(D_SB), tok(D_SB), tok(3 * D_CONV)],
        compiler_params=_cparams(1),
    )(x, n1, w_in)


def _ssm_discretize(lr, li, ldt, br, bi):
    dt = jnp.exp(ldt)
    mag = jnp.exp(lr * dt)
    ab_re = mag * jnp.cos(li * dt)
    ab_im = mag * jnp.sin(li * dt)
    den = lr * lr + li * li
    xr = ab_re - 1.0
    coef_re = (xr * lr + ab_im * li) / den
    coef_im = (ab_im * lr - xr * li) / den
    bb_re = coef_re * br - coef_im * bi
    bb_im = coef_re * bi + coef_im * br
    return ab_re, ab_im, bb_re, bb_im


def _ssm_params_fwd(lr, li, ldt, br, bi):
    def body(lr_ref, li_ref, ldt_ref, br_ref, bi_ref, ar_ref, ai_ref, bbr_ref, bbi_ref):
        ar, ai, bbr, bbi = _ssm_discretize(lr_ref[...], li_ref[...], ldt_ref[...], br_ref[...], bi_ref[...])
        ar_ref[...] = ar
        ai_ref[...] = ai
        bbr_ref[...] = bbr
        bbi_ref[...] = bbi

    row = jax.ShapeDtypeStruct((1, D_STATE), F32)
    mat = jax.ShapeDtypeStruct((GROUP_CH, D_STATE), F32)
    return pl.pallas_call(body, name="ssm_params_fwd", out_shape=[row, row, mat, mat])(lr, li, ldt, br, bi)


def _ssm_params_bwd(lr, li, ldt, br, bi, d_ar, d_ai, d_bbr, d_bbi):
    def body(lr_ref, li_ref, ldt_ref, br_ref, bi_ref, dar_ref, dai_ref, dbbr_ref, dbbi_ref,
             glr_ref, gli_ref, gdt_ref, gbr_ref, gbi_ref):
        _, vjp = jax.vjp(_ssm_discretize, lr_ref[...], li_ref[...], ldt_ref[...], br_ref[...], bi_ref[...])
        glr, gli, gdt, gbr, gbi = vjp((dar_ref[...], dai_ref[...], dbbr_ref[...], dbbi_ref[...]))
        glr_ref[...] = glr
        gli_ref[...] = gli
        gbr_ref[...] = gbr
        gbi_ref[...] = gbi
        grp = lax.broadcasted_iota(jnp.int32, (N_GROUPS, D_STATE), 0)
        col = lax.broadcasted_iota(jnp.int32, (N_GROUPS, D_STATE), 1)
        own = (col // N_STATE) == grp
        per_group = jnp.sum(jnp.where(own, jnp.broadcast_to(gdt, (N_GROUPS, D_STATE)), 0.0), axis=1, keepdims=True)
        gdt_ref[...] = jnp.broadcast_to(per_group, (N_GROUPS, LANES))

    row = jax.ShapeDtypeStruct((1, D_STATE), F32)
    mat = jax.ShapeDtypeStruct((GROUP_CH, D_STATE), F32)
    return pl.pallas_call(
        body, name="ssm_params_bwd",
        out_shape=[row, row, jax.ShapeDtypeStruct((N_GROUPS, LANES), F32), mat, mat],
    )(lr, li, ldt, br, bi, d_ar, d_ai, d_bbr, d_bbi)


def _ssm_fwd(u, ab_re, ab_im, bb_re, bb_im, cc_re, cc_im, d_skip):
    s = u.shape[0]
    ts = _token_tile(s)
    nt = s // ts

    def body(u_ref, ar_ref, ai_ref, bbr_ref, bbi_ref, ccr_ref, cci_ref, d_ref, y_ref, ypre_ref, st_ref,
             bur, bui, hr_s, hi_s, cr, ci):
        @pl.when(pl.program_id(0) == 0)
        def _():
            cr[...] = jnp.zeros_like(cr)
            ci[...] = jnp.zeros_like(ci)

        u = u_ref[...]
        bur[...] = _dot(u, bbr_ref[...])
        bui[...] = _dot(u, bbi_ref[...])
        st_ref[0, 0:1, :] = cr[...]
        st_ref[0, 1:2, :] = ci[...]
        ar = ar_ref[...]
        ai = ai_ref[...]

        def step(t, carry):
            hr, hi = carry
            nr = ar * hr - ai * hi + bur[pl.ds(t, 1), :]
            ni = ar * hi + ai * hr + bui[pl.ds(t, 1), :]
            hr_s[pl.ds(t, 1), :] = nr
            hi_s[pl.ds(t, 1), :] = ni
            return nr, ni

        hr, hi = lax.fori_loop(0, ts, step, (cr[...], ci[...]), unroll=8)
        cr[...] = hr
        ci[...] = hi
        y = _dot_nt(hr_s[...], ccr_ref[...]) - _dot_nt(hi_s[...], cci_ref[...]) + d_ref[...] * u
        ypre_ref[...] = y
        y_ref[...] = _gelu(y).astype(BF16)

    tok = pl.BlockSpec((ts, D_SSM), lambda i: (i, 0))
    row = _full((1, D_STATE))
    mat = _full((D_SSM, D_STATE))
    return pl.pallas_call(
        body, name="ssm_fwd", grid=(nt,),
        out_shape=[jax.ShapeDtypeStruct((s, D_SSM), BF16), jax.ShapeDtypeStruct((s, D_SSM), F32),
                   jax.ShapeDtypeStruct((nt, 2, D_STATE), F32)],
        in_specs=[tok, row, row, mat, mat, mat, mat, _full((1, D_SSM))],
        out_specs=[tok, tok, pl.BlockSpec((1, 2, D_STATE), lambda i: (i, 0, 0))],
        scratch_shapes=[pltpu.VMEM((ts, D_STATE), F32)] * 4 + [pltpu.VMEM((1, D_STATE), F32)] * 2,
        compiler_params=_cparams(1),
    )(u, ab_re, ab_im, bb_re, bb_im, cc_re, cc_im, d_skip)


def _attn_tile(s):
    return min(256, s)


def _sb_block(z, valid):
    lp = jnp.log2(1.0 + jnp.exp2(-jnp.abs(z)))
    ls = jnp.minimum(z, 0.0) - lp
    l1m = ls - z
    if valid is not None:
        l1m = jnp.where(valid, l1m, 0.0)
    return ls, l1m


def _attn_fwd(q, k, v):
    s = q.shape[0]
    t = _attn_tile(s)
    nq = s // t
    assert nq <= LANES

    def body(q_ref, k_ref, v_ref, o_ref, tab_ref, z_s, ls_s, lg_s, q_s, tri_s, acc_s):
        i = pl.program_id(1)
        lane = lax.broadcasted_iota(jnp.int32, (t, LANES), 1)
        first = lane < HEAD_DIM
        row = lax.broadcasted_iota(jnp.int32, (t, t), 0)
        col = lax.broadcasted_iota(jnp.int32, (t, t), 1)
        tri_s[...] = (row > col).astype(BF16)
        causal = col < row
        q2 = q_ref[...]
        zero = jnp.zeros_like(q2)
        q_s[0] = jnp.where(first, q2, zero)
        q_s[1] = jnp.where(first, zero, q2)
        acc_s[...] = jnp.zeros_like(acc_s)
        tab_ref[...] = jnp.full_like(tab_ref, -1e30)

        def scores(j):
            k2 = k_ref[pl.ds(pl.multiple_of(jnp.maximum(j, 0) * t, t), t), :]
            for a in range(2):
                z_s[a] = _dot_nt(q_s[a], k2)

        def logits(j, carries, valid):
            out = []
            for a in range(2):
                c1 = carries[a]
                ls, l1m = _sb_block(z_s[a], valid)
                ls_s[a] = ls + c1
                suffix = _dot(l1m, tri_s[...])
                lg_s[a] = ls_s[a] + suffix
                cols = slice(a * LANES, (a + 1) * LANES)
                tab_ref[:, cols] = jnp.where(lane == j, c1, tab_ref[:, cols])
                out.append(c1 + suffix[:, 0:1] + l1m[:, 0:1])
            return tuple(out)

        def accumulate(j, valid):
            v2 = v_ref[pl.ds(pl.multiple_of(jnp.maximum(j, 0) * t, t), t), :]
            for a in range(2):
                w = jnp.exp2(lg_s[a])
                if valid is not None:
                    w = jnp.where(valid, w, 0.0)
                acc_s[a] += _dot(w, v2)

        carries = tuple(jnp.zeros((t, 1), F32) for _ in range(2))
        scores(i)
        carries = logits(i, carries, causal)
        scores(i - 1)
        accumulate(i, causal)
        carries = logits(i - 1, carries, None)
        scores(i - 2)

        def trip(n, carries):
            accumulate(i - n + 2, None)
            carries = logits(i - n + 1, carries, None)
            scores(i - n)
            return carries

        def alive(c):
            n, carries = c
            return (n <= i + 1) & (jnp.max(jnp.maximum(carries[0], carries[1])) >= DEAD_LOG2)

        n_end, _ = lax.while_loop(alive, lambda c: (c[0] + 1, trip(c[0], c[1])), (jnp.int32(3), carries))

        @pl.when(i - n_end + 2 >= 0)
        def _():
            accumulate(i - n_end + 2, None)

        o_ref[...] = jnp.where(first, acc_s[0], acc_s[1]).astype(BF16)

    return pl.pallas_call(
        body, name="attn_fwd", grid=(D_SB // LANES, nq),
        out_shape=[jax.ShapeDtypeStruct((s, D_SB), BF16), jax.ShapeDtypeStruct((s, 2 * D_SB), F32)],
        in_specs=[pl.BlockSpec((t, LANES), lambda p, i: (i, p)), pl.BlockSpec((s, LANES), lambda p, i: (0, p)),
                  pl.BlockSpec((s, LANES), lambda p, i: (0, p))],
        out_specs=[pl.BlockSpec((t, LANES), lambda p, i: (i, p)), pl.BlockSpec((t, 2 * LANES), lambda p, i: (i, p))],
        scratch_shapes=[pltpu.VMEM((2, t, t), F32)] * 3 + [pltpu.VMEM((2, t, LANES), BF16), pltpu.VMEM((t, t), BF16),
                                                           pltpu.VMEM((2, t, LANES), F32)],
        compiler_params=_cparams(2),
    )(q, k, v)


def _conv_taps(z, halo, rowi):
    z1 = jnp.where(rowi == 0, halo[7:8, :], pltpu.roll(z, 1, 0))
    z2 = jnp.where(rowi == 0, halo[6:7, :], jnp.where(rowi == 1, halo[7:8, :], pltpu.roll(z, 2, 0)))
    return z1, z2


def _merge_branches(x_ref, y_ref, o_ref, c_ref, ch_ref, n1_ref, wg_ref, bg_ref, wv_ref, wsg_ref, wao_ref,
                    cw_ref, wc_ref, tb):
    i = pl.program_id(0)
    h = _rms_fwd(x_ref[...], n1_ref[...])
    g = _sigmoid(_dot(h, wg_ref[...]) + bg_ref[...])
    y256 = y_ref[...]
    val = _dot(y256, wv_ref[...])
    sg = _sigmoid(_dot(y256, wsg_ref[...]))
    ysb = _dot(o_ref[...], wao_ref[...])
    c3 = c_ref[...]
    cb, cc, cx = c3[:, 0:256], c3[:, 256:512], c3[:, 512:768]
    ch = ch_ref[...]
    z = cc * cx
    zh = jnp.where(i > 0, ch[:, 256:512] * ch[:, 512:768], 0.0)
    rowi = lax.broadcasted_iota(jnp.int32, (tb, D_CONV), 0)
    z1, z2 = _conv_taps(z, zh, rowi)
    cw = cw_ref[...]
    cv = cw[0:1, :] * z2 + cw[1:2, :] * z1 + cw[2:3, :] * z
    yc = _dot(cb * cv, wc_ref[...])
    return dict(h=h, g=g, val=val, sg=sg, ysb=ysb, cb=cb, cv=cv, yc=yc)


def _merge_fwd(x, y256, o, c3, w):
    s = x.shape[0]
    tb = _token_tile(s)

    def body(x_ref, y_ref, o_ref, c_ref, ch_ref, n1_ref, wg_ref, bg_ref, wv_ref, wsg_ref, wao_ref, cw_ref, wc_ref,
             wm_ref, n2_ref, x1_ref, m2_ref, mg_ref):
        br = _merge_branches(x_ref, y_ref, o_ref, c_ref, ch_ref, n1_ref, wg_ref, bg_ref, wv_ref, wsg_ref, wao_ref,
                             cw_ref, wc_ref, tb)
        g = br["g"]
        merged = (g[:, 0:1024] * (br["val"] * br["sg"]) + g[:, 1024:2048] * br["ysb"] + g[:, 2048:3072] * br["yc"])
        mg_ref[...] = merged.astype(BF16)
        m2 = _dot(merged, wm_ref[...])
        m2_ref[...] = m2
        x1_ref[...] = x_ref[...] + _rms_fwd(m2, n2_ref[...])

    def tok(wd):
        return pl.BlockSpec((tb, wd), lambda i: (i, 0))

    halo = pl.BlockSpec((8, 3 * D_CONV), lambda i: (jnp.maximum(i * (tb // 8) - 1, 0), 0))
    return pl.pallas_call(
        body, name="merge_fwd", grid=(s // tb,),
        out_shape=[jax.ShapeDtypeStruct((s, D_MODEL), F32), jax.ShapeDtypeStruct((s, D_MODEL), F32),
                   jax.ShapeDtypeStruct((s, D_MODEL), BF16)],
        in_specs=[tok(D_MODEL), tok(D_SSM), tok(D_SB), tok(3 * D_CONV), halo,
                  _full((1, D_MODEL)), _full((D_MODEL, 3 * D_MODEL)), _full((1, 3 * D_MODEL)),
                  _full((D_SSM, D_MODEL)), _full((D_SSM, D_MODEL)), _full((D_SB, D_MODEL)),
                  _full((8, D_CONV)), _full((D_CONV, D_MODEL)), _full((D_MODEL, D_MODEL)), _full((1, D_MODEL))],
        out_specs=[tok(D_MODEL), tok(D_MODEL), tok(D_MODEL)],
        compiler_params=_cparams(1),
    )(x, y256, o, c3, c3, w["n1"], w["w_gate"], w["b_gate"], w["w_glu_val"], w["w_glu_gate"], w["w_attn_out"],
      w["conv_w"], w["w_conv_out"], w["w_mix_out"], w["n2"])


FF_CHUNK = 1024


def _ffn_fwd(x1, n3, w_up, w_dn, n4):
    s = x1.shape[0]
    tb = min(512, s)
    nh = D_FF // FF_CHUNK

    def body(x_ref, n3_ref, wu_ref, wd_ref, n4_ref, x2_ref, f_ref, h_s, acc):
        j = pl.program_id(1)

        @pl.when(j == 0)
        def _():
            h_s[...] = _rms_fwd(x_ref[...], n3_ref[...]).astype(BF16)
            acc[...] = jnp.zeros_like(acc)

        a = jnp.maximum(_dot(h_s[...], wu_ref[...]), 0.0)
        acc[...] += _dot(a * a, wd_ref[...])

        @pl.when(j == nh - 1)
        def _():
            f = acc[...]
            f_ref[...] = f
            x2_ref[...] = x_ref[...] + _rms_fwd(f, n4_ref[...])

    tok = pl.BlockSpec((tb, D_MODEL), lambda i, j: (i, 0))
    return pl.pallas_call(
        body, name="ffn_fwd", grid=(s // tb, nh),
        out_shape=[jax.ShapeDtypeStruct((s, D_MODEL), F32)] * 2,
        in_specs=[tok, _full((1, D_MODEL)), pl.BlockSpec((D_MODEL, FF_CHUNK), lambda i, j: (0, j)),
                  pl.BlockSpec((FF_CHUNK, D_MODEL), lambda i, j: (j, 0)), _full((1, D_MODEL))],
        out_specs=[tok, tok],
        scratch_shapes=[pltpu.VMEM((tb, D_MODEL), BF16), pltpu.VMEM((tb, D_MODEL), F32)],
        compiler_params=_cparams(2),
    )(x1, n3, w_up, w_dn, n4)


def _loss_head(y, target):
    s = y.shape[0]
    tb = _token_tile(s)

    def body(y_ref, t_ref, dy_ref, l_ref):
        @pl.when(pl.program_id(0) == 0)
        def _():
            l_ref[...] = jnp.zeros_like(l_ref)

        err = y_ref[...] - t_ref[...]
        dy_ref[...] = err * (1.0 / D_MODEL)
        l_ref[...] += 0.5 * jnp.sum(jnp.mean(err * err, axis=-1, keepdims=True), axis=0, keepdims=True)

    tok = pl.BlockSpec((tb, D_MODEL), lambda i: (i, 0))
    return pl.pallas_call(
        body, name="loss_head", grid=(s // tb,),
        out_shape=[jax.ShapeDtypeStruct((s, D_MODEL), F32), jax.ShapeDtypeStruct((8, LANES), F32)],
        in_specs=[tok, tok], out_specs=[tok, _full((8, LANES))],
        compiler_params=_cparams(1),
    )(y, target)


def _ffn_bwd(dx2, x1, f, n3, w_up, w_dn, n4):
    s = x1.shape[0]
    tb = min(512, s)
    nh = D_FF // FF_CHUNK
    nt = s // tb

    def body(dx2_ref, x_ref, f_ref, n3_ref, wu_ref, wd_ref, n4_ref,
             dx1_ref, r_ref, da_ref, h_ref, df_ref, dn3_ref, dn4_ref, acc):
        i = pl.program_id(0)
        j = pl.program_id(1)

        @pl.when((i == 0) & (j == 0))
        def _():
            dn3_ref[...] = jnp.zeros_like(dn3_ref)
            dn4_ref[...] = jnp.zeros_like(dn4_ref)

        @pl.when(j == 0)
        def _():
            h_ref[...] = _rms_fwd(x_ref[...], n3_ref[...]).astype(BF16)
            df, dn4 = _rms_bwd(dx2_ref[...], f_ref[...], n4_ref[...])
            df_ref[...] = df.astype(BF16)
            dn4_ref[...] += dn4
            acc[...] = jnp.zeros_like(acc)

        a = jnp.maximum(_dot(h_ref[...], wu_ref[...]), 0.0)
        r_ref[...] = (a * a).astype(BF16)
        da = (_dot_nt(df_ref[...], wd_ref[...]) * (2.0 * a)).astype(BF16)
        da_ref[...] = da
        acc[...] += _dot_nt(da, wu_ref[...])

        @pl.when(j == nh - 1)
        def _():
            dx, dn3 = _rms_bwd(acc[...], x_ref[...], n3_ref[...])
            dx1_ref[...] = dx2_ref[...] + dx
            dn3_ref[...] += dn3

    tok = pl.BlockSpec((tb, D_MODEL), lambda i, j: (i, 0))
    hid = pl.BlockSpec((tb, FF_CHUNK), lambda i, j: (i, j))
    tok_b = jax.ShapeDtypeStruct((s, D_MODEL), BF16)
    hid_b = jax.ShapeDtypeStruct((s, D_FF), BF16)
    row = jax.ShapeDtypeStruct((1, D_MODEL), F32)
    return pl.pallas_call(
        body, name="ffn_bwd", grid=(nt, nh),
        out_shape=[jax.ShapeDtypeStruct((s, D_MODEL), F32), hid_b, hid_b, tok_b, tok_b, row, row],
        in_specs=[tok, tok, tok, _full((1, D_MODEL)), pl.BlockSpec((D_MODEL, FF_CHUNK), lambda i, j: (0, j)),
                  pl.BlockSpec((FF_CHUNK, D_MODEL), lambda i, j: (j, 0)), _full((1, D_MODEL))],
        out_specs=[tok, hid, hid, tok, tok, _full((1, D_MODEL)), _full((1, D_MODEL))],
        scratch_shapes=[pltpu.VMEM((tb, D_MODEL), F32)],
        compiler_params=_cparams(2),
    )(dx2, x1, f, n3, w_up, w_dn, n4)


def _merge_bwd(dx1, x, m2, y256, o, c3, w):
    s = x.shape[0]
    tb = _token_tile(s)

    def body(dx1_ref, m2_ref, x_ref, y_ref, o_ref, c_ref, ch_ref, n1_ref, wg_ref, bg_ref, wv_ref, wsg_ref, wao_ref,
             cw_ref, wc_ref, wm_ref, n2_ref,
             dhg_ref, h_ref, dgz_ref, dvs_ref, dy_ref, do_ref, dyb_ref, dyc_ref, bcv_ref, dcv_ref, dcb_ref, dm2_ref,
             dbg_ref, dn2_ref):
        @pl.when(pl.program_id(0) == 0)
        def _():
            dbg_ref[...] = jnp.zeros_like(dbg_ref)
            dn2_ref[...] = jnp.zeros_like(dn2_ref)

        dm2, dn2 = _rms_bwd(dx1_ref[...], m2_ref[...], n2_ref[...])
        dn2_ref[...] += dn2
        dm2_ref[...] = dm2.astype(BF16)
        dmg = _dot_nt(dm2, wm_ref[...])
        br = _merge_branches(x_ref, y_ref, o_ref, c_ref, ch_ref, n1_ref, wg_ref, bg_ref, wv_ref, wsg_ref, wao_ref,
                             cw_ref, wc_ref, tb)
        g = br["g"]
        g1, g2, g3 = g[:, 0:1024], g[:, 1024:2048], g[:, 2048:3072]
        val, sg = br["val"], br["sg"]
        h_ref[...] = br["h"].astype(BF16)
        dgz = jnp.concatenate([dmg * (val * sg) * g1 * (1.0 - g1), dmg * br["ysb"] * g2 * (1.0 - g2),
                               dmg * br["yc"] * g3 * (1.0 - g3)], axis=1)
        dbg_ref[...] += jnp.sum(dgz, axis=0, keepdims=True)
        dgz_ref[...] = dgz.astype(BF16)
        dhg_ref[...] = _dot_nt(dgz, wg_ref[...])
        dys = dmg * g1
        dval = dys * sg
        dsg = dys * val * sg * (1.0 - sg)
        dvs_ref[:, 0:1024] = dval.astype(BF16)
        dvs_ref[:, 1024:2048] = dsg.astype(BF16)
        dy_ref[...] = _dot_nt(dval, wv_ref[...]) + _dot_nt(dsg, wsg_ref[...])
        dyb = (dmg * g2).astype(BF16)
        dyb_ref[...] = dyb
        do_ref[...] = _dot_nt(dyb, wao_ref[...]).astype(BF16)
        dyc = (dmg * g3).astype(BF16)
        dyc_ref[...] = dyc
        dcq = _dot_nt(dyc, wc_ref[...])
        bcv_ref[...] = (br["cb"] * br["cv"]).astype(BF16)
        dcb_ref[...] = dcq * br["cv"]
        dcv_ref[...] = dcq * br["cb"]

    def tok(wd):
        return pl.BlockSpec((tb, wd), lambda i: (i, 0))

    def out(wd, dt):
        return jax.ShapeDtypeStruct((s, wd), dt)

    halo = pl.BlockSpec((8, 3 * D_CONV), lambda i: (jnp.maximum(i * (tb // 8) - 1, 0), 0))
    return pl.pallas_call(
        body, name="merge_bwd", grid=(s // tb,),
        out_shape=[out(D_MODEL, F32), out(D_MODEL, BF16), out(3 * D_MODEL, BF16), out(2 * D_MODEL, BF16),
                   out(D_SSM, F32), out(D_SB, BF16), out(D_MODEL, BF16), out(D_MODEL, BF16), out(D_CONV, BF16),
                   out(D_CONV, F32), out(D_CONV, F32), out(D_MODEL, BF16),
                   jax.ShapeDtypeStruct((1, 3 * D_MODEL), F32), jax.ShapeDtypeStruct((1, D_MODEL), F32)],
        in_specs=[tok(D_MODEL), tok(D_MODEL), tok(D_MODEL), tok(D_SSM), tok(D_SB), tok(3 * D_CONV), halo,
                  _full((1, D_MODEL)), _full((D_MODEL, 3 * D_MODEL)), _full((1, 3 * D_MODEL)),
                  _full((D_SSM, D_MODEL)), _full((D_SSM, D_MODEL)), _full((D_SB, D_MODEL)),
                  _full((8, D_CONV)), _full((D_CONV, D_MODEL)), _full((D_MODEL, D_MODEL)), _full((1, D_MODEL))],
        out_specs=[tok(D_MODEL), tok(D_MODEL), tok(3 * D_MODEL), tok(2 * D_MODEL), tok(D_SSM), tok(D_SB),
                   tok(D_MODEL), tok(D_MODEL), tok(D_CONV), tok(D_CONV), tok(D_CONV), tok(D_MODEL),
                   _full((1, 3 * D_MODEL)), _full((1, D_MODEL))],
        compiler_params=_cparams(1),
    )(dx1, m2, x, y256, o, c3, c3, w["n1"], w["w_gate"], w["b_gate"], w["w_glu_val"], w["w_glu_gate"],
      w["w_attn_out"], w["conv_w"], w["w_conv_out"], w["w_mix_out"], w["n2"])


def _ssm_bwd(u, ypre, dy, states, ab_re, ab_im, bb_re, bb_im, cc_re, cc_im, d_skip):
    s = u.shape[0]
    ts = _token_tile(s)
    nt = s // ts

    def body(u_ref, yp_ref, dy_ref, st_ref, ar_ref, ai_ref, bbr_ref, bbi_ref, ccr_ref, cci_ref, d_ref,
             du_ref, dar_ref, dai_ref, dbbr_ref, dbbi_ref, dccr_ref, dcci_ref, dd_ref,
             bur, bui, hr_s, hi_s, pr_s, pi_s, lr_s, li_s, cr, ci):
        @pl.when(pl.program_id(0) == 0)
        def _():
            cr[...] = jnp.zeros_like(cr)
            ci[...] = jnp.zeros_like(ci)
            for ref in (dar_ref, dai_ref, dbbr_ref, dbbi_ref, dccr_ref, dcci_ref, dd_ref):
                ref[...] = jnp.zeros_like(ref)

        u = u_ref[...]
        ub = u.astype(BF16)
        bur[...] = _dot(ub, bbr_ref[...])
        bui[...] = _dot(ub, bbi_ref[...])
        ar = ar_ref[...]
        ai = ai_ref[...]

        def fwd_step(t, carry):
            hr, hi = carry
            pr_s[pl.ds(t, 1), :] = hr
            pi_s[pl.ds(t, 1), :] = hi
            nr = ar * hr - ai * hi + bur[pl.ds(t, 1), :]
            ni = ar * hi + ai * hr + bui[pl.ds(t, 1), :]
            hr_s[pl.ds(t, 1), :] = nr
            hi_s[pl.ds(t, 1), :] = ni
            return nr, ni

        lax.fori_loop(0, ts, fwd_step, (st_ref[0, 0:1, :], st_ref[0, 1:2, :]), unroll=8)

        dyp = dy_ref[...] * _gelu_grad(yp_ref[...])
        dypb = dyp.astype(BF16)
        lr_s[...] = _dot(dypb, ccr_ref[...])
        li_s[...] = -_dot(dypb, cci_ref[...])

        def bwd_step(tt, carry):
            t = ts - 1 - tt
            nr, ni = carry
            qr = lr_s[pl.ds(t, 1), :] + ar * nr + ai * ni
            qi = li_s[pl.ds(t, 1), :] + ar * ni - ai * nr
            lr_s[pl.ds(t, 1), :] = qr
            li_s[pl.ds(t, 1), :] = qi
            return qr, qi

        nr, ni = lax.fori_loop(0, ts, bwd_step, (cr[...], ci[...]), unroll=8)
        cr[...] = nr
        ci[...] = ni
        lam_r = lr_s[...]
        lam_i = li_s[...]
        pr = pr_s[...]
        pi = pi_s[...]
        dar_ref[...] += jnp.sum(lam_r * pr + lam_i * pi, axis=0, keepdims=True)
        dai_ref[...] += jnp.sum(lam_i * pr - lam_r * pi, axis=0, keepdims=True)
        lrb = lam_r.astype(BF16)
        lib = lam_i.astype(BF16)
        du_ref[...] = _dot_nt(lrb, bbr_ref[...]) + _dot_nt(lib, bbi_ref[...]) + d_ref[...] * dyp
        dbbr_ref[...] += _dot_tn(ub, lrb)
        dbbi_ref[...] += _dot_tn(ub, lib)
        dccr_ref[...] += _dot_tn(dypb, hr_s[...])
        dcci_ref[...] -= _dot_tn(dypb, hi_s[...])
        dd_ref[...] += jnp.sum(dyp * u, axis=0, keepdims=True)

    tok = pl.BlockSpec((ts, D_SSM), lambda i: (nt - 1 - i, 0))
    row = _full((1, D_STATE))
    mat = _full((D_SSM, D_STATE))
    row_o = jax.ShapeDtypeStruct((1, D_STATE), F32)
    mat_o = jax.ShapeDtypeStruct((D_SSM, D_STATE), F32)
    return pl.pallas_call(
        body, name="ssm_bwd", grid=(nt,),
        out_shape=[jax.ShapeDtypeStruct((s, D_SSM), F32), row_o, row_o, mat_o, mat_o, mat_o, mat_o,
                   jax.ShapeDtypeStruct((1, D_SSM), F32)],
        in_specs=[tok, tok, tok, pl.BlockSpec((1, 2, D_STATE), lambda i: (nt - 1 - i, 0, 0)),
                  row, row, mat, mat, mat, mat, _full((1, D_SSM))],
        out_specs=[tok, row, row, mat, mat, mat, mat, _full((1, D_SSM))],
        scratch_shapes=[pltpu.VMEM((ts, D_STATE), F32)] * 8 + [pltpu.VMEM((1, D_STATE), F32)] * 2,
        compiler_params=_cparams(1),
    )(u, ypre, dy, states, ab_re, ab_im, bb_re, bb_im, cc_re, cc_im, d_skip)


def _attn_bwd(q, k, v, do, tab):
    s = q.shape[0]
    t = _attn_tile(s)
    nq = s // t
    n_pairs = D_SB // LANES

    def body(q_ref, do_ref, tab_ref, k_ref, v_ref, dq_ref, dk_hbm, dv_hbm, dk_s, dv_s, sems,
             z_s, ls_s, suf_s, dw_s, dl_s, beta_s, pre_s, wb_s, dq_s, q_s, do_s, qt_s, dot_s, tri_s):
        p = pl.program_id(0)
        i = pl.program_id(1)

        @pl.when(i == 0)
        def _():
            dk_s[...] = jnp.zeros_like(dk_s)
            dv_s[...] = jnp.zeros_like(dv_s)

        lane = lax.broadcasted_iota(jnp.int32, (t, LANES), 1)
        first = lane < HEAD_DIM
        row = lax.broadcasted_iota(jnp.int32, (t, t), 0)
        col = lax.broadcasted_iota(jnp.int32, (t, t), 1)
        tri_s[0] = (row > col).astype(BF16)
        tri_s[1] = (row < col).astype(BF16)
        causal = col < row
        q2 = q_ref[...]
        do2 = do_ref[...]
        zero = jnp.zeros_like(q2)
        for a in range(2):
            mine = first if a == 0 else jnp.logical_not(first)
            q_s[a] = jnp.where(mine, q2, zero)
            do_s[a] = jnp.where(mine, do2, zero)
            qt_s[a] = jnp.where(mine, q2, zero).astype(F32).T.astype(BF16)
            dot_s[a] = jnp.where(mine, do2, zero).astype(F32).T.astype(BF16)
        z_s[...] = jnp.full_like(z_s, -1e30)
        ls_s[...] = jnp.full_like(ls_s, -1e30)
        for ref in (suf_s, dw_s, dl_s, beta_s, pre_s, wb_s, dq_s):
            ref[...] = jnp.zeros_like(ref)

        def tile_rows(j):
            return pl.ds(pl.multiple_of(jnp.maximum(j, 0) * t, t), t)

        def scores(j):
            k2 = k_ref[tile_rows(j), :]
            for a in range(2):
                z_s[a] = _dot_nt(q_s[a], k2)

        def spread(j, valid):
            v2 = v_ref[tile_rows(j), :]
            for a in range(2):
                ls, l1m = _sb_block(z_s[a], valid)
                ls_s[a] = ls
                suf_s[a] = _dot(l1m, tri_s[0])
                dw_s[a] = _dot_nt(do_s[a], v2)

        def weights(j, valid):
            for a in range(2):
                c1 = jnp.sum(jnp.where(lane == j, tab_ref[:, a * LANES:(a + 1) * LANES], 0.0), axis=1, keepdims=True)
                ls = ls_s[a]
                w = jnp.exp2(ls + suf_s[a] + c1)
                if valid is not None:
                    w = jnp.where(valid, w, 0.0)
                dl = w * dw_s[a]
                dl_s[a] = dl
                beta_s[a] = jnp.exp2(ls)
                wb_s[a] = w.astype(BF16)
                pre_s[a] = _dot(dl, tri_s[1])

        def grads(j, c2s, valid):
            rows = tile_rows(j)
            k2 = k_ref[rows, :]
            out = []
            dkt = dvt = None
            for a in range(2):
                dl = dl_s[a]
                pre = pre_s[a]
                dz = dl - beta_s[a] * (dl + pre + c2s[a])
                if valid is not None:
                    dz = jnp.where(valid, dz, 0.0)
                dzb = dz.astype(BF16)
                dq_s[a] += _dot(dzb, k2)
                dk_a = jnp.dot(qt_s[a], dzb, preferred_element_type=F32)
                dv_a = jnp.dot(dot_s[a], wb_s[a], preferred_element_type=F32)
                dkt = dk_a if dkt is None else dkt + dk_a
                dvt = dv_a if dvt is None else dvt + dv_a
                out.append(c2s[a] + pre[:, t - 1:t] + dl[:, t - 1:t])
            dk_s[:, rows] += dkt
            dv_s[:, rows] += dvt
            return tuple(out)

        def trip(n, c2s):
            c2s = grads(n - 3, c2s, None)
            weights(n - 2, None)
            spread(n - 1, None)
            scores(n)
            return c2s

        reach = jnp.max(jnp.maximum(tab_ref[:, 0:LANES], tab_ref[:, LANES:2 * LANES]), axis=0, keepdims=True)
        tile_id = lax.broadcasted_iota(jnp.int32, (1, LANES), 1)
        j_min = jnp.min(jnp.where(reach >= DEAD_LOG2, tile_id, i))
        c2s = lax.fori_loop(j_min, i + 1, trip, tuple(jnp.zeros((t, 1), F32) for _ in range(2)))
        c2s = grads(i - 2, c2s, None)
        weights(i - 1, None)
        spread(i, causal)
        c2s = grads(i - 1, c2s, None)
        weights(i, causal)
        grads(i, c2s, causal)
        dq_ref[...] = Q_SCALE * jnp.where(first, dq_s[0], dq_s[1])

        @pl.when(i == nq - 1)
        def _():
            dk_s[...] = dk_s[...] * (1.0 / LOG2E)
            ck = pltpu.make_async_copy(dk_s, dk_hbm.at[p], sems.at[0])
            cv = pltpu.make_async_copy(dv_s, dv_hbm.at[p], sems.at[1])
            ck.start()
            cv.start()
            ck.wait()
            cv.wait()

    blk = pl.BlockSpec((t, LANES), lambda p, i: (i, p))
    seq = pl.BlockSpec((s, LANES), lambda p, i: (0, p))
    pairs = jax.ShapeDtypeStruct((n_pairs, LANES, s), F32)
    stage = pltpu.VMEM((2, t, t), F32)
    return pl.pallas_call(
        body, name="attn_bwd", grid=(n_pairs, nq),
        out_shape=[jax.ShapeDtypeStruct((s, D_SB), F32), pairs, pairs],
        in_specs=[blk, blk, pl.BlockSpec((t, 2 * LANES), lambda p, i: (i, p)), seq, seq],
        out_specs=[blk, pl.BlockSpec(memory_space=pl.ANY), pl.BlockSpec(memory_space=pl.ANY)],
        scratch_shapes=[pltpu.VMEM((LANES, s), F32), pltpu.VMEM((LANES, s), F32), pltpu.SemaphoreType.DMA((2,))]
        + [stage] * 7 + [pltpu.VMEM((2, t, t), BF16), pltpu.VMEM((2, t, LANES), F32),
                         pltpu.VMEM((2, t, LANES), BF16), pltpu.VMEM((2, t, LANES), BF16),
                         pltpu.VMEM((2, LANES, t), BF16), pltpu.VMEM((2, LANES, t), BF16),
                         pltpu.VMEM((2, t, t), BF16)],
        compiler_params=_cparams(2),
    )(q, do, tab, k, v)


def _pre_bwd(dres, dhg, x, du, dq, dk, dv, dcb, dcv, c3, n1, w_in, conv_w):
    s = x.shape[0]
    tb = _token_tile(s)
    nt = s // tb

    def body(dres_ref, dhg_ref, x_ref, du_ref, dq_ref, dk_ref, dv_ref, dcb_ref, dcv_ref, dnext_ref, c_ref, ch_ref,
             n1_ref, w_ref, cw_ref, dx_ref, dp_ref, dn1_ref, dcw_ref):
        i = pl.program_id(0)

        @pl.when(i == 0)
        def _():
            dn1_ref[...] = jnp.zeros_like(dn1_ref)
            dcw_ref[...] = jnp.zeros_like(dcw_ref)

        c3 = c_ref[...]
        cc, cx = c3[:, 256:512], c3[:, 512:768]
        ch = ch_ref[...]
        z = cc * cx
        zh = jnp.where(i > 0, ch[:, 256:512] * ch[:, 512:768], 0.0)
        rowi = lax.broadcasted_iota(jnp.int32, (tb, D_CONV), 0)
        z1, z2 = _conv_taps(z, zh, rowi)
        dcv = dcv_ref[...]
        nxt = jnp.where(i < nt - 1, dnext_ref[...], 0.0)
        d1 = jnp.where(rowi == tb - 1, nxt[0:1, :], pltpu.roll(dcv, tb - 1, 0))
        d2 = jnp.where(rowi == tb - 2, nxt[0:1, :], jnp.where(rowi == tb - 1, nxt[1:2, :], pltpu.roll(dcv, tb - 2, 0)))
        cw = cw_ref[...]
        dz = cw[2:3, :] * dcv + cw[1:2, :] * d1 + cw[0:1, :] * d2
        dcw_ref[0:1, :] += jnp.sum(dcv * z2, axis=0, keepdims=True)
        dcw_ref[1:2, :] += jnp.sum(dcv * z1, axis=0, keepdims=True)
        dcw_ref[2:3, :] += jnp.sum(dcv * z, axis=0, keepdims=True)
        dp_ref[:, 0:256] = du_ref[...].astype(BF16)
        dp_ref[:, 256:768] = dq_ref[...].astype(BF16)
        dp_ref[:, 768:1280] = dk_ref[...].astype(BF16)
        dp_ref[:, 1280:1792] = dv_ref[...].astype(BF16)
        dp_ref[:, 1792:2048] = dcb_ref[...].astype(BF16)
        dp_ref[:, 2048:2304] = (dz * cx).astype(BF16)
        dp_ref[:, 2304:2560] = (dz * cc).astype(BF16)
        dh = dhg_ref[...] + _dot_nt(dp_ref[...], w_ref[...])
        dx, dn1 = _rms_bwd(dh, x_ref[...], n1_ref[...])
        dx_ref[...] = dres_ref[...] + dx
        dn1_ref[...] += dn1

    def tok(wd):
        return pl.BlockSpec((tb, wd), lambda i: (i, 0))

    halo_prev = pl.BlockSpec((8, 3 * D_CONV), lambda i: (jnp.maximum(i * (tb // 8) - 1, 0), 0))
    halo_next = pl.BlockSpec((8, D_CONV), lambda i: (jnp.minimum((i + 1) * (tb // 8), s // 8 - 1), 0))
    return pl.pallas_call(
        body, name="pre_bwd", grid=(nt,),
        out_shape=[jax.ShapeDtypeStruct((s, D_MODEL), F32), jax.ShapeDtypeStruct((s, D_IN), BF16),
                   jax.ShapeDtypeStruct((1, D_MODEL), F32), jax.ShapeDtypeStruct((8, D_CONV), F32)],
        in_specs=[tok(D_MODEL), tok(D_MODEL), tok(D_MODEL), tok(D_SSM), tok(D_SB), tok(D_SB), tok(D_SB),
                  tok(D_CONV), tok(D_CONV), halo_next, tok(3 * D_CONV), halo_prev,
                  _full((1, D_MODEL)), _full((D_MODEL, D_IN)), _full((8, D_CONV))],
        out_specs=[tok(D_MODEL), tok(D_IN), _full((1, D_MODEL)), _full((8, D_CONV))],
        compiler_params=_cparams(1),
    )(dres, dhg, x, du, dq, dk, dv, dcb, dcv, dcv, c3, c3, n1, w_in, conv_w)


def _padded(n):
    return -(-n // PIECE_ALIGN) * PIECE_ALIGN


def _pack(pieces, total_rows=None):
    flat = []
    for a in pieces:
        a = a.reshape(-1)
        flat.append(jnp.pad(a, (0, _padded(a.size) - a.size)))
    out = jnp.concatenate(flat)
    if total_rows is not None:
        out = jnp.pad(out, (0, total_rows * LANES - out.size))
    return out.reshape(-1, LANES)


def _unpack(flat, shapes, lead=()):
    flat = flat.reshape(lead + (-1,))
    out, off = [], 0
    for shp in shapes:
        n = math.prod(shp)
        out.append(flat[..., off:off + n].reshape(lead + tuple(shp)))
        off += _padded(n)
    return out


def _to_shards(full, axis):
    shp = full.shape
    cut = shp[:axis] + (N_DEV, shp[axis] // N_DEV) + shp[axis + 1:]
    return jnp.moveaxis(full.reshape(cut), axis, 0)


def _from_shards(shards, axis):
    moved = jnp.moveaxis(shards, 0, axis)
    shp = moved.shape
    return moved.reshape(shp[:axis] + (shp[axis] * shp[axis + 1],) + shp[axis + 2:])


def _expand_groups(compact):
    rows = lax.broadcasted_iota(jnp.int32, (D_SSM, D_STATE), 0) // GROUP_CH
    cols = lax.broadcasted_iota(jnp.int32, (D_SSM, D_STATE), 1) // N_STATE
    return jnp.where(rows == cols, jnp.tile(compact, (N_GROUPS, 1)), 0.0)


def _collect_groups(dense):
    rows = lax.broadcasted_iota(jnp.int32, (D_SSM, D_STATE), 0) // GROUP_CH
    cols = lax.broadcasted_iota(jnp.int32, (D_SSM, D_STATE), 1) // N_STATE
    return jnp.where(rows == cols, dense, 0.0).reshape(N_GROUPS, GROUP_CH, D_STATE).sum(axis=0)


def kernel(x, norm_mix_pre, norm_mix_post, w_in, w_gate, b_gate, ssm_a_re, ssm_a_im, ssm_log_dt, ssm_b_re, ssm_b_im, ssm_c_re, ssm_c_im, ssm_d, w_glu_val, w_glu_gate, w_attn_out, conv_w, w_conv_out, w_mix_out, norm_ffn_pre, norm_ffn_post, w_ffn_up, w_ffn_down, loss_target, m_norm_mix_pre, m_norm_mix_post, m_w_in, m_w_gate, m_b_gate, m_ssm_a_re, m_ssm_a_im, m_ssm_log_dt, m_ssm_b_re, m_ssm_b_im, m_ssm_c_re, m_ssm_c_im, m_ssm_d, m_w_glu_val, m_w_glu_gate, m_w_attn_out, m_conv_w, m_w_conv_out, m_w_mix_out, m_norm_ffn_pre, m_norm_ffn_post, m_w_ffn_up, m_w_ffn_down, v_norm_mix_pre, v_norm_mix_post, v_w_in, v_w_gate, v_b_gate, v_ssm_a_re, v_ssm_a_im, v_ssm_log_dt, v_ssm_b_re, v_ssm_b_im, v_ssm_c_re, v_ssm_c_im, v_ssm_d, v_w_glu_val, v_w_glu_gate, v_w_attn_out, v_conv_w, v_w_conv_out, v_w_mix_out, v_norm_ffn_pre, v_norm_ffn_post, v_w_ffn_up, v_w_ffn_down):
    args = dict(locals())
    wts = {n: args[n] for n in WEIGHTS}
    mom = {n: args["m_" + n] for n in WEIGHTS}
    vel = {n: args["v_" + n] for n in WEIGHTS}
    seq = x.shape[1]
    x0 = x.reshape(seq, D_MODEL)
    target = loss_target.reshape(seq, D_MODEL)

    pieces = []
    for n in SHARDED:
        if n == "conv_w":
            hi = wts[n].astype(BF16)
            pieces += [hi, (wts[n] - hi.astype(F32)).astype(BF16)]
        else:
            pieces.append(wts[n].astype(BF16))
    shard_shapes = [p.shape for p in pieces]
    gathered = _unpack(_all_gather(_pack(pieces)), shard_shapes, lead=(N_DEV,))
    full = {}
    it = iter(gathered)
    for n in SHARDED:
        if n == "conv_w":
            full[n] = _from_shards(next(it).astype(F32) + next(it).astype(F32), SHARD_AXIS[n])
        else:
            full[n] = _from_shards(next(it), SHARD_AXIS[n])

    def layer_weights(l):
        return dict(
            n1=norm_mix_pre[l][None], n2=norm_mix_post[l][None], n3=norm_ffn_pre[l][None], n4=norm_ffn_post[l][None],
            w_in=full["w_in"][l], w_gate=full["w_gate"][l], b_gate=b_gate[l][None],
            w_glu_val=full["w_glu_val"][l], w_glu_gate=full["w_glu_gate"][l], w_attn_out=full["w_attn_out"][l],
            conv_w=jnp.pad(full["conv_w"][l], ((0, 5), (0, 0))), w_conv_out=full["w_conv_out"][l],
            w_mix_out=full["w_mix_out"][l], w_ffn_up=full["w_ffn_up"][l], w_ffn_down=full["w_ffn_down"][l],
            lr=ssm_a_re[l].reshape(1, D_STATE), li=ssm_a_im[l].reshape(1, D_STATE),
            ldt=jnp.repeat(ssm_log_dt[l], N_STATE).reshape(1, D_STATE),
            br=jnp.transpose(ssm_b_re[l], (2, 0, 1)).reshape(GROUP_CH, D_STATE),
            bi=jnp.transpose(ssm_b_im[l], (2, 0, 1)).reshape(GROUP_CH, D_STATE),
            cc_re=_expand_groups(jnp.transpose(ssm_c_re[l], (1, 0, 2)).reshape(GROUP_CH, D_STATE)).astype(BF16),
            cc_im=_expand_groups(jnp.transpose(ssm_c_im[l], (1, 0, 2)).reshape(GROUP_CH, D_STATE)).astype(BF16),
            d_skip=ssm_d[l][None],
        )

    saved = []
    xin = x0
    for l in range(DEPTH):
        w = layer_weights(l)
        ab_re, ab_im, bbr, bbi = _ssm_params_fwd(w["lr"], w["li"], w["ldt"], w["br"], w["bi"])
        w.update(ab_re=ab_re, ab_im=ab_im, bb_re=_expand_groups(bbr).astype(BF16), bb_im=_expand_groups(bbi).astype(BF16))
        u, q, k, v, c3 = _pre_fwd(xin, w["n1"], w["w_in"])
        y256, ypre, states = _ssm_fwd(u, ab_re, ab_im, w["bb_re"], w["bb_im"], w["cc_re"], w["cc_im"], w["d_skip"])
        o, tab = _attn_fwd(q, k, v)
        x1, m2, merged = _merge_fwd(xin, y256, o, c3, w)
        x2, f = _ffn_fwd(x1, w["n3"], w["w_ffn_up"], w["w_ffn_down"], w["n4"])
        saved.append(dict(w=w, x0=xin, u=u, q=q, k=k, v=v, c3=c3, y256=y256, ypre=ypre, states=states, o=o, tab=tab, x1=x1,
                          m2=m2, merged=merged, f=f))
        xin = x2

    dxo, loss_part = _loss_head(xin, target)
    loss = lax.psum(loss_part[0, 0], ("x", "y", "c"))

    grads = {n: [None] * DEPTH for n in WEIGHTS}
    for l in reversed(range(DEPTH)):
        sv = saved[l]
        w = sv["w"]
        dx1, r, da, h2, df, dn3, dn4 = _ffn_bwd(dxo, sv["x1"], sv["f"], w["n3"], w["w_ffn_up"], w["w_ffn_down"], w["n4"])
        grads["w_ffn_down"][l] = _tn_matmul(r, df, "grad_w_ffn_down")
        grads["w_ffn_up"][l] = _tn_matmul(h2, da, "grad_w_ffn_up")
        grads["norm_ffn_pre"][l] = dn3[0]
        grads["norm_ffn_post"][l] = dn4[0]
        (dhg, hb, dgz, dvs, dy256, do, dyb, dyc, bcv, dcv, dcb, dm2, dbg, dn2) = _merge_bwd(
            dx1, sv["x0"], sv["m2"], sv["y256"], sv["o"], sv["c3"], w)
        grads["w_mix_out"][l] = _tn_matmul(sv["merged"], dm2, "grad_w_mix_out")
        grads["w_gate"][l] = _tn_matmul(hb, dgz, "grad_w_gate")
        dglu = _tn_matmul(sv["y256"], dvs, "grad_w_glu")
        grads["w_glu_val"][l] = dglu[:, :D_MODEL]
        grads["w_glu_gate"][l] = dglu[:, D_MODEL:]
        grads["w_attn_out"][l] = _tn_matmul(sv["o"], dyb, "grad_w_attn_out")
        grads["w_conv_out"][l] = _tn_matmul(bcv, dyc, "grad_w_conv_out")
        grads["b_gate"][l] = dbg[0]
        grads["norm_mix_post"][l] = dn2[0]
        du, dar, dai, dbbr, dbbi, dccr, dcci, dd = _ssm_bwd(
            sv["u"], sv["ypre"], dy256, sv["states"], w["ab_re"], w["ab_im"], w["bb_re"], w["bb_im"],
            w["cc_re"], w["cc_im"], w["d_skip"])
        glr, gli, gdt, gbr, gbi = _ssm_params_bwd(w["lr"], w["li"], w["ldt"], w["br"], w["bi"], dar, dai,
                                                   _collect_groups(dbbr), _collect_groups(dbbi))
        grads["ssm_a_re"][l] = glr.reshape(N_GROUPS, N_STATE)
        grads["ssm_a_im"][l] = gli.reshape(N_GROUPS, N_STATE)
        grads["ssm_log_dt"][l] = gdt[:, 0]
        grads["ssm_b_re"][l] = jnp.transpose(gbr.reshape(GROUP_CH, N_GROUPS, N_STATE), (1, 2, 0))
        grads["ssm_b_im"][l] = jnp.transpose(gbi.reshape(GROUP_CH, N_GROUPS, N_STATE), (1, 2, 0))
        grads["ssm_c_re"][l] = jnp.transpose(_collect_groups(dccr).reshape(GROUP_CH, N_GROUPS, N_STATE), (1, 0, 2))
        grads["ssm_c_im"][l] = jnp.transpose(_collect_groups(dcci).reshape(GROUP_CH, N_GROUPS, N_STATE), (1, 0, 2))
        grads["ssm_d"][l] = dd[0]
        dq, dk4, dv4 = _attn_bwd(sv["q"], sv["k"], sv["v"], do, sv["tab"])
        dk = jnp.transpose(dk4, (2, 0, 1)).reshape(seq, D_SB)
        dv = jnp.transpose(dv4, (2, 0, 1)).reshape(seq, D_SB)
        dxo, dp, dn1, dcw = _pre_bwd(dx1, dhg, sv["x0"], du, dq, dk, dv, dcb, dcv, sv["c3"], w["n1"], w["w_in"], w["conv_w"])
        grads["w_in"][l] = _tn_matmul(hb, dp, "grad_w_in")
        grads["norm_mix_pre"][l] = dn1[0]
        grads["conv_w"][l] = dcw[0:3]
    grad_x = dxo.reshape(x.shape)
    gfull = {n: jnp.stack(grads[n]) for n in WEIGHTS}

    rep_parts = _pack([gfull[n] for n in REPLICATED])
    sh_parts = jnp.stack([_pack([_to_shards(gfull[n], SHARD_AXIS[n])[d] for n in SHARDED]) for d in range(N_DEV)])
    rows = sh_parts.shape[1] + rep_parts.shape[0]
    total_rows = -(-rows // 512) * 512
    parts = jnp.concatenate([sh_parts, jnp.broadcast_to(rep_parts, (N_DEV,) + rep_parts.shape),
                             jnp.zeros((N_DEV, total_rows - rows, LANES), F32)], axis=1)
    recv = _reduce_scatter(parts)

    def mine(d):
        sh = _pack([d[n] for n in SHARDED])
        rp = _pack([d[n] for n in REPLICATED])
        return jnp.concatenate([sh, rp, jnp.zeros((total_rows - rows, LANES), F32)], axis=0)

    flat_out = _adamw(recv, mine(wts), mine(mom), mine(vel))
    n_sh_rows = sh_parts.shape[1]
    results = []
    for fo in flat_out:
        sh = _unpack(fo[:n_sh_rows], [wts[n].shape for n in SHARDED])
        rp = _unpack(fo[n_sh_rows:rows], [wts[n].shape for n in REPLICATED])
        by_name = dict(zip(SHARDED, sh))
        by_name.update(zip(REPLICATED, rp))
        results.append([by_name[n] for n in WEIGHTS])
    g_out, d_out, m_out, v_out = results
    return (loss, grad_x, *g_out, *d_out, *m_out, *v_out)
```

```python
import functools
import math

import jax
import jax.numpy as jnp
from jax import lax
from jax.experimental import pallas as pl
from jax.experimental.pallas import tpu as pltpu

F32 = jnp.float32
BF16 = jnp.bfloat16

N_DEV = 8
D_MODEL = 1024
DEPTH = 2
D_SSM = 256
N_GROUPS = 16
GROUP_CH = 16
N_STATE = 64
D_STATE = N_GROUPS * N_STATE
D_SB = 512
HEAD_DIM = 64
D_CONV = 256
D_IN = 2560
D_FF = 4096
EPS = 1e-6
Q_SCALE = HEAD_DIM ** -0.5
LOG2E = math.log2(math.e)
DEAD_LOG2 = -160.0

ADAM_LR = 0.001
ADAM_B1 = 0.9
ADAM_B2 = 0.999
ADAM_EPS = 1e-08
ADAM_WD = 0.01
ADAM_STEP = 10

LANES = 128
PIECE_ALIGN = 2048
VMEM_LIMIT = 56 * 1024 * 1024

WEIGHTS = ['norm_mix_pre', 'norm_mix_post', 'w_in', 'w_gate', 'b_gate', 'ssm_a_re', 'ssm_a_im', 'ssm_log_dt',
           'ssm_b_re', 'ssm_b_im', 'ssm_c_re', 'ssm_c_im', 'ssm_d', 'w_glu_val', 'w_glu_gate', 'w_attn_out',
           'conv_w', 'w_conv_out', 'w_mix_out', 'norm_ffn_pre', 'norm_ffn_post', 'w_ffn_up', 'w_ffn_down']
SHARD_AXIS = {'w_in': 2, 'w_gate': 2, 'w_glu_val': 2, 'w_glu_gate': 2, 'w_attn_out': 2, 'conv_w': 2,
              'w_conv_out': 2, 'w_mix_out': 1, 'w_ffn_up': 2, 'w_ffn_down': 1}
SHARDED = [n for n in WEIGHTS if n in SHARD_AXIS]
REPLICATED = [n for n in WEIGHTS if n not in SHARD_AXIS]


def _cparams(n_grid):
    return pltpu.CompilerParams(dimension_semantics=("arbitrary",) * n_grid, vmem_limit_bytes=VMEM_LIMIT)


def _dot(a, b):
    return jnp.dot(a.astype(BF16), b.astype(BF16), preferred_element_type=F32)


def _dot_nt(a, b):
    return lax.dot_general(a.astype(BF16), b.astype(BF16), (((1,), (1,)), ((), ())), preferred_element_type=F32)


def _dot_tn(a, b):
    return lax.dot_general(a.astype(BF16), b.astype(BF16), (((0,), (0,)), ((), ())), preferred_element_type=F32)


def _split_dot(a, b):
    hi = a.astype(BF16)
    lo = (a - hi.astype(F32)).astype(BF16)
    return jnp.dot(hi, b, preferred_element_type=F32) + jnp.dot(lo, b, preferred_element_type=F32)


def _rms_fwd(x, g):
    r = lax.rsqrt(jnp.mean(x * x, axis=-1, keepdims=True) + EPS)
    return x * r * g


def _rms_bwd(dy, x, g):
    r = lax.rsqrt(jnp.mean(x * x, axis=-1, keepdims=True) + EPS)
    dyg = dy * g
    dx = r * dyg - x * (r * r * r) * jnp.mean(dyg * x, axis=-1, keepdims=True)
    dg = jnp.sum(dy * x * r, axis=0, keepdims=True)
    return dx, dg


def _sigmoid(x):
    return 1.0 / (1.0 + jnp.exp(-x))


_GELU_C = math.sqrt(2.0 / math.pi)


def _gelu(y):
    return 0.5 * y * (1.0 + jnp.tanh(_GELU_C * (y + 0.044715 * y * y * y)))


def _gelu_grad(y):
    t = jnp.tanh(_GELU_C * (y + 0.044715 * y * y * y))
    return 0.5 * (1.0 + t) + 0.5 * y * (1.0 - t * t) * _GELU_C * (1.0 + 3.0 * 0.044715 * y * y)


def _full(shape):
    return pl.BlockSpec(shape, lambda *_: (0,) * len(shape))


def _peer(x, y, c, k):
    px = 1 - x if (k >> 2) & 1 else x
    py = 1 - y if (k >> 1) & 1 else y
    pc = 1 - c if k & 1 else c
    return px, py, pc


def _all_gather(shard):
    rows, lanes = shard.shape

    def body(src_ref, out_ref, send_sems, recv_sems, local_sem):
        x, y, c = lax.axis_index("x"), lax.axis_index("y"), lax.axis_index("c")
        me = 4 * x + 2 * y + c
        mine = pltpu.make_async_copy(src_ref, out_ref.at[me], local_sem)
        mine.start()
        copies = []
        for k in range(1, N_DEV):
            cp = pltpu.make_async_remote_copy(
                src_ref=src_ref, dst_ref=out_ref.at[me],
                send_sem=send_sems.at[k - 1], recv_sem=recv_sems.at[k - 1],
                device_id=_peer(x, y, c, k), device_id_type=pl.DeviceIdType.MESH)
            cp.start()
            copies.append(cp)
        for cp in copies:
            cp.wait()
        mine.wait()

    return pl.pallas_call(
        body, name="weights_all_gather",
        out_shape=jax.ShapeDtypeStruct((N_DEV, rows, lanes), shard.dtype),
        in_specs=[pl.BlockSpec(memory_space=pl.ANY)],
        out_specs=pl.BlockSpec(memory_space=pl.ANY),
        scratch_shapes=[pltpu.SemaphoreType.DMA((N_DEV - 1,)), pltpu.SemaphoreType.DMA((N_DEV - 1,)),
                        pltpu.SemaphoreType.DMA],
    )(shard)


def _all_to_all(parts):
    n, rows, lanes = parts.shape

    def body(src_ref, out_ref, send_sems, recv_sems, local_sem):
        x, y, c = lax.axis_index("x"), lax.axis_index("y"), lax.axis_index("c")
        me = 4 * x + 2 * y + c
        mine = pltpu.make_async_copy(src_ref.at[me], out_ref.at[me], local_sem)
        mine.start()
        copies = []
        for k in range(1, N_DEV):
            px, py, pc = _peer(x, y, c, k)
            cp = pltpu.make_async_remote_copy(
                src_ref=src_ref.at[4 * px + 2 * py + pc], dst_ref=out_ref.at[me],
                send_sem=send_sems.at[k - 1], recv_sem=recv_sems.at[k - 1],
                device_id=(px, py, pc), device_id_type=pl.DeviceIdType.MESH)
            cp.start()
            copies.append(cp)
        for cp in copies:
            cp.wait()
        mine.wait()

    return pl.pallas_call(
        body, name="grads_all_to_all",
        out_shape=jax.ShapeDtypeStruct((N_DEV, rows, lanes), parts.dtype),
        in_specs=[pl.BlockSpec(memory_space=pl.ANY)],
        out_specs=pl.BlockSpec(memory_space=pl.ANY),
        scratch_shapes=[pltpu.SemaphoreType.DMA((N_DEV - 1,)), pltpu.SemaphoreType.DMA((N_DEV - 1,)),
                        pltpu.SemaphoreType.DMA],
    )(parts)


def _adamw(recv, w, m, v):
    rows = w.shape[0]
    n_src = recv.shape[0]
    tr = 512
    assert rows % tr == 0
    c1 = 1.0 / (1.0 - ADAM_B1 ** ADAM_STEP)
    c2 = 1.0 / (1.0 - ADAM_B2 ** ADAM_STEP)

    def body(r_ref, w_ref, m_ref, v_ref, g_ref, d_ref, mo_ref, vo_ref):
        g = r_ref[0]
        for s in range(1, n_src):
            g = g + r_ref[s]
        mn = ADAM_B1 * m_ref[...] + (1.0 - ADAM_B1) * g
        vn = ADAM_B2 * v_ref[...] + (1.0 - ADAM_B2) * (g * g)
        upd = (mn * c1) / (jnp.sqrt(vn * c2) + ADAM_EPS) + ADAM_WD * w_ref[...]
        g_ref[...] = g
        d_ref[...] = -ADAM_LR * upd
        mo_ref[...] = mn
        vo_ref[...] = vn

    spec = pl.BlockSpec((tr, LANES), lambda i: (i, 0))
    return pl.pallas_call(
        body, name="adamw", grid=(rows // tr,),
        out_shape=[jax.ShapeDtypeStruct((rows, LANES), F32)] * 4,
        in_specs=[pl.BlockSpec((n_src, tr, LANES), lambda i: (0, i, 0)), spec, spec, spec],
        out_specs=[spec] * 4,
        compiler_params=_cparams(1),
    )(recv, w, m, v)


def _tn_matmul(a, b, name):
    s, m = a.shape
    n = b.shape[1]
    tk = min(1024, s)
    tm = min(1024, m)
    tn = max(c for c in range(LANES, 1280 + 1, LANES) if n % c == 0)
    assert m % tm == 0 and s % tk == 0
    nk = s // tk

    def body(a_ref, b_ref, o_ref):
        k = pl.program_id(2)

        @pl.when(k == 0)
        def _():
            o_ref[...] = jnp.zeros_like(o_ref)

        o_ref[...] += _dot_tn(a_ref[...], b_ref[...])

    return pl.pallas_call(
        body, name=name, grid=(m // tm, n // tn, nk),
        out_shape=jax.ShapeDtypeStruct((m, n), F32),
        in_specs=[pl.BlockSpec((tk, tm), lambda i, j, k: (k, i)), pl.BlockSpec((tk, tn), lambda i, j, k: (k, j))],
        out_specs=pl.BlockSpec((tm, tn), lambda i, j, k: (i, j)),
        compiler_params=_cparams(3),
    )(a, b)


def _token_tile(s):
    return min(256, s)


def _pre_fwd(x, n1, w_in):
    s = x.shape[0]
    tb = _token_tile(s)

    def body(x_ref, n_ref, w_ref, u_ref, q_ref, k_ref, v_ref, c_ref):
        h = _rms_fwd(x_ref[...], n_ref[...])
        p = _dot(h, w_ref[...])
        u_ref[...] = p[:, 0:256]
        q_ref[...] = (p[:, 256:768] * (Q_SCALE * LOG2E)).astype(BF16)
        k_ref[...] = p[:, 768:1280].astype(BF16)
        v_ref[...] = p[:, 1280:1792].astype(BF16)
        c_ref[...] = p[:, 1792:2560]

    def tok(w):
        return pl.BlockSpec((tb, w), lambda i: (i, 0))

    return pl.pallas_call(
        body, name="pre_fwd", grid=(s // tb,),
        out_shape=[jax.ShapeDtypeStruct((s, D_SSM), F32), jax.ShapeDtypeStruct((s, D_SB), BF16),
                   jax.ShapeDtypeStruct((s, D_SB), BF16), jax.ShapeDtypeStruct((s, D_SB), BF16),
                   jax.ShapeDtypeStruct((s, 3 * D_CONV), F32)],
        in_specs=[tok(D_MODEL), _full((1, D_MODEL)), _full((D_MODEL, D_IN))],
        out_specs=[tok(D_SSM), tok(D_SB), tok(D_SB), tok(D_SB), tok(3 * D_CONV)],
        compiler_params=_cparams(1),
    )(x, n1, w_in)


def _ssm_discretize(lr, li, ldt, br, bi):
    dt = jnp.exp(ldt)
    mag = jnp.exp(lr * dt)
    ab_re = mag * jnp.cos(li * dt)
    ab_im = mag * jnp.sin(li * dt)
    den = lr * lr + li * li
    xr = ab_re - 1.0
    coef_re = (xr * lr + ab_im * li) / den
    coef_im = (ab_im * lr - xr * li) / den
    bb_re = coef_re * br - coef_im * bi
    bb_im = coef_re * bi + coef_im * br
    return ab_re, ab_im, bb_re, bb_im


def _ssm_params_fwd(lr, li, ldt, br, bi):
    def body(lr_ref, li_ref, ldt_ref, br_ref, bi_ref, ar_ref, ai_ref, bbr_ref, bbi_ref):
        ar, ai, bbr, bbi = _ssm_discretize(lr_ref[...], li_ref[...], ldt_ref[...], br_ref[...], bi_ref[...])
        ar_ref[...] = ar
        ai_ref[...] = ai
        bbr_ref[...] = bbr
        bbi_ref[...] = bbi

    row = jax.ShapeDtypeStruct((1, D_STATE), F32)
    mat = jax.ShapeDtypeStruct((GROUP_CH, D_STATE), F32)
    return pl.pallas_call(body, name="ssm_params_fwd", out_shape=[row, row, mat, mat])(lr, li, ldt, br, bi)


def _ssm_params_bwd(lr, li, ldt, br, bi, d_ar, d_ai, d_bbr, d_bbi):
    def body(lr_ref, li_ref, ldt_ref, br_ref, bi_ref, dar_ref, dai_ref, dbbr_ref, dbbi_ref,
             glr_ref, gli_ref, gdt_ref, gbr_ref, gbi_ref):
        _, vjp = jax.vjp(_ssm_discretize, lr_ref[...], li_ref[...], ldt_ref[...], br_ref[...], bi_ref[...])
        glr, gli, gdt, gbr, gbi = vjp((dar_ref[...], dai_ref[...], dbbr_ref[...], dbbi_ref[...]))
        glr_ref[...] = glr
        gli_ref[...] = gli
        gbr_ref[...] = gbr
        gbi_ref[...] = gbi
        grp = lax.broadcasted_iota(jnp.int32, (N_GROUPS, D_STATE), 0)
        col = lax.broadcasted_iota(jnp.int32, (N_GROUPS, D_STATE), 1)
        own = (col // N_STATE) == grp
        per_group = jnp.sum(jnp.where(own, jnp.broadcast_to(gdt, (N_GROUPS, D_STATE)), 0.0), axis=1, keepdims=True)
        gdt_ref[...] = jnp.broadcast_to(per_group, (N_GROUPS, LANES))

    row = jax.ShapeDtypeStruct((1, D_STATE), F32)
    mat = jax.ShapeDtypeStruct((GROUP_CH, D_STATE), F32)
    return pl.pallas_call(
        body, name="ssm_params_bwd",
        out_shape=[row, row, jax.ShapeDtypeStruct((N_GROUPS, LANES), F32), mat, mat],
    )(lr, li, ldt, br, bi, d_ar, d_ai, d_bbr, d_bbi)


def _ssm_fwd(u, ab_re, ab_im, bb_re, bb_im, cc_re, cc_im, d_skip):
    s = u.shape[0]
    ts = _token_tile(s)
    nt = s // ts

    def body(u_ref, ar_ref, ai_ref, bbr_ref, bbi_ref, ccr_ref, cci_ref, d_ref, y_ref, ypre_ref, st_ref,
             bur, bui, hr_s, hi_s, cr, ci):
        @pl.when(pl.program_id(0) == 0)
        def _():
            cr[...] = jnp.zeros_like(cr)
            ci[...] = jnp.zeros_like(ci)

        u = u_ref[...]
        bur[...] = _dot(u, bbr_ref[...])
        bui[...] = _dot(u, bbi_ref[...])
        st_ref[0, 0:1, :] = cr[...]
        st_ref[0, 1:2, :] = ci[...]
        ar = ar_ref[...]
        ai = ai_ref[...]

        def step(t, carry):
            hr, hi = carry
            nr = ar * hr - ai * hi + bur[pl.ds(t, 1), :]
            ni = ar * hi + ai * hr + bui[pl.ds(t, 1), :]
            hr_s[pl.ds(t, 1), :] = nr
            hi_s[pl.ds(t, 1), :] = ni
            return nr, ni

        hr, hi = lax.fori_loop(0, ts, step, (cr[...], ci[...]), unroll=8)
        cr[...] = hr
        ci[...] = hi
        y = _dot_nt(hr_s[...], ccr_ref[...]) - _dot_nt(hi_s[...], cci_ref[...]) + d_ref[...] * u
        ypre_ref[...] = y
        y_ref[...] = _gelu(y).astype(BF16)

    tok = pl.BlockSpec((ts, D_SSM), lambda i: (i, 0))
    row = _full((1, D_STATE))
    mat = _full((D_SSM, D_STATE))
    return pl.pallas_call(
        body, name="ssm_fwd", grid=(nt,),
        out_shape=[jax.ShapeDtypeStruct((s, D_SSM), BF16), jax.ShapeDtypeStruct((s, D_SSM), F32),
                   jax.ShapeDtypeStruct((nt, 2, D_STATE), F32)],
        in_specs=[tok, row, row, mat, mat, mat, mat, _full((1, D_SSM))],
        out_specs=[tok, tok, pl.BlockSpec((1, 2, D_STATE), lambda i: (i, 0, 0))],
        scratch_shapes=[pltpu.VMEM((ts, D_STATE), F32)] * 4 + [pltpu.VMEM((1, D_STATE), F32)] * 2,
        compiler_params=_cparams(1),
    )(u, ab_re, ab_im, bb_re, bb_im, cc_re, cc_im, d_skip)


def _attn_tile(s):
    return min(256, s)


def _sb_block(z, valid):
    lp = jnp.log2(1.0 + jnp.exp2(-jnp.abs(z)))
    ls = jnp.minimum(z, 0.0) - lp
    l1m = ls - z
    if valid is not None:
        l1m = jnp.where(valid, l1m, 0.0)
    return ls, l1m


def _attn_fwd(q, k, v):
    s = q.shape[0]
    t = _attn_tile(s)
    nq = s // t
    assert nq <= LANES

    def body(q_ref, k_ref, v_ref, o_ref, tab_ref, z_s, ls_s, lg_s, q_s, tri_s, acc_s):
        i = pl.program_id(1)
        lane = lax.broadcasted_iota(jnp.int32, (t, LANES), 1)
        first = lane < HEAD_DIM
        row = lax.broadcasted_iota(jnp.int32, (t, t), 0)
        col = lax.broadcasted_iota(jnp.int32, (t, t), 1)
        tri_s[...] = (row > col).astype(BF16)
        causal = col < row
        q2 = q_ref[...]
        zero = jnp.zeros_like(q2)
        q_s[0] = jnp.where(first, q2, zero)
        q_s[1] = jnp.where(first, zero, q2)
        acc_s[...] = jnp.zeros_like(acc_s)
        tab_ref[...] = jnp.full_like(tab_ref, -1e30)

        def scores(j):
            k2 = k_ref[pl.ds(pl.multiple_of(jnp.maximum(j, 0) * t, t), t), :]
            for a in range(2):
                z_s[a] = _dot_nt(q_s[a], k2)

        def logits(j, carries, valid):
            out = []
            for a in range(2):
                c1 = carries[a]
                ls, l1m = _sb_block(z_s[a], valid)
                ls_s[a] = ls + c1
                suffix = _dot(l1m, tri_s[...])
                lg_s[a] = ls_s[a] + suffix
                cols = slice(a * LANES, (a + 1) * LANES)
                tab_ref[:, cols] = jnp.where(lane == j, c1, tab_ref[:, cols])
                out.append(c1 + suffix[:, 0:1] + l1m[:, 0:1])
            return tuple(out)

        def accumulate(j, valid):
            v2 = v_ref[pl.ds(pl.multiple_of(jnp.maximum(j, 0) * t, t), t), :]
            for a in range(2):
                w = jnp.exp2(lg_s[a])
                if valid is not None:
                    w = jnp.where(valid, w, 0.0)
                acc_s[a] += _dot(w, v2)

        carries = tuple(jnp.zeros((t, 1), F32) for _ in range(2))
        scores(i)
        carries = logits(i, carries, causal)
        scores(i - 1)
        accumulate(i, causal)
        carries = logits(i - 1, carries, None)
        scores(i - 2)

        def trip(n, carries):
            accumulate(i - n + 2, None)
            carries = logits(i - n + 1, carries, None)
            scores(i - n)
            return carries

        def alive(c):
            n, carries = c
            return (n <= i + 1) & (jnp.max(jnp.maximum(carries[0], carries[1])) >= DEAD_LOG2)

        n_end, _ = lax.while_loop(alive, lambda c: (c[0] + 1, trip(c[0], c[1])), (jnp.int32(3), carries))

        @pl.when(i - n_end + 2 >= 0)
        def _():
            accumulate(i - n_end + 2, None)

        o_ref[...] = jnp.where(first, acc_s[0], acc_s[1]).astype(BF16)

    return pl.pallas_call(
        body, name="attn_fwd", grid=(D_SB // LANES, nq),
        out_shape=[jax.ShapeDtypeStruct((s, D_SB), BF16), jax.ShapeDtypeStruct((s, 2 * D_SB), F32)],
        in_specs=[pl.BlockSpec((t, LANES), lambda p, i: (i, p)), pl.BlockSpec((s, LANES), lambda p, i: (0, p)),
                  pl.BlockSpec((s, LANES), lambda p, i: (0, p))],
        out_specs=[pl.BlockSpec((t, LANES), lambda p, i: (i, p)), pl.BlockSpec((t, 2 * LANES), lambda p, i: (i, p))],
        scratch_shapes=[pltpu.VMEM((2, t, t), F32)] * 3 + [pltpu.VMEM((2, t, LANES), BF16), pltpu.VMEM((t, t), BF16),
                                                           pltpu.VMEM((2, t, LANES), F32)],
        compiler_params=_cparams(2),
    )(q, k, v)


def _conv_taps(z, halo, rowi):
    z1 = jnp.where(rowi == 0, halo[7:8, :], pltpu.roll(z, 1, 0))
    z2 = jnp.where(rowi == 0, halo[6:7, :], jnp.where(rowi == 1, halo[7:8, :], pltpu.roll(z, 2, 0)))
    return z1, z2


def _merge_branches(x_ref, y_ref, o_ref, c_ref, ch_ref, n1_ref, wg_ref, bg_ref, wv_ref, wsg_ref, wao_ref,
                    cw_ref, wc_ref, tb):
    i = pl.program_id(0)
    h = _rms_fwd(x_ref[...], n1_ref[...])
    g = _sigmoid(_dot(h, wg_ref[...]) + bg_ref[...])
    y256 = y_ref[...]
    val = _dot(y256, wv_ref[...])
    sg = _sigmoid(_dot(y256, wsg_ref[...]))
    ysb = _dot(o_ref[...], wao_ref[...])
    c3 = c_ref[...]
    cb, cc, cx = c3[:, 0:256], c3[:, 256:512], c3[:, 512:768]
    ch = ch_ref[...]
    z = cc * cx
    zh = jnp.where(i > 0, ch[:, 256:512] * ch[:, 512:768], 0.0)
    rowi = lax.broadcasted_iota(jnp.int32, (tb, D_CONV), 0)
    z1, z2 = _conv_taps(z, zh, rowi)
    cw = cw_ref[...]
    cv = cw[0:1, :] * z2 + cw[1:2, :] * z1 + cw[2:3, :] * z
    yc = _dot(cb * cv, wc_ref[...])
    return dict(h=h, g=g, val=val, sg=sg, ysb=ysb, cb=cb, cv=cv, yc=yc)


def _merge_fwd(x, y256, o, c3, w):
    s = x.shape[0]
    tb = _token_tile(s)

    def body(x_ref, y_ref, o_ref, c_ref, ch_ref, n1_ref, wg_ref, bg_ref, wv_ref, wsg_ref, wao_ref, cw_ref, wc_ref,
             wm_ref, n2_ref, x1_ref, m2_ref, mg_ref):
        br = _merge_branches(x_ref, y_ref, o_ref, c_ref, ch_ref, n1_ref, wg_ref, bg_ref, wv_ref, wsg_ref, wao_ref,
                             cw_ref, wc_ref, tb)
        g = br["g"]
        merged = (g[:, 0:1024] * (br["val"] * br["sg"]) + g[:, 1024:2048] * br["ysb"] + g[:, 2048:3072] * br["yc"])
        mg_ref[...] = merged.astype(BF16)
        m2 = _dot(merged, wm_ref[...])
        m2_ref[...] = m2
        x1_ref[...] = x_ref[...] + _rms_fwd(m2, n2_ref[...])

    def tok(wd):
        return pl.BlockSpec((tb, wd), lambda i: (i, 0))

    halo = pl.BlockSpec((8, 3 * D_CONV), lambda i: (jnp.maximum(i * (tb // 8) - 1, 0), 0))
    return pl.pallas_call(
        body, name="merge_fwd", grid=(s // tb,),
        out_shape=[jax.ShapeDtypeStruct((s, D_MODEL), F32), jax.ShapeDtypeStruct((s, D_MODEL), F32),
                   jax.ShapeDtypeStruct((s, D_MODEL), BF16)],
        in_specs=[tok(D_MODEL), tok(D_SSM), tok(D_SB), tok(3 * D_CONV), halo,
                  _full((1, D_MODEL)), _full((D_MODEL, 3 * D_MODEL)), _full((1, 3 * D_MODEL)),
                  _full((D_SSM, D_MODEL)), _full((D_SSM, D_MODEL)), _full((D_SB, D_MODEL)),
                  _full((8, D_CONV)), _full((D_CONV, D_MODEL)), _full((D_MODEL, D_MODEL)), _full((1, D_MODEL))],
        out_specs=[tok(D_MODEL), tok(D_MODEL), tok(D_MODEL)],
        compiler_params=_cparams(1),
    )(x, y256, o, c3, c3, w["n1"], w["w_gate"], w["b_gate"], w["w_glu_val"], w["w_glu_gate"], w["w_attn_out"],
      w["conv_w"], w["w_conv_out"], w["w_mix_out"], w["n2"])


FF_CHUNK = 1024


def _ffn_fwd(x1, n3, w_up, w_dn, n4):
    s = x1.shape[0]
    tb = min(512, s)
    nh = D_FF // FF_CHUNK

    def body(x_ref, n3_ref, wu_ref, wd_ref, n4_ref, x2_ref, f_ref, h_s, acc):
        j = pl.program_id(1)

        @pl.when(j == 0)
        def _():
            h_s[...] = _rms_fwd(x_ref[...], n3_ref[...]).astype(BF16)
            acc[...] = jnp.zeros_like(acc)

        a = jnp.maximum(_dot(h_s[...], wu_ref[...]), 0.0)
        acc[...] += _dot(a * a, wd_ref[...])

        @pl.when(j == nh - 1)
        def _():
            f = acc[...]
            f_ref[...] = f
            x2_ref[...] = x_ref[...] + _rms_fwd(f, n4_ref[...])

    tok = pl.BlockSpec((tb, D_MODEL), lambda i, j: (i, 0))
    return pl.pallas_call(
        body, name="ffn_fwd", grid=(s // tb, nh),
        out_shape=[jax.ShapeDtypeStruct((s, D_MODEL), F32)] * 2,
        in_specs=[tok, _full((1, D_MODEL)), pl.BlockSpec((D_MODEL, FF_CHUNK), lambda i, j: (0, j)),
                  pl.BlockSpec((FF_CHUNK, D_MODEL), lambda i, j: (j, 0)), _full((1, D_MODEL))],
        out_specs=[tok, tok],
        scratch_shapes=[pltpu.VMEM((tb, D_MODEL), BF16), pltpu.VMEM((tb, D_MODEL), F32)],
        compiler_params=_cparams(2),
    )(x1, n3, w_up, w_dn, n4)


def _loss_head(y, target):
    s = y.shape[0]
    tb = _token_tile(s)

    def body(y_ref, t_ref, dy_ref, l_ref):
        @pl.when(pl.program_id(0) == 0)
        def _():
            l_ref[...] = jnp.zeros_like(l_ref)

        err = y_ref[...] - t_ref[...]
        dy_ref[...] = err * (1.0 / D_MODEL)
        l_ref[...] += 0.5 * jnp.sum(jnp.mean(err * err, axis=-1, keepdims=True), axis=0, keepdims=True)

    tok = pl.BlockSpec((tb, D_MODEL), lambda i: (i, 0))
    return pl.pallas_call(
        body, name="loss_head", grid=(s // tb,),
        out_shape=[jax.ShapeDtypeStruct((s, D_MODEL), F32), jax.ShapeDtypeStruct((8, LANES), F32)],
        in_specs=[tok, tok], out_specs=[tok, _full((8, LANES))],
        compiler_params=_cparams(1),
    )(y, target)


def _ffn_bwd(dx2, x1, f, n3, w_up, w_dn, n4):
    s = x1.shape[0]
    tb = min(512, s)
    nh = D_FF // FF_CHUNK
    nt = s // tb

    def body(dx2_ref, x_ref, f_ref, n3_ref, wu_ref, wd_ref, n4_ref,
             dx1_ref, r_ref, da_ref, h_ref, df_ref, dn3_ref, dn4_ref, acc):
        i = pl.program_id(0)
        j = pl.program_id(1)

        @pl.when((i == 0) & (j == 0))
        def _():
            dn3_ref[...] = jnp.zeros_like(dn3_ref)
            dn4_ref[...] = jnp.zeros_like(dn4_ref)

        @pl.when(j == 0)
        def _():
            h_ref[...] = _rms_fwd(x_ref[...], n3_ref[...]).astype(BF16)
            df, dn4 = _rms_bwd(dx2_ref[...], f_ref[...], n4_ref[...])
            df_ref[...] = df.astype(BF16)
            dn4_ref[...] += dn4
            acc[...] = jnp.zeros_like(acc)

        a = jnp.maximum(_dot(h_ref[...], wu_ref[...]), 0.0)
        r_ref[...] = (a * a).astype(BF16)
        da = (_dot_nt(df_ref[...], wd_ref[...]) * (2.0 * a)).astype(BF16)
        da_ref[...] = da
        acc[...] += _dot_nt(da, wu_ref[...])

        @pl.when(j == nh - 1)
        def _():
            dx, dn3 = _rms_bwd(acc[...], x_ref[...], n3_ref[...])
            dx1_ref[...] = dx2_ref[...] + dx
            dn3_ref[...] += dn3

    tok = pl.BlockSpec((tb, D_MODEL), lambda i, j: (i, 0))
    hid = pl.BlockSpec((tb, FF_CHUNK), lambda i, j: (i, j))
    tok_b = jax.ShapeDtypeStruct((s, D_MODEL), BF16)
    hid_b = jax.ShapeDtypeStruct((s, D_FF), BF16)
    row = jax.ShapeDtypeStruct((1, D_MODEL), F32)
    return pl.pallas_call(
        body, name="ffn_bwd", grid=(nt, nh),
        out_shape=[jax.ShapeDtypeStruct((s, D_MODEL), F32), hid_b, hid_b, tok_b, tok_b, row, row],
        in_specs=[tok, tok, tok, _full((1, D_MODEL)), pl.BlockSpec((D_MODEL, FF_CHUNK), lambda i, j: (0, j)),
                  pl.BlockSpec((FF_CHUNK, D_MODEL), lambda i, j: (j, 0)), _full((1, D_MODEL))],
        out_specs=[tok, hid, hid, tok, tok, _full((1, D_MODEL)), _full((1, D_MODEL))],
        scratch_shapes=[pltpu.VMEM((tb, D_MODEL), F32)],
        compiler_params=_cparams(2),
    )(dx2, x1, f, n3, w_up, w_dn, n4)


def _merge_bwd(dx1, x, m2, y256, o, c3, w):
    s = x.shape[0]
    tb = _token_tile(s)

    def body(dx1_ref, m2_ref, x_ref, y_ref, o_ref, c_ref, ch_ref, n1_ref, wg_ref, bg_ref, wv_ref, wsg_ref, wao_ref,
             cw_ref, wc_ref, wm_ref, n2_ref,
             dhg_ref, h_ref, dgz_ref, dvs_ref, dy_ref, do_ref, dyb_ref, dyc_ref, bcv_ref, dcv_ref, dcb_ref, dm2_ref,
             dbg_ref, dn2_ref):
        @pl.when(pl.program_id(0) == 0)
        def _():
            dbg_ref[...] = jnp.zeros_like(dbg_ref)
            dn2_ref[...] = jnp.zeros_like(dn2_ref)

        dm2, dn2 = _rms_bwd(dx1_ref[...], m2_ref[...], n2_ref[...])
        dn2_ref[...] += dn2
        dm2_ref[...] = dm2.astype(BF16)
        dmg = _dot_nt(dm2, wm_ref[...])
        br = _merge_branches(x_ref, y_ref, o_ref, c_ref, ch_ref, n1_ref, wg_ref, bg_ref, wv_ref, wsg_ref, wao_ref,
                             cw_ref, wc_ref, tb)
        g = br["g"]
        g1, g2, g3 = g[:, 0:1024], g[:, 1024:2048], g[:, 2048:3072]
        val, sg = br["val"], br["sg"]
        h_ref[...] = br["h"].astype(BF16)
        dgz = jnp.concatenate([dmg * (val * sg) * g1 * (1.0 - g1), dmg * br["ysb"] * g2 * (1.0 - g2),
                               dmg * br["yc"] * g3 * (1.0 - g3)], axis=1)
        dbg_ref[...] += jnp.sum(dgz, axis=0, keepdims=True)
        dgz_ref[...] = dgz.astype(BF16)
        dhg_ref[...] = _dot_nt(dgz, wg_ref[...])
        dys = dmg * g1
        dval = dys * sg
        dsg = dys * val * sg * (1.0 - sg)
        dvs_ref[:, 0:1024] = dval.astype(BF16)
        dvs_ref[:, 1024:2048] = dsg.astype(BF16)
        dy_ref[...] = _dot_nt(dval, wv_ref[...]) + _dot_nt(dsg, wsg_ref[...])
        dyb = (dmg * g2).astype(BF16)
        dyb_ref[...] = dyb
        do_ref[...] = _dot_nt(dyb, wao_ref[...]).astype(BF16)
        dyc = (dmg * g3).astype(BF16)
        dyc_ref[...] = dyc
        dcq = _dot_nt(dyc, wc_ref[...])
        bcv_ref[...] = (br["cb"] * br["cv"]).astype(BF16)
        dcb_ref[...] = dcq * br["cv"]
        dcv_ref[...] = dcq * br["cb"]

    def tok(wd):
        return pl.BlockSpec((tb, wd), lambda i: (i, 0))

    def out(wd, dt):
        return jax.ShapeDtypeStruct((s, wd), dt)

    halo = pl.BlockSpec((8, 3 * D_CONV), lambda i: (jnp.maximum(i * (tb // 8) - 1, 0), 0))
    return pl.pallas_call(
        body, name="merge_bwd", grid=(s // tb,),
        out_shape=[out(D_MODEL, F32), out(D_MODEL, BF16), out(3 * D_MODEL, BF16), out(2 * D_MODEL, BF16),
                   out(D_SSM, F32), out(D_SB, BF16), out(D_MODEL, BF16), out(D_MODEL, BF16), out(D_CONV, BF16),
                   out(D_CONV, F32), out(D_CONV, F32), out(D_MODEL, BF16),
                   jax.ShapeDtypeStruct((1, 3 * D_MODEL), F32), jax.ShapeDtypeStruct((1, D_MODEL), F32)],
        in_specs=[tok(D_MODEL), tok(D_MODEL), tok(D_MODEL), tok(D_SSM), tok(D_SB), tok(3 * D_CONV), halo,
                  _full((1, D_MODEL)), _full((D_MODEL, 3 * D_MODEL)), _full((1, 3 * D_MODEL)),
                  _full((D_SSM, D_MODEL)), _full((D_SSM, D_MODEL)), _full((D_SB, D_MODEL)),
                  _full((8, D_CONV)), _full((D_CONV, D_MODEL)), _full((D_MODEL, D_MODEL)), _full((1, D_MODEL))],
        out_specs=[tok(D_MODEL), tok(D_MODEL), tok(3 * D_MODEL), tok(2 * D_MODEL), tok(D_SSM), tok(D_SB),
                   tok(D_MODEL), tok(D_MODEL), tok(D_CONV), tok(D_CONV), tok(D_CONV), tok(D_MODEL),
                   _full((1, 3 * D_MODEL)), _full((1, D_MODEL))],
        compiler_params=_cparams(1),
    )(dx1, m2, x, y256, o, c3, c3, w["n1"], w["w_gate"], w["b_gate"], w["w_glu_val"], w["w_glu_gate"],
      w["w_attn_out"], w["conv_w"], w["w_conv_out"], w["w_mix_out"], w["n2"])


def _ssm_bwd(u, ypre, dy, states, ab_re, ab_im, bb_re, bb_im, cc_re, cc_im, d_skip):
    s = u.shape[0]
    ts = _token_tile(s)
    nt = s // ts

    def body(u_ref, yp_ref, dy_ref, st_ref, ar_ref, ai_ref, bbr_ref, bbi_ref, ccr_ref, cci_ref, d_ref,
             du_ref, dar_ref, dai_ref, dbbr_ref, dbbi_ref, dccr_ref, dcci_ref, dd_ref,
             bur, bui, hr_s, hi_s, pr_s, pi_s, lr_s, li_s, cr, ci):
        @pl.when(pl.program_id(0) == 0)
        def _():
            cr[...] = jnp.zeros_like(cr)
            ci[...] = jnp.zeros_like(ci)
            for ref in (dar_ref, dai_ref, dbbr_ref, dbbi_ref, dccr_ref, dcci_ref, dd_ref):
                ref[...] = jnp.zeros_like(ref)

        u = u_ref[...]
        ub = u.astype(BF16)
        bur[...] = _dot(ub, bbr_ref[...])
        bui[...] = _dot(ub, bbi_ref[...])
        ar = ar_ref[...]
        ai = ai_ref[...]

        def fwd_step(t, carry):
            hr, hi = carry
            pr_s[pl.ds(t, 1), :] = hr
            pi_s[pl.ds(t, 1), :] = hi
            nr = ar * hr - ai * hi + bur[pl.ds(t, 1), :]
            ni = ar * hi + ai * hr + bui[pl.ds(t, 1), :]
            hr_s[pl.ds(t, 1), :] = nr
            hi_s[pl.ds(t, 1), :] = ni
            return nr, ni

        lax.fori_loop(0, ts, fwd_step, (st_ref[0, 0:1, :], st_ref[0, 1:2, :]), unroll=8)

        dyp = dy_ref[...] * _gelu_grad(yp_ref[...])
        dypb = dyp.astype(BF16)
        lr_s[...] = _dot(dypb, ccr_ref[...])
        li_s[...] = -_dot(dypb, cci_ref[...])

        def bwd_step(tt, carry):
            t = ts - 1 - tt
            nr, ni = carry
            qr = lr_s[pl.ds(t, 1), :] + ar * nr + ai * ni
            qi = li_s[pl.ds(t, 1), :] + ar * ni - ai * nr
            lr_s[pl.ds(t, 1), :] = qr
            li_s[pl.ds(t, 1), :] = qi
            return qr, qi

        nr, ni = lax.fori_loop(0, ts, bwd_step, (cr[...], ci[...]), unroll=8)
        cr[...] = nr
        ci[...] = ni
        lam_r = lr_s[...]
        lam_i = li_s[...]
        pr = pr_s[...]
        pi = pi_s[...]
        dar_ref[...] += jnp.sum(lam_r * pr + lam_i * pi, axis=0, keepdims=True)
        dai_ref[...] += jnp.sum(lam_i * pr - lam_r * pi, axis=0, keepdims=True)
        lrb = lam_r.astype(BF16)
        lib = lam_i.astype(BF16)
        du_ref[...] = _dot_nt(lrb, bbr_ref[...]) + _dot_nt(lib, bbi_ref[...]) + d_ref[...] * dyp
        dbbr_ref[...] += _dot_tn(ub, lrb)
        dbbi_ref[...] += _dot_tn(ub, lib)
        dccr_ref[...] += _dot_tn(dypb, hr_s[...])
        dcci_ref[...] -= _dot_tn(dypb, hi_s[...])
        dd_ref[...] += jnp.sum(dyp * u, axis=0, keepdims=True)

    tok = pl.BlockSpec((ts, D_SSM), lambda i: (nt - 1 - i, 0))
    row = _full((1, D_STATE))
    mat = _full((D_SSM, D_STATE))
    row_o = jax.ShapeDtypeStruct((1, D_STATE), F32)
    mat_o = jax.ShapeDtypeStruct((D_SSM, D_STATE), F32)
    return pl.pallas_call(
        body, name="ssm_bwd", grid=(nt,),
        out_shape=[jax.ShapeDtypeStruct((s, D_SSM), F32), row_o, row_o, mat_o, mat_o, mat_o, mat_o,
                   jax.ShapeDtypeStruct((1, D_SSM), F32)],
        in_specs=[tok, tok, tok, pl.BlockSpec((1, 2, D_STATE), lambda i: (nt - 1 - i, 0, 0)),
                  row, row, mat, mat, mat, mat, _full((1, D_SSM))],
        out_specs=[tok, row, row, mat, mat, mat, mat, _full((1, D_SSM))],
        scratch_shapes=[pltpu.VMEM((ts, D_STATE), F32)] * 8 + [pltpu.VMEM((1, D_STATE), F32)] * 2,
        compiler_params=_cparams(1),
    )(u, ypre, dy, states, ab_re, ab_im, bb_re, bb_im, cc_re, cc_im, d_skip)


def _attn_bwd(q, k, v, do, tab):
    s = q.shape[0]
    t = _attn_tile(s)
    nq = s // t
    n_pairs = D_SB // LANES

    def body(q_ref, do_ref, tab_ref, k_ref, v_ref, dq_ref, dk_hbm, dv_hbm, dk_s, dv_s, sems,
             z_s, ls_s, suf_s, dw_s, dl_s, beta_s, pre_s, wb_s, dq_s, q_s, do_s, qt_s, dot_s, tri_s):
        p = pl.program_id(0)
        i = pl.program_id(1)

        @pl.when(i == 0)
        def _():
            dk_s[...] = jnp.zeros_like(dk_s)
            dv_s[...] = jnp.zeros_like(dv_s)

        lane = lax.broadcasted_iota(jnp.int32, (t, LANES), 1)
        first = lane < HEAD_DIM
        row = lax.broadcasted_iota(jnp.int32, (t, t), 0)
        col = lax.broadcasted_iota(jnp.int32, (t, t), 1)
        tri_s[0] = (row > col).astype(BF16)
        tri_s[1] = (row < col).astype(BF16)
        causal = col < row
        q2 = q_ref[...]
        do2 = do_ref[...]
        zero = jnp.zeros_like(q2)
        for a in range(2):
            mine = first if a == 0 else jnp.logical_not(first)
            q_s[a] = jnp.where(mine, q2, zero)
            do_s[a] = jnp.where(mine, do2, zero)
            qt_s[a] = jnp.where(mine, q2, zero).astype(F32).T.astype(BF16)
            dot_s[a] = jnp.where(mine, do2, zero).astype(F32).T.astype(BF16)
        dq_s[...] = jnp.zeros_like(dq_s)

        def tile_rows(j):
            return pl.ds(pl.multiple_of(jnp.maximum(j, 0) * t, t), t)

        def scores(j, b=0):
            k2 = k_ref[tile_rows(j), :]
            for a in range(2):
                z_s[b, a] = _dot_nt(q_s[a], k2)

        def spread(j, valid, b=0):
            v2 = v_ref[tile_rows(j), :]
            for a in range(2):
                ls, l1m = _sb_block(z_s[b, a], valid)
                ls_s[b, a] = ls
                suf_s[b, a] = _dot(l1m, tri_s[0])
                dw_s[b, a] = _dot_nt(do_s[a], v2)

        def weights(j, valid, b=0):
            for a in range(2):
                c1 = jnp.sum(jnp.where(lane == j, tab_ref[:, a * LANES:(a + 1) * LANES], 0.0), axis=1, keepdims=True)
                ls = ls_s[b, a]
                w = jnp.exp2(ls + suf_s[b, a] + c1)
                if valid is not None:
                    w = jnp.where(valid, w, 0.0)
                dl = w * dw_s[b, a]
                dl_s[b, a] = dl
                beta_s[b, a] = jnp.exp2(ls)
                wb_s[b, a] = w.astype(BF16)
                pre_s[b, a] = _dot(dl, tri_s[1])

        def grads(j, c2s, valid, b=0):
            rows = tile_rows(j)
            k2 = k_ref[rows, :]
            out = []
            dkt = dvt = None
            for a in range(2):
                dl = dl_s[b, a]
                pre = pre_s[b, a]
                dz = dl - beta_s[b, a] * (dl + pre + c2s[a])
                if valid is not None:
                    dz = jnp.where(valid, dz, 0.0)
                dzb = dz.astype(BF16)
                dq_s[a] += _dot(dzb, k2)
                dk_a = jnp.dot(qt_s[a], dzb, preferred_element_type=F32)
                dv_a = jnp.dot(dot_s[a], wb_s[b, a], preferred_element_type=F32)
                dkt = dk_a if dkt is None else dkt + dk_a
                dvt = dv_a if dvt is None else dvt + dv_a
                out.append(c2s[a] + pre[:, t - 1:t] + dl[:, t - 1:t])
            dk_s[:, rows] += dkt
            dv_s[:, rows] += dvt
            return tuple(out)

        reach = jnp.max(jnp.maximum(tab_ref[:, 0:LANES], tab_ref[:, LANES:2 * LANES]), axis=0, keepdims=True)
        tile_id = lax.broadcasted_iota(jnp.int32, (1, LANES), 1)
        j_min = jnp.min(jnp.where(reach >= DEAD_LOG2, tile_id, i))
        no_sum = tuple(jnp.zeros((t, 1), F32) for _ in range(2))
        two_tiles = (i >= 1) & (j_min >= i - 1)

        @pl.when(two_tiles)
        def _():
            scores(i - 1, 0)
            scores(i, 1)
            spread(i - 1, None, 0)
            spread(i, causal, 1)
            weights(i - 1, None, 0)
            weights(i, causal, 1)
            grads(i, grads(i - 1, no_sum, None, 0), causal, 1)

        @pl.when(jnp.logical_not(two_tiles))
        def _():
            z_s[0] = jnp.full((2, t, t), -1e30, F32)
            ls_s[0] = jnp.full((2, t, t), -1e30, F32)
            for ref in (suf_s, dw_s, dl_s, beta_s, pre_s, wb_s):
                ref[0] = jnp.zeros((2, t, t), ref.dtype)

            def trip(n, c2s):
                c2s = grads(n - 3, c2s, None)
                weights(n - 2, None)
                spread(n - 1, None)
                scores(n)
                return c2s

            c2s = lax.fori_loop(j_min, i + 1, trip, no_sum)
            c2s = grads(i - 2, c2s, None)
            weights(i - 1, None)
            spread(i, causal)
            c2s = grads(i - 1, c2s, None)
            weights(i, causal)
            grads(i, c2s, causal)

        dq_ref[...] = Q_SCALE * jnp.where(first, dq_s[0], dq_s[1])

        @pl.when(i == nq - 1)
        def _():
            dk_s[...] = dk_s[...] * (1.0 / LOG2E)
            ck = pltpu.make_async_copy(dk_s, dk_hbm.at[p], sems.at[0])
            cv = pltpu.make_async_copy(dv_s, dv_hbm.at[p], sems.at[1])
            ck.start()
            cv.start()
            ck.wait()
            cv.wait()

    blk = pl.BlockSpec((t, LANES), lambda p, i: (i, p))
    seq = pl.BlockSpec((s, LANES), lambda p, i: (0, p))
    pairs = jax.ShapeDtypeStruct((n_pairs, LANES, s), F32)
    stage = pltpu.VMEM((2, 2, t, t), F32)
    return pl.pallas_call(
        body, name="attn_bwd", grid=(n_pairs, nq),
        out_shape=[jax.ShapeDtypeStruct((s, D_SB), F32), pairs, pairs],
        in_specs=[blk, blk, pl.BlockSpec((t, 2 * LANES), lambda p, i: (i, p)), seq, seq],
        out_specs=[blk, pl.BlockSpec(memory_space=pl.ANY), pl.BlockSpec(memory_space=pl.ANY)],
        scratch_shapes=[pltpu.VMEM((LANES, s), F32), pltpu.VMEM((LANES, s), F32), pltpu.SemaphoreType.DMA((2,))]
        + [stage] * 7 + [pltpu.VMEM((2, 2, t, t), BF16), pltpu.VMEM((2, t, LANES), F32),
                         pltpu.VMEM((2, t, LANES), BF16), pltpu.VMEM((2, t, LANES), BF16),
                         pltpu.VMEM((2, LANES, t), BF16), pltpu.VMEM((2, LANES, t), BF16),
                         pltpu.VMEM((2, t, t), BF16)],
        compiler_params=_cparams(2),
    )(q, do, tab, k, v)


def _pre_bwd(dres, dhg, x, du, dq, dk, dv, dcb, dcv, c3, n1, w_in, conv_w):
    s = x.shape[0]
    tb = _token_tile(s)
    nt = s // tb

    def body(dres_ref, dhg_ref, x_ref, du_ref, dq_ref, dk_ref, dv_ref, dcb_ref, dcv_ref, dnext_ref, c_ref, ch_ref,
             n1_ref, w_ref, cw_ref, dx_ref, dp_ref, dn1_ref, dcw_ref):
        i = pl.program_id(0)

        @pl.when(i == 0)
        def _():
            dn1_ref[...] = jnp.zeros_like(dn1_ref)
            dcw_ref[...] = jnp.zeros_like(dcw_ref)

        c3 = c_ref[...]
        cc, cx = c3[:, 256:512], c3[:, 512:768]
        ch = ch_ref[...]
        z = cc * cx
        zh = jnp.where(i > 0, ch[:, 256:512] * ch[:, 512:768], 0.0)
        rowi = lax.broadcasted_iota(jnp.int32, (tb, D_CONV), 0)
        z1, z2 = _conv_taps(z, zh, rowi)
        dcv = dcv_ref[...]
        nxt = jnp.where(i < nt - 1, dnext_ref[...], 0.0)
        d1 = jnp.where(rowi == tb - 1, nxt[0:1, :], pltpu.roll(dcv, tb - 1, 0))
        d2 = jnp.where(rowi == tb - 2, nxt[0:1, :], jnp.where(rowi == tb - 1, nxt[1:2, :], pltpu.roll(dcv, tb - 2, 0)))
        cw = cw_ref[...]
        dz = cw[2:3, :] * dcv + cw[1:2, :] * d1 + cw[0:1, :] * d2
        dcw_ref[0:1, :] += jnp.sum(dcv * z2, axis=0, keepdims=True)
        dcw_ref[1:2, :] += jnp.sum(dcv * z1, axis=0, keepdims=True)
        dcw_ref[2:3, :] += jnp.sum(dcv * z, axis=0, keepdims=True)
        dp_ref[:, 0:256] = du_ref[...].astype(BF16)
        dp_ref[:, 256:768] = dq_ref[...].astype(BF16)
        dp_ref[:, 768:1280] = dk_ref[...].astype(BF16)
        dp_ref[:, 1280:1792] = dv_ref[...].astype(BF16)
        dp_ref[:, 1792:2048] = dcb_ref[...].astype(BF16)
        dp_ref[:, 2048:2304] = (dz * cx).astype(BF16)
        dp_ref[:, 2304:2560] = (dz * cc).astype(BF16)
        dh = dhg_ref[...] + _dot_nt(dp_ref[...], w_ref[...])
        dx, dn1 = _rms_bwd(dh, x_ref[...], n1_ref[...])
        dx_ref[...] = dres_ref[...] + dx
        dn1_ref[...] += dn1

    def tok(wd):
        return pl.BlockSpec((tb, wd), lambda i: (i, 0))

    halo_prev = pl.BlockSpec((8, 3 * D_CONV), lambda i: (jnp.maximum(i * (tb // 8) - 1, 0), 0))
    halo_next = pl.BlockSpec((8, D_CONV), lambda i: (jnp.minimum((i + 1) * (tb // 8), s // 8 - 1), 0))
    return pl.pallas_call(
        body, name="pre_bwd", grid=(nt,),
        out_shape=[jax.ShapeDtypeStruct((s, D_MODEL), F32), jax.ShapeDtypeStruct((s, D_IN), BF16),
                   jax.ShapeDtypeStruct((1, D_MODEL), F32), jax.ShapeDtypeStruct((8, D_CONV), F32)],
        in_specs=[tok(D_MODEL), tok(D_MODEL), tok(D_MODEL), tok(D_SSM), tok(D_SB), tok(D_SB), tok(D_SB),
                  tok(D_CONV), tok(D_CONV), halo_next, tok(3 * D_CONV), halo_prev,
                  _full((1, D_MODEL)), _full((D_MODEL, D_IN)), _full((8, D_CONV))],
        out_specs=[tok(D_MODEL), tok(D_IN), _full((1, D_MODEL)), _full((8, D_CONV))],
        compiler_params=_cparams(1),
    )(dres, dhg, x, du, dq, dk, dv, dcb, dcv, dcv, c3, c3, n1, w_in, conv_w)


def _padded(n):
    return -(-n // PIECE_ALIGN) * PIECE_ALIGN


def _pack(pieces, total_rows=None):
    flat = []
    for a in pieces:
        a = a.reshape(-1)
        flat.append(jnp.pad(a, (0, _padded(a.size) - a.size)))
    out = jnp.concatenate(flat)
    if total_rows is not None:
        out = jnp.pad(out, (0, total_rows * LANES - out.size))
    return out.reshape(-1, LANES)


def _unpack(flat, shapes, lead=()):
    out, row = [], 0
    for shp in shapes:
        n = math.prod(shp)
        rows = _padded(n) // LANES
        piece = flat[..., row:row + rows, :].reshape(lead + (-1,))
        out.append(piece[..., :n].reshape(lead + tuple(shp)))
        row += rows
    return out


def _to_shards(full, axis):
    shp = full.shape
    cut = shp[:axis] + (N_DEV, shp[axis] // N_DEV) + shp[axis + 1:]
    return jnp.moveaxis(full.reshape(cut), axis, 0)


def _from_shards(shards, axis):
    moved = jnp.moveaxis(shards, 0, axis)
    shp = moved.shape
    return moved.reshape(shp[:axis] + (shp[axis] * shp[axis + 1],) + shp[axis + 2:])


def _expand_groups(compact):
    rows = lax.broadcasted_iota(jnp.int32, (D_SSM, D_STATE), 0) // GROUP_CH
    cols = lax.broadcasted_iota(jnp.int32, (D_SSM, D_STATE), 1) // N_STATE
    return jnp.where(rows == cols, jnp.tile(compact, (N_GROUPS, 1)), 0.0)


def _collect_groups(dense):
    rows = lax.broadcasted_iota(jnp.int32, (D_SSM, D_STATE), 0) // GROUP_CH
    cols = lax.broadcasted_iota(jnp.int32, (D_SSM, D_STATE), 1) // N_STATE
    return jnp.where(rows == cols, dense, 0.0).reshape(N_GROUPS, GROUP_CH, D_STATE).sum(axis=0)


def kernel(x, norm_mix_pre, norm_mix_post, w_in, w_gate, b_gate, ssm_a_re, ssm_a_im, ssm_log_dt, ssm_b_re, ssm_b_im, ssm_c_re, ssm_c_im, ssm_d, w_glu_val, w_glu_gate, w_attn_out, conv_w, w_conv_out, w_mix_out, norm_ffn_pre, norm_ffn_post, w_ffn_up, w_ffn_down, loss_target, m_norm_mix_pre, m_norm_mix_post, m_w_in, m_w_gate, m_b_gate, m_ssm_a_re, m_ssm_a_im, m_ssm_log_dt, m_ssm_b_re, m_ssm_b_im, m_ssm_c_re, m_ssm_c_im, m_ssm_d, m_w_glu_val, m_w_glu_gate, m_w_attn_out, m_conv_w, m_w_conv_out, m_w_mix_out, m_norm_ffn_pre, m_norm_ffn_post, m_w_ffn_up, m_w_ffn_down, v_norm_mix_pre, v_norm_mix_post, v_w_in, v_w_gate, v_b_gate, v_ssm_a_re, v_ssm_a_im, v_ssm_log_dt, v_ssm_b_re, v_ssm_b_im, v_ssm_c_re, v_ssm_c_im, v_ssm_d, v_w_glu_val, v_w_glu_gate, v_w_attn_out, v_conv_w, v_w_conv_out, v_w_mix_out, v_norm_ffn_pre, v_norm_ffn_post, v_w_ffn_up, v_w_ffn_down):
    args = dict(locals())
    wts = {n: args[n] for n in WEIGHTS}
    mom = {n: args["m_" + n] for n in WEIGHTS}
    vel = {n: args["v_" + n] for n in WEIGHTS}
    seq = x.shape[1]
    x0 = x.reshape(seq, D_MODEL)
    target = loss_target.reshape(seq, D_MODEL)

    pieces = []
    for n in SHARDED:
        if n == "conv_w":
            hi = wts[n].astype(BF16)
            pieces += [hi, (wts[n] - hi.astype(F32)).astype(BF16)]
        else:
            pieces.append(wts[n].astype(BF16))
    shard_shapes = [p.shape for p in pieces]
    gathered = _unpack(_all_gather(_pack(pieces)), shard_shapes, lead=(N_DEV,))
    full = {}
    it = iter(gathered)
    for n in SHARDED:
        if n == "conv_w":
            full[n] = _from_shards(next(it).astype(F32) + next(it).astype(F32), SHARD_AXIS[n])
        else:
            full[n] = _from_shards(next(it), SHARD_AXIS[n])

    def layer_weights(l):
        return dict(
            n1=norm_mix_pre[l][None], n2=norm_mix_post[l][None], n3=norm_ffn_pre[l][None], n4=norm_ffn_post[l][None],
            w_in=full["w_in"][l], w_gate=full["w_gate"][l], b_gate=b_gate[l][None],
            w_glu_val=full["w_glu_val"][l], w_glu_gate=full["w_glu_gate"][l], w_attn_out=full["w_attn_out"][l],
            conv_w=jnp.pad(full["conv_w"][l], ((0, 5), (0, 0))), w_conv_out=full["w_conv_out"][l],
            w_mix_out=full["w_mix_out"][l], w_ffn_up=full["w_ffn_up"][l], w_ffn_down=full["w_ffn_down"][l],
            lr=ssm_a_re[l].reshape(1, D_STATE), li=ssm_a_im[l].reshape(1, D_STATE),
            ldt=jnp.repeat(ssm_log_dt[l], N_STATE).reshape(1, D_STATE),
            br=jnp.transpose(ssm_b_re[l], (2, 0, 1)).reshape(GROUP_CH, D_STATE),
            bi=jnp.transpose(ssm_b_im[l], (2, 0, 1)).reshape(GROUP_CH, D_STATE),
            cc_re=_expand_groups(jnp.transpose(ssm_c_re[l], (1, 0, 2)).reshape(GROUP_CH, D_STATE)).astype(BF16),
            cc_im=_expand_groups(jnp.transpose(ssm_c_im[l], (1, 0, 2)).reshape(GROUP_CH, D_STATE)).astype(BF16),
            d_skip=ssm_d[l][None],
        )

    saved = []
    xin = x0
    for l in range(DEPTH):
        w = layer_weights(l)
        ab_re, ab_im, bbr, bbi = _ssm_params_fwd(w["lr"], w["li"], w["ldt"], w["br"], w["bi"])
        w.update(ab_re=ab_re, ab_im=ab_im, bb_re=_expand_groups(bbr).astype(BF16), bb_im=_expand_groups(bbi).astype(BF16))
        u, q, k, v, c3 = _pre_fwd(xin, w["n1"], w["w_in"])
        y256, ypre, states = _ssm_fwd(u, ab_re, ab_im, w["bb_re"], w["bb_im"], w["cc_re"], w["cc_im"], w["d_skip"])
        o, tab = _attn_fwd(q, k, v)
        x1, m2, merged = _merge_fwd(xin, y256, o, c3, w)
        x2, f = _ffn_fwd(x1, w["n3"], w["w_ffn_up"], w["w_ffn_down"], w["n4"])
        saved.append(dict(w=w, x0=xin, u=u, q=q, k=k, v=v, c3=c3, y256=y256, ypre=ypre, states=states, o=o, tab=tab, x1=x1,
                          m2=m2, merged=merged, f=f))
        xin = x2

    dxo, loss_part = _loss_head(xin, target)
    loss = lax.psum(loss_part[0, 0], ("x", "y", "c"))

    grads = {n: [None] * DEPTH for n in WEIGHTS}
    for l in reversed(range(DEPTH)):
        sv = saved[l]
        w = sv["w"]
        dx1, r, da, h2, df, dn3, dn4 = _ffn_bwd(dxo, sv["x1"], sv["f"], w["n3"], w["w_ffn_up"], w["w_ffn_down"], w["n4"])
        grads["w_ffn_down"][l] = _tn_matmul(r, df, "grad_w_ffn_down")
        grads["w_ffn_up"][l] = _tn_matmul(h2, da, "grad_w_ffn_up")
        grads["norm_ffn_pre"][l] = dn3[0]
        grads["norm_ffn_post"][l] = dn4[0]
        (dhg, hb, dgz, dvs, dy256, do, dyb, dyc, bcv, dcv, dcb, dm2, dbg, dn2) = _merge_bwd(
            dx1, sv["x0"], sv["m2"], sv["y256"], sv["o"], sv["c3"], w)
        grads["w_mix_out"][l] = _tn_matmul(sv["merged"], dm2, "grad_w_mix_out")
        grads["w_gate"][l] = _tn_matmul(hb, dgz, "grad_w_gate")
        dglu = _tn_matmul(sv["y256"], dvs, "grad_w_glu")
        grads["w_glu_val"][l] = dglu[:, :D_MODEL]
        grads["w_glu_gate"][l] = dglu[:, D_MODEL:]
        grads["w_attn_out"][l] = _tn_matmul(sv["o"], dyb, "grad_w_attn_out")
        grads["w_conv_out"][l] = _tn_matmul(bcv, dyc, "grad_w_conv_out")
        grads["b_gate"][l] = dbg[0]
        grads["norm_mix_post"][l] = dn2[0]
        du, dar, dai, dbbr, dbbi, dccr, dcci, dd = _ssm_bwd(
            sv["u"], sv["ypre"], dy256, sv["states"], w["ab_re"], w["ab_im"], w["bb_re"], w["bb_im"],
            w["cc_re"], w["cc_im"], w["d_skip"])
        glr, gli, gdt, gbr, gbi = _ssm_params_bwd(w["lr"], w["li"], w["ldt"], w["br"], w["bi"], dar, dai,
                                                   _collect_groups(dbbr), _collect_groups(dbbi))
        grads["ssm_a_re"][l] = glr.reshape(N_GROUPS, N_STATE)
        grads["ssm_a_im"][l] = gli.reshape(N_GROUPS, N_STATE)
        grads["ssm_log_dt"][l] = gdt[:, 0]
        grads["ssm_b_re"][l] = jnp.transpose(gbr.reshape(GROUP_CH, N_GROUPS, N_STATE), (1, 2, 0))
        grads["ssm_b_im"][l] = jnp.transpose(gbi.reshape(GROUP_CH, N_GROUPS, N_STATE), (1, 2, 0))
        grads["ssm_c_re"][l] = jnp.transpose(_collect_groups(dccr).reshape(GROUP_CH, N_GROUPS, N_STATE), (1, 0, 2))
        grads["ssm_c_im"][l] = jnp.transpose(_collect_groups(dcci).reshape(GROUP_CH, N_GROUPS, N_STATE), (1, 0, 2))
        grads["ssm_d"][l] = dd[0]
        dq, dk4, dv4 = _attn_bwd(sv["q"], sv["k"], sv["v"], do, sv["tab"])
        dk = jnp.transpose(dk4, (2, 0, 1)).reshape(seq, D_SB)
        dv = jnp.transpose(dv4, (2, 0, 1)).reshape(seq, D_SB)
        dxo, dp, dn1, dcw = _pre_bwd(dx1, dhg, sv["x0"], du, dq, dk, dv, dcb, dcv, sv["c3"], w["n1"], w["w_in"], w["conv_w"])
        grads["w_in"][l] = _tn_matmul(hb, dp, "grad_w_in")
        grads["norm_mix_pre"][l] = dn1[0]
        grads["conv_w"][l] = dcw[0:3]
    grad_x = dxo.reshape(x.shape)
    gfull = {n: jnp.stack(grads[n]) for n in WEIGHTS}

    rep_parts = _pack([gfull[n] for n in REPLICATED])
    sh_parts = jnp.stack([_pack([_to_shards(gfull[n], SHARD_AXIS[n])[d] for n in SHARDED]) for d in range(N_DEV)])
    rows = sh_parts.shape[1] + rep_parts.shape[0]
    total_rows = -(-rows // 512) * 512
    parts = jnp.concatenate([sh_parts, jnp.broadcast_to(rep_parts, (N_DEV,) + rep_parts.shape),
                             jnp.zeros((N_DEV, total_rows - rows, LANES), F32)], axis=1)
    recv = _all_to_all(parts)

    def mine(d):
        sh = _pack([d[n] for n in SHARDED])
        rp = _pack([d[n] for n in REPLICATED])
        return jnp.concatenate([sh, rp, jnp.zeros((total_rows - rows, LANES), F32)], axis=0)

    flat_out = _adamw(recv, mine(wts), mine(mom), mine(vel))
    n_sh_rows = sh_parts.shape[1]
    results = []
    for fo in flat_out:
        sh = _unpack(fo[:n_sh_rows], [wts[n].shape for n in SHARDED])
        rp = _unpack(fo[n_sh_rows:rows], [wts[n].shape for n in REPLICATED])
        by_name = dict(zip(SHARDED, sh))
        by_name.update(zip(REPLICATED, rp))
        results.append([by_name[n] for n in WEIGHTS])
    g_out, d_out, m_out, v_out = results
    return (loss, grad_x, *g_out, *d_out, *m_out, *v_out)
```

```python
import functools
import math

import jax
import jax.numpy as jnp
from jax import lax
from jax.experimental import pallas as pl
from jax.experimental.pallas import tpu as pltpu

F32 = jnp.float32
BF16 = jnp.bfloat16

N_DEV = 8
D_MODEL = 1024
DEPTH = 2
D_SSM = 256
N_GROUPS = 16
GROUP_CH = 16
N_STATE = 64
D_STATE = N_GROUPS * N_STATE
D_SB = 512
HEAD_DIM = 64
D_CONV = 256
D_IN = 2560
D_FF = 4096
EPS = 1e-6
Q_SCALE = HEAD_DIM ** -0.5
LOG2E = math.log2(math.e)
DEAD_LOG2 = -160.0

ADAM_LR = 0.001
ADAM_B1 = 0.9
ADAM_B2 = 0.999
ADAM_EPS = 1e-08
ADAM_WD = 0.01
ADAM_STEP = 10

LANES = 128
PIECE_ALIGN = 2048
VMEM_LIMIT = 56 * 1024 * 1024

WEIGHTS = ['norm_mix_pre', 'norm_mix_post', 'w_in', 'w_gate', 'b_gate', 'ssm_a_re', 'ssm_a_im', 'ssm_log_dt',
           'ssm_b_re', 'ssm_b_im', 'ssm_c_re', 'ssm_c_im', 'ssm_d', 'w_glu_val', 'w_glu_gate', 'w_attn_out',
           'conv_w', 'w_conv_out', 'w_mix_out', 'norm_ffn_pre', 'norm_ffn_post', 'w_ffn_up', 'w_ffn_down']
SHARD_AXIS = {'w_in': 2, 'w_gate': 2, 'w_glu_val': 2, 'w_glu_gate': 2, 'w_attn_out': 2, 'conv_w': 2,
              'w_conv_out': 2, 'w_mix_out': 1, 'w_ffn_up': 2, 'w_ffn_down': 1}
SHARDED = [n for n in WEIGHTS if n in SHARD_AXIS]
REPLICATED = [n for n in WEIGHTS if n not in SHARD_AXIS]


def _cparams(n_grid):
    return pltpu.CompilerParams(dimension_semantics=("arbitrary",) * n_grid, vmem_limit_bytes=VMEM_LIMIT)


def _dot(a, b):
    return jnp.dot(a.astype(BF16), b.astype(BF16), preferred_element_type=F32)


def _dot_nt(a, b):
    return lax.dot_general(a.astype(BF16), b.astype(BF16), (((1,), (1,)), ((), ())), preferred_element_type=F32)


def _dot_tn(a, b):
    return lax.dot_general(a.astype(BF16), b.astype(BF16), (((0,), (0,)), ((), ())), preferred_element_type=F32)


def _split_dot(a, b):
    hi = a.astype(BF16)
    lo = (a - hi.astype(F32)).astype(BF16)
    return jnp.dot(hi, b, preferred_element_type=F32) + jnp.dot(lo, b, preferred_element_type=F32)


def _rms_fwd(x, g):
    r = lax.rsqrt(jnp.mean(x * x, axis=-1, keepdims=True) + EPS)
    return x * r * g


def _rms_bwd(dy, x, g):
    r = lax.rsqrt(jnp.mean(x * x, axis=-1, keepdims=True) + EPS)
    dyg = dy * g
    dx = r * dyg - x * (r * r * r) * jnp.mean(dyg * x, axis=-1, keepdims=True)
    dg = jnp.sum(dy * x * r, axis=0, keepdims=True)
    return dx, dg


def _sigmoid(x):
    return 1.0 / (1.0 + jnp.exp(-x))


_GELU_C = math.sqrt(2.0 / math.pi)


def _gelu(y):
    return 0.5 * y * (1.0 + jnp.tanh(_GELU_C * (y + 0.044715 * y * y * y)))


def _gelu_grad(y):
    t = jnp.tanh(_GELU_C * (y + 0.044715 * y * y * y))
    return 0.5 * (1.0 + t) + 0.5 * y * (1.0 - t * t) * _GELU_C * (1.0 + 3.0 * 0.044715 * y * y)


def _full(shape):
    return pl.BlockSpec(shape, lambda *_: (0,) * len(shape))


def _peer(x, y, c, k):
    px = 1 - x if (k >> 2) & 1 else x
    py = 1 - y if (k >> 1) & 1 else y
    pc = 1 - c if k & 1 else c
    return px, py, pc


def _all_gather(shard):
    rows, lanes = shard.shape

    def body(src_ref, out_ref, send_sems, recv_sems, local_sem):
        x, y, c = lax.axis_index("x"), lax.axis_index("y"), lax.axis_index("c")
        me = 4 * x + 2 * y + c
        mine = pltpu.make_async_copy(src_ref, out_ref.at[me], local_sem)
        mine.start()
        copies = []
        for k in range(1, N_DEV):
            cp = pltpu.make_async_remote_copy(
                src_ref=src_ref, dst_ref=out_ref.at[me],
                send_sem=send_sems.at[k - 1], recv_sem=recv_sems.at[k - 1],
                device_id=_peer(x, y, c, k), device_id_type=pl.DeviceIdType.MESH)
            cp.start()
            copies.append(cp)
        for cp in copies:
            cp.wait()
        mine.wait()

    return pl.pallas_call(
        body, name="weights_all_gather",
        out_shape=jax.ShapeDtypeStruct((N_DEV, rows, lanes), shard.dtype),
        in_specs=[pl.BlockSpec(memory_space=pl.ANY)],
        out_specs=pl.BlockSpec(memory_space=pl.ANY),
        scratch_shapes=[pltpu.SemaphoreType.DMA((N_DEV - 1,)), pltpu.SemaphoreType.DMA((N_DEV - 1,)),
                        pltpu.SemaphoreType.DMA],
    )(shard)


def _all_to_all(parts):
    n, rows, lanes = parts.shape

    def body(src_ref, out_ref, send_sems, recv_sems, local_sem):
        x, y, c = lax.axis_index("x"), lax.axis_index("y"), lax.axis_index("c")
        me = 4 * x + 2 * y + c
        mine = pltpu.make_async_copy(src_ref.at[me], out_ref.at[me], local_sem)
        mine.start()
        copies = []
        for k in range(1, N_DEV):
            px, py, pc = _peer(x, y, c, k)
            cp = pltpu.make_async_remote_copy(
                src_ref=src_ref.at[4 * px + 2 * py + pc], dst_ref=out_ref.at[me],
                send_sem=send_sems.at[k - 1], recv_sem=recv_sems.at[k - 1],
                device_id=(px, py, pc), device_id_type=pl.DeviceIdType.MESH)
            cp.start()
            copies.append(cp)
        for cp in copies:
            cp.wait()
        mine.wait()

    return pl.pallas_call(
        body, name="grads_all_to_all",
        out_shape=jax.ShapeDtypeStruct((N_DEV, rows, lanes), parts.dtype),
        in_specs=[pl.BlockSpec(memory_space=pl.ANY)],
        out_specs=pl.BlockSpec(memory_space=pl.ANY),
        scratch_shapes=[pltpu.SemaphoreType.DMA((N_DEV - 1,)), pltpu.SemaphoreType.DMA((N_DEV - 1,)),
                        pltpu.SemaphoreType.DMA],
    )(parts)


def _adamw(recv, w, m, v):
    rows = w.shape[0]
    n_src = recv.shape[0]
    tr = 512
    assert rows % tr == 0
    c1 = 1.0 / (1.0 - ADAM_B1 ** ADAM_STEP)
    c2 = 1.0 / (1.0 - ADAM_B2 ** ADAM_STEP)

    def body(r_ref, w_ref, m_ref, v_ref, g_ref, d_ref, mo_ref, vo_ref):
        g = r_ref[0]
        for s in range(1, n_src):
            g = g + r_ref[s]
        mn = ADAM_B1 * m_ref[...] + (1.0 - ADAM_B1) * g
        vn = ADAM_B2 * v_ref[...] + (1.0 - ADAM_B2) * (g * g)
        upd = (mn * c1) / (jnp.sqrt(vn * c2) + ADAM_EPS) + ADAM_WD * w_ref[...]
        g_ref[...] = g
        d_ref[...] = -ADAM_LR * upd
        mo_ref[...] = mn
        vo_ref[...] = vn

    spec = pl.BlockSpec((tr, LANES), lambda i: (i, 0))
    return pl.pallas_call(
        body, name="adamw", grid=(rows // tr,),
        out_shape=[jax.ShapeDtypeStruct((rows, LANES), F32)] * 4,
        in_specs=[pl.BlockSpec((n_src, tr, LANES), lambda i: (0, i, 0)), spec, spec, spec],
        out_specs=[spec] * 4,
        compiler_params=_cparams(1),
    )(recv, w, m, v)


def _tn_matmul(a, b, name):
    s, m = a.shape
    n = b.shape[1]
    tk = min(1024, s)
    tm = min(1024, m)
    tn = max(c for c in range(LANES, 1280 + 1, LANES) if n % c == 0)
    assert m % tm == 0 and s % tk == 0
    nk = s // tk

    def body(a_ref, b_ref, o_ref):
        k = pl.program_id(2)

        @pl.when(k == 0)
        def _():
            o_ref[...] = jnp.zeros_like(o_ref)

        o_ref[...] += _dot_tn(a_ref[...], b_ref[...])

    return pl.pallas_call(
        body, name=name, grid=(m // tm, n // tn, nk),
        out_shape=jax.ShapeDtypeStruct((m, n), F32),
        in_specs=[pl.BlockSpec((tk, tm), lambda i, j, k: (k, i)), pl.BlockSpec((tk, tn), lambda i, j, k: (k, j))],
        out_specs=pl.BlockSpec((tm, tn), lambda i, j, k: (i, j)),
        compiler_params=_cparams(3),
    )(a, b)


def _token_tile(s):
    return min(256, s)


def _pre_fwd(x, n1, w_in):
    s = x.shape[0]
    tb = _token_tile(s)

    def body(x_ref, n_ref, w_ref, u_ref, q_ref, k_ref, v_ref, c_ref):
        h = _rms_fwd(x_ref[...], n_ref[...])
        p = _dot(h, w_ref[...])
        u_ref[...] = p[:, 0:256]
        q_ref[...] = (p[:, 256:768] * (Q_SCALE * LOG2E)).astype(BF16)
        k_ref[...] = p[:, 768:1280].astype(BF16)
        v_ref[...] = p[:, 1280:1792].astype(BF16)
        c_ref[...] = p[:, 1792:2560]

    def tok(w):
        return pl.BlockSpec((tb, w), lambda i: (i, 0))

    return pl.pallas_call(
        body, name="pre_fwd", grid=(s // tb,),
        out_shape=[jax.ShapeDtypeStruct((s, D_SSM), F32), jax.ShapeDtypeStruct((s, D_SB), BF16),
                   jax.ShapeDtypeStruct((s, D_SB), BF16), jax.ShapeDtypeStruct((s, D_SB), BF16),
                   jax.ShapeDtypeStruct((s, 3 * D_CONV), F32)],
        in_specs=[tok(D_MODEL), _full((1, D_MODEL)), _full((D_MODEL, D_IN))],
        out_specs=[tok(D_SSM), tok(D_SB), tok(D_SB), tok(D_SB), tok(3 * D_CONV)],
        compiler_params=_cparams(1),
    )(x, n1, w_in)


def _ssm_discretize(lr, li, ldt, br, bi):
    dt = jnp.exp(ldt)
    mag = jnp.exp(lr * dt)
    ab_re = mag * jnp.cos(li * dt)
    ab_im = mag * jnp.sin(li * dt)
    den = lr * lr + li * li
    xr = ab_re - 1.0
    coef_re = (xr * lr + ab_im * li) / den
    coef_im = (ab_im * lr - xr * li) / den
    bb_re = coef_re * br - coef_im * bi
    bb_im = coef_re * bi + coef_im * br
    return ab_re, ab_im, bb_re, bb_im


def _ssm_params_fwd(lr, li, ldt, br, bi):
    def body(lr_ref, li_ref, ldt_ref, br_ref, bi_ref, ar_ref, ai_ref, bbr_ref, bbi_ref):
        ar, ai, bbr, bbi = _ssm_discretize(lr_ref[...], li_ref[...], ldt_ref[...], br_ref[...], bi_ref[...])
        ar_ref[...] = ar
        ai_ref[...] = ai
        bbr_ref[...] = bbr
        bbi_ref[...] = bbi

    row = jax.ShapeDtypeStruct((1, D_STATE), F32)
    mat = jax.ShapeDtypeStruct((GROUP_CH, D_STATE), F32)
    return pl.pallas_call(body, name="ssm_params_fwd", out_shape=[row, row, mat, mat])(lr, li, ldt, br, bi)


def _ssm_params_bwd(lr, li, ldt, br, bi, d_ar, d_ai, d_bbr, d_bbi):
    def body(lr_ref, li_ref, ldt_ref, br_ref, bi_ref, dar_ref, dai_ref, dbbr_ref, dbbi_ref,
             glr_ref, gli_ref, gdt_ref, gbr_ref, gbi_ref):
        _, vjp = jax.vjp(_ssm_discretize, lr_ref[...], li_ref[...], ldt_ref[...], br_ref[...], bi_ref[...])
        glr, gli, gdt, gbr, gbi = vjp((dar_ref[...], dai_ref[...], dbbr_ref[...], dbbi_ref[...]))
        glr_ref[...] = glr
        gli_ref[...] = gli
        gbr_ref[...] = gbr
        gbi_ref[...] = gbi
        grp = lax.broadcasted_iota(jnp.int32, (N_GROUPS, D_STATE), 0)
        col = lax.broadcasted_iota(jnp.int32, (N_GROUPS, D_STATE), 1)
        own = (col // N_STATE) == grp
        per_group = jnp.sum(jnp.where(own, jnp.broadcast_to(gdt, (N_GROUPS, D_STATE)), 0.0), axis=1, keepdims=True)
        gdt_ref[...] = jnp.broadcast_to(per_group, (N_GROUPS, LANES))

    row = jax.ShapeDtypeStruct((1, D_STATE), F32)
    mat = jax.ShapeDtypeStruct((GROUP_CH, D_STATE), F32)
    return pl.pallas_call(
        body, name="ssm_params_bwd",
        out_shape=[row, row, jax.ShapeDtypeStruct((N_GROUPS, LANES), F32), mat, mat],
    )(lr, li, ldt, br, bi, d_ar, d_ai, d_bbr, d_bbi)


def _ssm_fwd(u, ab_re, ab_im, bb_re, bb_im, cc_re, cc_im, d_skip):
    s = u.shape[0]
    ts = _token_tile(s)
    nt = s // ts

    def body(u_ref, ar_ref, ai_ref, bbr_ref, bbi_ref, ccr_ref, cci_ref, d_ref, y_ref, ypre_ref, st_ref,
             bur, bui, hr_s, hi_s, cr, ci):
        @pl.when(pl.program_id(0) == 0)
        def _():
            cr[...] = jnp.zeros_like(cr)
            ci[...] = jnp.zeros_like(ci)

        u = u_ref[...]
        bur[...] = _dot(u, bbr_ref[...])
        bui[...] = _dot(u, bbi_ref[...])
        st_ref[0, 0:1, :] = cr[...]
        st_ref[0, 1:2, :] = ci[...]
        ar = ar_ref[...]
        ai = ai_ref[...]

        def step(t, carry):
            hr, hi = carry
            nr = ar * hr - ai * hi + bur[pl.ds(t, 1), :]
            ni = ar * hi + ai * hr + bui[pl.ds(t, 1), :]
            hr_s[pl.ds(t, 1), :] = nr
            hi_s[pl.ds(t, 1), :] = ni
            return nr, ni

        hr, hi = lax.fori_loop(0, ts, step, (cr[...], ci[...]), unroll=8)
        cr[...] = hr
        ci[...] = hi
        y = _dot_nt(hr_s[...], ccr_ref[...]) - _dot_nt(hi_s[...], cci_ref[...]) + d_ref[...] * u
        ypre_ref[...] = y
        y_ref[...] = _gelu(y).astype(BF16)

    tok = pl.BlockSpec((ts, D_SSM), lambda i: (i, 0))
    row = _full((1, D_STATE))
    mat = _full((D_SSM, D_STATE))
    return pl.pallas_call(
        body, name="ssm_fwd", grid=(nt,),
        out_shape=[jax.ShapeDtypeStruct((s, D_SSM), BF16), jax.ShapeDtypeStruct((s, D_SSM), F32),
                   jax.ShapeDtypeStruct((nt, 2, D_STATE), F32)],
        in_specs=[tok, row, row, mat, mat, mat, mat, _full((1, D_SSM))],
        out_specs=[tok, tok, pl.BlockSpec((1, 2, D_STATE), lambda i: (i, 0, 0))],
        scratch_shapes=[pltpu.VMEM((ts, D_STATE), F32)] * 4 + [pltpu.VMEM((1, D_STATE), F32)] * 2,
        compiler_params=_cparams(1),
    )(u, ab_re, ab_im, bb_re, bb_im, cc_re, cc_im, d_skip)


def _attn_tile(s):
    return min(256, s)


def _sb_block(z, valid):
    lp = jnp.log2(1.0 + jnp.exp2(-jnp.abs(z)))
    ls = jnp.minimum(z, 0.0) - lp
    l1m = ls - z
    if valid is not None:
        l1m = jnp.where(valid, l1m, 0.0)
    return ls, l1m


def _attn_fwd(q, k, v):
    s = q.shape[0]
    t = _attn_tile(s)
    nq = s // t
    assert nq <= LANES

    def body(q_ref, k_ref, v_ref, o_ref, tab_ref, z_s, ls_s, lg_s, q_s, tri_s, acc_s):
        i = pl.program_id(1)
        lane = lax.broadcasted_iota(jnp.int32, (t, LANES), 1)
        first = lane < HEAD_DIM
        row = lax.broadcasted_iota(jnp.int32, (t, t), 0)
        col = lax.broadcasted_iota(jnp.int32, (t, t), 1)
        tri_s[...] = (row > col).astype(BF16)
        causal = col < row
        q2 = q_ref[...]
        zero = jnp.zeros_like(q2)
        q_s[0] = jnp.where(first, q2, zero)
        q_s[1] = jnp.where(first, zero, q2)
        acc_s[...] = jnp.zeros_like(acc_s)
        tab_ref[...] = jnp.full_like(tab_ref, -1e30)

        def scores(j):
            k2 = k_ref[pl.ds(pl.multiple_of(jnp.maximum(j, 0) * t, t), t), :]
            for a in range(2):
                z_s[a] = _dot_nt(q_s[a], k2)

        def logits(j, carries, valid):
            out = []
            for a in range(2):
                c1 = carries[a]
                ls, l1m = _sb_block(z_s[a], valid)
                ls_s[a] = ls + c1
                suffix = _dot(l1m, tri_s[...])
                lg_s[a] = ls_s[a] + suffix
                cols = slice(a * LANES, (a + 1) * LANES)
                tab_ref[:, cols] = jnp.where(lane == j, c1, tab_ref[:, cols])
                out.append(c1 + suffix[:, 0:1] + l1m[:, 0:1])
            return tuple(out)

        def accumulate(j, valid):
            v2 = v_ref[pl.ds(pl.multiple_of(jnp.maximum(j, 0) * t, t), t), :]
            for a in range(2):
                w = jnp.exp2(lg_s[a])
                if valid is not None:
                    w = jnp.where(valid, w, 0.0)
                acc_s[a] += _dot(w, v2)

        carries = tuple(jnp.zeros((t, 1), F32) for _ in range(2))
        scores(i)
        carries = logits(i, carries, causal)
        scores(i - 1)
        accumulate(i, causal)
        carries = logits(i - 1, carries, None)
        scores(i - 2)

        def trip(n, carries):
            accumulate(i - n + 2, None)
            carries = logits(i - n + 1, carries, None)
            scores(i - n)
            return carries

        def alive(c):
            n, carries = c
            return (n <= i + 1) & (jnp.max(jnp.maximum(carries[0], carries[1])) >= DEAD_LOG2)

        n_end, _ = lax.while_loop(alive, lambda c: (c[0] + 1, trip(c[0], c[1])), (jnp.int32(3), carries))

        @pl.when(i - n_end + 2 >= 0)
        def _():
            accumulate(i - n_end + 2, None)

        o_ref[...] = jnp.where(first, acc_s[0], acc_s[1]).astype(BF16)

    return pl.pallas_call(
        body, name="attn_fwd", grid=(D_SB // LANES, nq),
        out_shape=[jax.ShapeDtypeStruct((s, D_SB), BF16), jax.ShapeDtypeStruct((s, 2 * D_SB), F32)],
        in_specs=[pl.BlockSpec((t, LANES), lambda p, i: (i, p)), pl.BlockSpec((s, LANES), lambda p, i: (0, p)),
                  pl.BlockSpec((s, LANES), lambda p, i: (0, p))],
        out_specs=[pl.BlockSpec((t, LANES), lambda p, i: (i, p)), pl.BlockSpec((t, 2 * LANES), lambda p, i: (i, p))],
        scratch_shapes=[pltpu.VMEM((2, t, t), F32)] * 3 + [pltpu.VMEM((2, t, LANES), BF16), pltpu.VMEM((t, t), BF16),
                                                           pltpu.VMEM((2, t, LANES), F32)],
        compiler_params=_cparams(2),
    )(q, k, v)


def _conv_taps(z, halo, rowi):
    z1 = jnp.where(rowi == 0, halo[7:8, :], pltpu.roll(z, 1, 0))
    z2 = jnp.where(rowi == 0, halo[6:7, :], jnp.where(rowi == 1, halo[7:8, :], pltpu.roll(z, 2, 0)))
    return z1, z2


def _merge_branches(x_ref, y_ref, o_ref, c_ref, ch_ref, n1_ref, wg_ref, bg_ref, wv_ref, wsg_ref, wao_ref,
                    cw_ref, wc_ref, tb):
    i = pl.program_id(0)
    h = _rms_fwd(x_ref[...], n1_ref[...])
    g = _sigmoid(_dot(h, wg_ref[...]) + bg_ref[...])
    y256 = y_ref[...]
    val = _dot(y256, wv_ref[...])
    sg = _sigmoid(_dot(y256, wsg_ref[...]))
    ysb = _dot(o_ref[...], wao_ref[...])
    c3 = c_ref[...]
    cb, cc, cx = c3[:, 0:256], c3[:, 256:512], c3[:, 512:768]
    ch = ch_ref[...]
    z = cc * cx
    zh = jnp.where(i > 0, ch[:, 256:512] * ch[:, 512:768], 0.0)
    rowi = lax.broadcasted_iota(jnp.int32, (tb, D_CONV), 0)
    z1, z2 = _conv_taps(z, zh, rowi)
    cw = cw_ref[...]
    cv = cw[0:1, :] * z2 + cw[1:2, :] * z1 + cw[2:3, :] * z
    yc = _dot(cb * cv, wc_ref[...])
    return dict(h=h, g=g, val=val, sg=sg, ysb=ysb, cb=cb, cv=cv, yc=yc)


def _merge_fwd(x, y256, o, c3, w):
    s = x.shape[0]
    tb = _token_tile(s)

    def body(x_ref, y_ref, o_ref, c_ref, ch_ref, n1_ref, wg_ref, bg_ref, wv_ref, wsg_ref, wao_ref, cw_ref, wc_ref,
             wm_ref, n2_ref, x1_ref, m2_ref, mg_ref):
        br = _merge_branches(x_ref, y_ref, o_ref, c_ref, ch_ref, n1_ref, wg_ref, bg_ref, wv_ref, wsg_ref, wao_ref,
                             cw_ref, wc_ref, tb)
        g = br["g"]
        merged = (g[:, 0:1024] * (br["val"] * br["sg"]) + g[:, 1024:2048] * br["ysb"] + g[:, 2048:3072] * br["yc"])
        mg_ref[...] = merged.astype(BF16)
        m2 = _dot(merged, wm_ref[...])
        m2_ref[...] = m2
        x1_ref[...] = x_ref[...] + _rms_fwd(m2, n2_ref[...])

    def tok(wd):
        return pl.BlockSpec((tb, wd), lambda i: (i, 0))

    halo = pl.BlockSpec((8, 3 * D_CONV), lambda i: (jnp.maximum(i * (tb // 8) - 1, 0), 0))
    return pl.pallas_call(
        body, name="merge_fwd", grid=(s // tb,),
        out_shape=[jax.ShapeDtypeStruct((s, D_MODEL), F32), jax.ShapeDtypeStruct((s, D_MODEL), F32),
                   jax.ShapeDtypeStruct((s, D_MODEL), BF16)],
        in_specs=[tok(D_MODEL), tok(D_SSM), tok(D_SB), tok(3 * D_CONV), halo,
                  _full((1, D_MODEL)), _full((D_MODEL, 3 * D_MODEL)), _full((1, 3 * D_MODEL)),
                  _full((D_SSM, D_MODEL)), _full((D_SSM, D_MODEL)), _full((D_SB, D_MODEL)),
                  _full((8, D_CONV)), _full((D_CONV, D_MODEL)), _full((D_MODEL, D_MODEL)), _full((1, D_MODEL))],
        out_specs=[tok(D_MODEL), tok(D_MODEL), tok(D_MODEL)],
        compiler_params=_cparams(1),
    )(x, y256, o, c3, c3, w["n1"], w["w_gate"], w["b_gate"], w["w_glu_val"], w["w_glu_gate"], w["w_attn_out"],
      w["conv_w"], w["w_conv_out"], w["w_mix_out"], w["n2"])


FF_CHUNK = 1024


def _ffn_fwd(x1, n3, w_up, w_dn, n4):
    s = x1.shape[0]
    tb = min(512, s)
    nh = D_FF // FF_CHUNK

    def body(x_ref, n3_ref, wu_ref, wd_ref, n4_ref, x2_ref, f_ref, h_s, acc):
        j = pl.program_id(1)

        @pl.when(j == 0)
        def _():
            h_s[...] = _rms_fwd(x_ref[...], n3_ref[...]).astype(BF16)
            acc[...] = jnp.zeros_like(acc)

        a = jnp.maximum(_dot(h_s[...], wu_ref[...]), 0.0)
        acc[...] += _dot(a * a, wd_ref[...])

        @pl.when(j == nh - 1)
        def _():
            f = acc[...]
            f_ref[...] = f
            x2_ref[...] = x_ref[...] + _rms_fwd(f, n4_ref[...])

    tok = pl.BlockSpec((tb, D_MODEL), lambda i, j: (i, 0))
    return pl.pallas_call(
        body, name="ffn_fwd", grid=(s // tb, nh),
        out_shape=[jax.ShapeDtypeStruct((s, D_MODEL), F32)] * 2,
        in_specs=[tok, _full((1, D_MODEL)), pl.BlockSpec((D_MODEL, FF_CHUNK), lambda i, j: (0, j)),
                  pl.BlockSpec((FF_CHUNK, D_MODEL), lambda i, j: (j, 0)), _full((1, D_MODEL))],
        out_specs=[tok, tok],
        scratch_shapes=[pltpu.VMEM((tb, D_MODEL), BF16), pltpu.VMEM((tb, D_MODEL), F32)],
        compiler_params=_cparams(2),
    )(x1, n3, w_up, w_dn, n4)


def _loss_head(y, target):
    s = y.shape[0]
    tb = _token_tile(s)

    def body(y_ref, t_ref, dy_ref, l_ref):
        @pl.when(pl.program_id(0) == 0)
        def _():
            l_ref[...] = jnp.zeros_like(l_ref)

        err = y_ref[...] - t_ref[...]
        dy_ref[...] = err * (1.0 / D_MODEL)
        l_ref[...] += 0.5 * jnp.sum(jnp.mean(err * err, axis=-1, keepdims=True), axis=0, keepdims=True)

    tok = pl.BlockSpec((tb, D_MODEL), lambda i: (i, 0))
    return pl.pallas_call(
        body, name="loss_head", grid=(s // tb,),
        out_shape=[jax.ShapeDtypeStruct((s, D_MODEL), F32), jax.ShapeDtypeStruct((8, LANES), F32)],
        in_specs=[tok, tok], out_specs=[tok, _full((8, LANES))],
        compiler_params=_cparams(1),
    )(y, target)


def _ffn_bwd(dx2, x1, f, n3, w_up, w_dn, n4):
    s = x1.shape[0]
    tb = min(512, s)
    nh = D_FF // FF_CHUNK
    nt = s // tb

    def body(dx2_ref, x_ref, f_ref, n3_ref, wu_ref, wd_ref, n4_ref,
             dx1_ref, r_ref, da_ref, h_ref, df_ref, dn3_ref, dn4_ref, acc):
        i = pl.program_id(0)
        j = pl.program_id(1)

        @pl.when((i == 0) & (j == 0))
        def _():
            dn3_ref[...] = jnp.zeros_like(dn3_ref)
            dn4_ref[...] = jnp.zeros_like(dn4_ref)

        @pl.when(j == 0)
        def _():
            h_ref[...] = _rms_fwd(x_ref[...], n3_ref[...]).astype(BF16)
            df, dn4 = _rms_bwd(dx2_ref[...], f_ref[...], n4_ref[...])
            df_ref[...] = df.astype(BF16)
            dn4_ref[...] += dn4
            acc[...] = jnp.zeros_like(acc)

        a = jnp.maximum(_dot(h_ref[...], wu_ref[...]), 0.0)
        r_ref[...] = (a * a).astype(BF16)
        da = (_dot_nt(df_ref[...], wd_ref[...]) * (2.0 * a)).astype(BF16)
        da_ref[...] = da
        acc[...] += _dot_nt(da, wu_ref[...])

        @pl.when(j == nh - 1)
        def _():
            dx, dn3 = _rms_bwd(acc[...], x_ref[...], n3_ref[...])
            dx1_ref[...] = dx2_ref[...] + dx
            dn3_ref[...] += dn3

    tok = pl.BlockSpec((tb, D_MODEL), lambda i, j: (i, 0))
    hid = pl.BlockSpec((tb, FF_CHUNK), lambda i, j: (i, j))
    tok_b = jax.ShapeDtypeStruct((s, D_MODEL), BF16)
    hid_b = jax.ShapeDtypeStruct((s, D_FF), BF16)
    row = jax.ShapeDtypeStruct((1, D_MODEL), F32)
    return pl.pallas_call(
        body, name="ffn_bwd", grid=(nt, nh),
        out_shape=[jax.ShapeDtypeStruct((s, D_MODEL), F32), hid_b, hid_b, tok_b, tok_b, row, row],
        in_specs=[tok, tok, tok, _full((1, D_MODEL)), pl.BlockSpec((D_MODEL, FF_CHUNK), lambda i, j: (0, j)),
                  pl.BlockSpec((FF_CHUNK, D_MODEL), lambda i, j: (j, 0)), _full((1, D_MODEL))],
        out_specs=[tok, hid, hid, tok, tok, _full((1, D_MODEL)), _full((1, D_MODEL))],
        scratch_shapes=[pltpu.VMEM((tb, D_MODEL), F32)],
        compiler_params=_cparams(2),
    )(dx2, x1, f, n3, w_up, w_dn, n4)


def _merge_bwd(dx1, x, m2, y256, o, c3, w):
    s = x.shape[0]
    tb = _token_tile(s)

    def body(dx1_ref, m2_ref, x_ref, y_ref, o_ref, c_ref, ch_ref, n1_ref, wg_ref, bg_ref, wv_ref, wsg_ref, wao_ref,
             cw_ref, wc_ref, wm_ref, n2_ref,
             dhg_ref, h_ref, dgz_ref, dvs_ref, dy_ref, do_ref, dyb_ref, dyc_ref, bcv_ref, dcv_ref, dcb_ref, dm2_ref,
             dbg_ref, dn2_ref):
        @pl.when(pl.program_id(0) == 0)
        def _():
            dbg_ref[...] = jnp.zeros_like(dbg_ref)
            dn2_ref[...] = jnp.zeros_like(dn2_ref)

        dm2, dn2 = _rms_bwd(dx1_ref[...], m2_ref[...], n2_ref[...])
        dn2_ref[...] += dn2
        dm2_ref[...] = dm2.astype(BF16)
        dmg = _dot_nt(dm2, wm_ref[...])
        br = _merge_branches(x_ref, y_ref, o_ref, c_ref, ch_ref, n1_ref, wg_ref, bg_ref, wv_ref, wsg_ref, wao_ref,
                             cw_ref, wc_ref, tb)
        g = br["g"]
        g1, g2, g3 = g[:, 0:1024], g[:, 1024:2048], g[:, 2048:3072]
        val, sg = br["val"], br["sg"]
        h_ref[...] = br["h"].astype(BF16)
        dgz = jnp.concatenate([dmg * (val * sg) * g1 * (1.0 - g1), dmg * br["ysb"] * g2 * (1.0 - g2),
                               dmg * br["yc"] * g3 * (1.0 - g3)], axis=1)
        dbg_ref[...] += jnp.sum(dgz, axis=0, keepdims=True)
        dgz_ref[...] = dgz.astype(BF16)
        dhg_ref[...] = _dot_nt(dgz, wg_ref[...])
        dys = dmg * g1
        dval = dys * sg
        dsg = dys * val * sg * (1.0 - sg)
        dvs_ref[:, 0:1024] = dval.astype(BF16)
        dvs_ref[:, 1024:2048] = dsg.astype(BF16)
        dy_ref[...] = _dot_nt(dval, wv_ref[...]) + _dot_nt(dsg, wsg_ref[...])
        dyb = (dmg * g2).astype(BF16)
        dyb_ref[...] = dyb
        do_ref[...] = _dot_nt(dyb, wao_ref[...]).astype(BF16)
        dyc = (dmg * g3).astype(BF16)
        dyc_ref[...] = dyc
        dcq = _dot_nt(dyc, wc_ref[...])
        bcv_ref[...] = (br["cb"] * br["cv"]).astype(BF16)
        dcb_ref[...] = dcq * br["cv"]
        dcv_ref[...] = dcq * br["cb"]

    def tok(wd):
        return pl.BlockSpec((tb, wd), lambda i: (i, 0))

    def out(wd, dt):
        return jax.ShapeDtypeStruct((s, wd), dt)

    halo = pl.BlockSpec((8, 3 * D_CONV), lambda i: (jnp.maximum(i * (tb // 8) - 1, 0), 0))
    return pl.pallas_call(
        body, name="merge_bwd", grid=(s // tb,),
        out_shape=[out(D_MODEL, F32), out(D_MODEL, BF16), out(3 * D_MODEL, BF16), out(2 * D_MODEL, BF16),
                   out(D_SSM, F32), out(D_SB, BF16), out(D_MODEL, BF16), out(D_MODEL, BF16), out(D_CONV, BF16),
                   out(D_CONV, F32), out(D_CONV, F32), out(D_MODEL, BF16),
                   jax.ShapeDtypeStruct((1, 3 * D_MODEL), F32), jax.ShapeDtypeStruct((1, D_MODEL), F32)],
        in_specs=[tok(D_MODEL), tok(D_MODEL), tok(D_MODEL), tok(D_SSM), tok(D_SB), tok(3 * D_CONV), halo,
                  _full((1, D_MODEL)), _full((D_MODEL, 3 * D_MODEL)), _full((1, 3 * D_MODEL)),
                  _full((D_SSM, D_MODEL)), _full((D_SSM, D_MODEL)), _full((D_SB, D_MODEL)),
                  _full((8, D_CONV)), _full((D_CONV, D_MODEL)), _full((D_MODEL, D_MODEL)), _full((1, D_MODEL))],
        out_specs=[tok(D_MODEL), tok(D_MODEL), tok(3 * D_MODEL), tok(2 * D_MODEL), tok(D_SSM), tok(D_SB),
                   tok(D_MODEL), tok(D_MODEL), tok(D_CONV), tok(D_CONV), tok(D_CONV), tok(D_MODEL),
                   _full((1, 3 * D_MODEL)), _full((1, D_MODEL))],
        compiler_params=_cparams(1),
    )(dx1, m2, x, y256, o, c3, c3, w["n1"], w["w_gate"], w["b_gate"], w["w_glu_val"], w["w_glu_gate"],
      w["w_attn_out"], w["conv_w"], w["w_conv_out"], w["w_mix_out"], w["n2"])


def _ssm_bwd(u, ypre, dy, states, ab_re, ab_im, bb_re, bb_im, cc_re, cc_im, d_skip):
    s = u.shape[0]
    ts = _token_tile(s)
    nt = s // ts

    def body(u_ref, yp_ref, dy_ref, st_ref, ar_ref, ai_ref, bbr_ref, bbi_ref, ccr_ref, cci_ref, d_ref,
             du_ref, dar_ref, dai_ref, dbbr_ref, dbbi_ref, dccr_ref, dcci_ref, dd_ref,
             bur, bui, hr_s, hi_s, pr_s, pi_s, lr_s, li_s, cr, ci):
        @pl.when(pl.program_id(0) == 0)
        def _():
            cr[...] = jnp.zeros_like(cr)
            ci[...] = jnp.zeros_like(ci)
            for ref in (dar_ref, dai_ref, dbbr_ref, dbbi_ref, dccr_ref, dcci_ref, dd_ref):
                ref[...] = jnp.zeros_like(ref)

        u = u_ref[...]
        ub = u.astype(BF16)
        bur[...] = _dot(ub, bbr_ref[...])
        bui[...] = _dot(ub, bbi_ref[...])
        ar = ar_ref[...]
        ai = ai_ref[...]

        def fwd_step(t, carry):
            hr, hi = carry
            pr_s[pl.ds(t, 1), :] = hr
            pi_s[pl.ds(t, 1), :] = hi
            nr = ar * hr - ai * hi + bur[pl.ds(t, 1), :]
            ni = ar * hi + ai * hr + bui[pl.ds(t, 1), :]
            hr_s[pl.ds(t, 1), :] = nr
            hi_s[pl.ds(t, 1), :] = ni
            return nr, ni

        lax.fori_loop(0, ts, fwd_step, (st_ref[0, 0:1, :], st_ref[0, 1:2, :]), unroll=8)

        dyp = dy_ref[...] * _gelu_grad(yp_ref[...])
        dypb = dyp.astype(BF16)
        lr_s[...] = _dot(dypb, ccr_ref[...])
        li_s[...] = -_dot(dypb, cci_ref[...])

        def bwd_step(tt, carry):
            t = ts - 1 - tt
            nr, ni = carry
            qr = lr_s[pl.ds(t, 1), :] + ar * nr + ai * ni
            qi = li_s[pl.ds(t, 1), :] + ar * ni - ai * nr
            lr_s[pl.ds(t, 1), :] = qr
            li_s[pl.ds(t, 1), :] = qi
            return qr, qi

        nr, ni = lax.fori_loop(0, ts, bwd_step, (cr[...], ci[...]), unroll=8)
        cr[...] = nr
        ci[...] = ni
        lam_r = lr_s[...]
        lam_i = li_s[...]
        pr = pr_s[...]
        pi = pi_s[...]
        dar_ref[...] += jnp.sum(lam_r * pr + lam_i * pi, axis=0, keepdims=True)
        dai_ref[...] += jnp.sum(lam_i * pr - lam_r * pi, axis=0, keepdims=True)
        lrb = lam_r.astype(BF16)
        lib = lam_i.astype(BF16)
        du_ref[...] = _dot_nt(lrb, bbr_ref[...]) + _dot_nt(lib, bbi_ref[...]) + d_ref[...] * dyp
        dbbr_ref[...] += _dot_tn(ub, lrb)
        dbbi_ref[...] += _dot_tn(ub, lib)
        dccr_ref[...] += _dot_tn(dypb, hr_s[...])
        dcci_ref[...] -= _dot_tn(dypb, hi_s[...])
        dd_ref[...] += jnp.sum(dyp * u, axis=0, keepdims=True)

    tok = pl.BlockSpec((ts, D_SSM), lambda i: (nt - 1 - i, 0))
    row = _full((1, D_STATE))
    mat = _full((D_SSM, D_STATE))
    row_o = jax.ShapeDtypeStruct((1, D_STATE), F32)
    mat_o = jax.ShapeDtypeStruct((D_SSM, D_STATE), F32)
    return pl.pallas_call(
        body, name="ssm_bwd", grid=(nt,),
        out_shape=[jax.ShapeDtypeStruct((s, D_SSM), F32), row_o, row_o, mat_o, mat_o, mat_o, mat_o,
                   jax.ShapeDtypeStruct((1, D_SSM), F32)],
        in_specs=[tok, tok, tok, pl.BlockSpec((1, 2, D_STATE), lambda i: (nt - 1 - i, 0, 0)),
                  row, row, mat, mat, mat, mat, _full((1, D_SSM))],
        out_specs=[tok, row, row, mat, mat, mat, mat, _full((1, D_SSM))],
        scratch_shapes=[pltpu.VMEM((ts, D_STATE), F32)] * 8 + [pltpu.VMEM((1, D_STATE), F32)] * 2,
        compiler_params=_cparams(1),
    )(u, ypre, dy, states, ab_re, ab_im, bb_re, bb_im, cc_re, cc_im, d_skip)


def _attn_bwd(q, k, v, do, tab):
    s = q.shape[0]
    t = _attn_tile(s)
    nq = s // t
    n_pairs = D_SB // LANES

    def body(q_ref, do_ref, tab_ref, k_ref, v_ref, dq_ref, dk_hbm, dv_hbm, dk_s, dv_s, sems,
             z_s, ls_s, suf_s, dw_s, dl_s, beta_s, pre_s, wb_s, dq_s, q_s, do_s, qt_s, dot_s, tri_s):
        p = pl.program_id(0)
        i = pl.program_id(1)

        @pl.when(i == 0)
        def _():
            dk_s[...] = jnp.zeros_like(dk_s)
            dv_s[...] = jnp.zeros_like(dv_s)

        lane = lax.broadcasted_iota(jnp.int32, (t, LANES), 1)
        first = lane < HEAD_DIM
        row = lax.broadcasted_iota(jnp.int32, (t, t), 0)
        col = lax.broadcasted_iota(jnp.int32, (t, t), 1)
        tri_s[0] = (row > col).astype(BF16)
        tri_s[1] = (row < col).astype(BF16)
        causal = col < row
        q2 = q_ref[...]
        do2 = do_ref[...]
        zero = jnp.zeros_like(q2)
        for a in range(2):
            mine = first if a == 0 else jnp.logical_not(first)
            q_s[a] = jnp.where(mine, q2, zero)
            do_s[a] = jnp.where(mine, do2, zero)
            qt_s[a] = jnp.where(mine, q2, zero).astype(F32).T.astype(BF16)
            dot_s[a] = jnp.where(mine, do2, zero).astype(F32).T.astype(BF16)
        dq_s[...] = jnp.zeros_like(dq_s)

        def tile_rows(j):
            return pl.ds(pl.multiple_of(jnp.maximum(j, 0) * t, t), t)

        def scores(j, b=0):
            k2 = k_ref[tile_rows(j), :]
            for a in range(2):
                z_s[b, a] = _dot_nt(q_s[a], k2)

        def spread(j, valid, b=0):
            v2 = v_ref[tile_rows(j), :]
            for a in range(2):
                ls, l1m = _sb_block(z_s[b, a], valid)
                ls_s[b, a] = ls
                suf_s[b, a] = _dot(l1m, tri_s[0])
                dw_s[b, a] = _dot_nt(do_s[a], v2)

        def weights(j, valid, b=0):
            for a in range(2):
                c1 = jnp.sum(jnp.where(lane == j, tab_ref[:, a * LANES:(a + 1) * LANES], 0.0), axis=1, keepdims=True)
                ls = ls_s[b, a]
                w = jnp.exp2(ls + suf_s[b, a] + c1)
                if valid is not None:
                    w = jnp.where(valid, w, 0.0)
                dl = w * dw_s[b, a]
                dl_s[b, a] = dl
                beta_s[b, a] = jnp.exp2(ls)
                wb_s[b, a] = w.astype(BF16)
                pre_s[b, a] = _dot(dl, tri_s[1])

        def grads(j, c2s, valid, b=0):
            rows = tile_rows(j)
            k2 = k_ref[rows, :]
            out = []
            dkt = dvt = None
            for a in range(2):
                dl = dl_s[b, a]
                pre = pre_s[b, a]
                dz = dl - beta_s[b, a] * (dl + pre + c2s[a])
                if valid is not None:
                    dz = jnp.where(valid, dz, 0.0)
                dzb = dz.astype(BF16)
                dq_s[a] += _dot(dzb, k2)
                dk_a = jnp.dot(qt_s[a], dzb, preferred_element_type=F32)
                dv_a = jnp.dot(dot_s[a], wb_s[b, a], preferred_element_type=F32)
                dkt = dk_a if dkt is None else dkt + dk_a
                dvt = dv_a if dvt is None else dvt + dv_a
                out.append(c2s[a] + pre[:, t - 1:t] + dl[:, t - 1:t])
            dk_s[:, rows] += dkt
            dv_s[:, rows] += dvt
            return tuple(out)

        reach = jnp.max(jnp.maximum(tab_ref[:, 0:LANES], tab_ref[:, LANES:2 * LANES]), axis=0, keepdims=True)
        tile_id = lax.broadcasted_iota(jnp.int32, (1, LANES), 1)
        j_min = jnp.min(jnp.where(reach >= DEAD_LOG2, tile_id, i))
        no_sum = tuple(jnp.zeros((t, 1), F32) for _ in range(2))
        two_tiles = (i >= 1) & (j_min >= i - 1)

        @pl.when(two_tiles)
        def _():
            scores(i - 1, 0)
            scores(i, 1)
            spread(i - 1, None, 0)
            spread(i, causal, 1)
            weights(i - 1, None, 0)
            weights(i, causal, 1)
            grads(i, grads(i - 1, no_sum, None, 0), causal, 1)

        @pl.when(jnp.logical_not(two_tiles))
        def _():
            z_s[0] = jnp.full((2, t, t), -1e30, F32)
            ls_s[0] = jnp.full((2, t, t), -1e30, F32)
            for ref in (suf_s, dw_s, dl_s, beta_s, pre_s, wb_s):
                ref[0] = jnp.zeros((2, t, t), ref.dtype)

            def trip(n, c2s):
                c2s = grads(n - 3, c2s, None)
                weights(n - 2, None)
                spread(n - 1, None)
                scores(n)
                return c2s

            c2s = lax.fori_loop(j_min, i + 1, trip, no_sum)
            c2s = grads(i - 2, c2s, None)
            weights(i - 1, None)
            spread(i, causal)
            c2s = grads(i - 1, c2s, None)
            weights(i, causal)
            grads(i, c2s, causal)

        dq_ref[...] = Q_SCALE * jnp.where(first, dq_s[0], dq_s[1])

        @pl.when(i == nq - 1)
        def _():
            dk_s[...] = dk_s[...] * (1.0 / LOG2E)
            ck = pltpu.make_async_copy(dk_s, dk_hbm.at[p], sems.at[0])
            cv = pltpu.make_async_copy(dv_s, dv_hbm.at[p], sems.at[1])
            ck.start()
            cv.start()
            ck.wait()
            cv.wait()

    blk = pl.BlockSpec((t, LANES), lambda p, i: (i, p))
    seq = pl.BlockSpec((s, LANES), lambda p, i: (0, p))
    pairs = jax.ShapeDtypeStruct((n_pairs, LANES, s), F32)
    stage = pltpu.VMEM((2, 2, t, t), F32)
    return pl.pallas_call(
        body, name="attn_bwd", grid=(n_pairs, nq),
        out_shape=[jax.ShapeDtypeStruct((s, D_SB), F32), pairs, pairs],
        in_specs=[blk, blk, pl.BlockSpec((t, 2 * LANES), lambda p, i: (i, p)), seq, seq],
        out_specs=[blk, pl.BlockSpec(memory_space=pl.ANY), pl.BlockSpec(memory_space=pl.ANY)],
        scratch_shapes=[pltpu.VMEM((LANES, s), F32), pltpu.VMEM((LANES, s), F32), pltpu.SemaphoreType.DMA((2,))]
        + [stage] * 7 + [pltpu.VMEM((2, 2, t, t), BF16), pltpu.VMEM((2, t, LANES), F32),
                         pltpu.VMEM((2, t, LANES), BF16), pltpu.VMEM((2, t, LANES), BF16),
                         pltpu.VMEM((2, LANES, t), BF16), pltpu.VMEM((2, LANES, t), BF16),
                         pltpu.VMEM((2, t, t), BF16)],
        compiler_params=_cparams(2),
    )(q, do, tab, k, v)


def _pre_bwd(dres, dhg, x, du, dq, dk, dv, dcb, dcv, c3, n1, w_in, conv_w):
    s = x.shape[0]
    tb = _token_tile(s)
    nt = s // tb

    def body(dres_ref, dhg_ref, x_ref, du_ref, dq_ref, dk_ref, dv_ref, dcb_ref, dcv_ref, dnext_ref, c_ref, ch_ref,
             n1_ref, w_ref, cw_ref, dx_ref, dp_ref, dn1_ref, dcw_ref):
        i = pl.program_id(0)

        @pl.when(i == 0)
        def _():
            dn1_ref[...] = jnp.zeros_like(dn1_ref)
            dcw_ref[...] = jnp.zeros_like(dcw_ref)

        c3 = c_ref[...]
        cc, cx = c3[:, 256:512], c3[:, 512:768]
        ch = ch_ref[...]
        z = cc * cx
        zh = jnp.where(i > 0, ch[:, 256:512] * ch[:, 512:768], 0.0)
        rowi = lax.broadcasted_iota(jnp.int32, (tb, D_CONV), 0)
        z1, z2 = _conv_taps(z, zh, rowi)
        dcv = dcv_ref[...]
        nxt = jnp.where(i < nt - 1, dnext_ref[...], 0.0)
        d1 = jnp.where(rowi == tb - 1, nxt[0:1, :], pltpu.roll(dcv, tb - 1, 0))
        d2 = jnp.where(rowi == tb - 2, nxt[0:1, :], jnp.where(rowi == tb - 1, nxt[1:2, :], pltpu.roll(dcv, tb - 2, 0)))
        cw = cw_ref[...]
        dz = cw[2:3, :] * dcv + cw[1:2, :] * d1 + cw[0:1, :] * d2
        dcw_ref[0:1, :] += jnp.sum(dcv * z2, axis=0, keepdims=True)
        dcw_ref[1:2, :] += jnp.sum(dcv * z1, axis=0, keepdims=True)
        dcw_ref[2:3, :] += jnp.sum(dcv * z, axis=0, keepdims=True)
        dp_ref[:, 0:256] = du_ref[...].astype(BF16)
        dp_ref[:, 256:768] = dq_ref[...].astype(BF16)
        for pair in range(D_SB // LANES):
            dp_ref[:, 768 + pair * LANES:768 + (pair + 1) * LANES] = dk_ref[pair].T.astype(BF16)
            dp_ref[:, 1280 + pair * LANES:1280 + (pair + 1) * LANES] = dv_ref[pair].T.astype(BF16)
        dp_ref[:, 1792:2048] = dcb_ref[...].astype(BF16)
        dp_ref[:, 2048:2304] = (dz * cx).astype(BF16)
        dp_ref[:, 2304:2560] = (dz * cc).astype(BF16)
        dh = dhg_ref[...] + _dot_nt(dp_ref[...], w_ref[...])
        dx, dn1 = _rms_bwd(dh, x_ref[...], n1_ref[...])
        dx_ref[...] = dres_ref[...] + dx
        dn1_ref[...] += dn1

    def tok(wd):
        return pl.BlockSpec((tb, wd), lambda i: (i, 0))

    halo_prev = pl.BlockSpec((8, 3 * D_CONV), lambda i: (jnp.maximum(i * (tb // 8) - 1, 0), 0))
    halo_next = pl.BlockSpec((8, D_CONV), lambda i: (jnp.minimum((i + 1) * (tb // 8), s // 8 - 1), 0))
    pairs_t = pl.BlockSpec((D_SB // LANES, LANES, tb), lambda i: (0, 0, i))
    return pl.pallas_call(
        body, name="pre_bwd", grid=(nt,),
        out_shape=[jax.ShapeDtypeStruct((s, D_MODEL), F32), jax.ShapeDtypeStruct((s, D_IN), BF16),
                   jax.ShapeDtypeStruct((1, D_MODEL), F32), jax.ShapeDtypeStruct((8, D_CONV), F32)],
        in_specs=[tok(D_MODEL), tok(D_MODEL), tok(D_MODEL), tok(D_SSM), tok(D_SB), pairs_t, pairs_t,
                  tok(D_CONV), tok(D_CONV), halo_next, tok(3 * D_CONV), halo_prev,
                  _full((1, D_MODEL)), _full((D_MODEL, D_IN)), _full((8, D_CONV))],
        out_specs=[tok(D_MODEL), tok(D_IN), _full((1, D_MODEL)), _full((8, D_CONV))],
        compiler_params=_cparams(1),
    )(dres, dhg, x, du, dq, dk, dv, dcb, dcv, dcv, c3, c3, n1, w_in, conv_w)


def _padded(n):
    return -(-n // PIECE_ALIGN) * PIECE_ALIGN


def _pack(pieces, total_rows=None):
    flat = []
    for a in pieces:
        a = a.reshape(-1)
        flat.append(jnp.pad(a, (0, _padded(a.size) - a.size)))
    out = jnp.concatenate(flat)
    if total_rows is not None:
        out = jnp.pad(out, (0, total_rows * LANES - out.size))
    return out.reshape(-1, LANES)


def _unpack(flat, shapes, lead=()):
    out, row = [], 0
    for shp in shapes:
        n = math.prod(shp)
        rows = _padded(n) // LANES
        piece = flat[..., row:row + rows, :].reshape(lead + (-1,))
        out.append(piece[..., :n].reshape(lead + tuple(shp)))
        row += rows
    return out


def _to_shards(full, axis):
    shp = full.shape
    cut = shp[:axis] + (N_DEV, shp[axis] // N_DEV) + shp[axis + 1:]
    return jnp.moveaxis(full.reshape(cut), axis, 0)


def _from_shards(shards, axis):
    moved = jnp.moveaxis(shards, 0, axis)
    shp = moved.shape
    return moved.reshape(shp[:axis] + (shp[axis] * shp[axis + 1],) + shp[axis + 2:])


def _expand_groups(compact):
    rows = lax.broadcasted_iota(jnp.int32, (D_SSM, D_STATE), 0) // GROUP_CH
    cols = lax.broadcasted_iota(jnp.int32, (D_SSM, D_STATE), 1) // N_STATE
    return jnp.where(rows == cols, jnp.tile(compact, (N_GROUPS, 1)), 0.0)


def _collect_groups(dense):
    rows = lax.broadcasted_iota(jnp.int32, (D_SSM, D_STATE), 0) // GROUP_CH
    cols = lax.broadcasted_iota(jnp.int32, (D_SSM, D_STATE), 1) // N_STATE
    return jnp.where(rows == cols, dense, 0.0).reshape(N_GROUPS, GROUP_CH, D_STATE).sum(axis=0)


def kernel(x, norm_mix_pre, norm_mix_post, w_in, w_gate, b_gate, ssm_a_re, ssm_a_im, ssm_log_dt, ssm_b_re, ssm_b_im, ssm_c_re, ssm_c_im, ssm_d, w_glu_val, w_glu_gate, w_attn_out, conv_w, w_conv_out, w_mix_out, norm_ffn_pre, norm_ffn_post, w_ffn_up, w_ffn_down, loss_target, m_norm_mix_pre, m_norm_mix_post, m_w_in, m_w_gate, m_b_gate, m_ssm_a_re, m_ssm_a_im, m_ssm_log_dt, m_ssm_b_re, m_ssm_b_im, m_ssm_c_re, m_ssm_c_im, m_ssm_d, m_w_glu_val, m_w_glu_gate, m_w_attn_out, m_conv_w, m_w_conv_out, m_w_mix_out, m_norm_ffn_pre, m_norm_ffn_post, m_w_ffn_up, m_w_ffn_down, v_norm_mix_pre, v_norm_mix_post, v_w_in, v_w_gate, v_b_gate, v_ssm_a_re, v_ssm_a_im, v_ssm_log_dt, v_ssm_b_re, v_ssm_b_im, v_ssm_c_re, v_ssm_c_im, v_ssm_d, v_w_glu_val, v_w_glu_gate, v_w_attn_out, v_conv_w, v_w_conv_out, v_w_mix_out, v_norm_ffn_pre, v_norm_ffn_post, v_w_ffn_up, v_w_ffn_down):
    args = dict(locals())
    wts = {n: args[n] for n in WEIGHTS}
    mom = {n: args["m_" + n] for n in WEIGHTS}
    vel = {n: args["v_" + n] for n in WEIGHTS}
    seq = x.shape[1]
    x0 = x.reshape(seq, D_MODEL)
    target = loss_target.reshape(seq, D_MODEL)

    pieces = []
    for n in SHARDED:
        if n == "conv_w":
            hi = wts[n].astype(BF16)
            pieces += [hi, (wts[n] - hi.astype(F32)).astype(BF16)]
        else:
            pieces.append(wts[n].astype(BF16))
    shard_shapes = [p.shape for p in pieces]
    gathered = _unpack(_all_gather(_pack(pieces)), shard_shapes, lead=(N_DEV,))
    full = {}
    it = iter(gathered)
    for n in SHARDED:
        if n == "conv_w":
            full[n] = _from_shards(next(it).astype(F32) + next(it).astype(F32), SHARD_AXIS[n])
        else:
            full[n] = _from_shards(next(it), SHARD_AXIS[n])

    def layer_weights(l):
        return dict(
            n1=norm_mix_pre[l][None], n2=norm_mix_post[l][None], n3=norm_ffn_pre[l][None], n4=norm_ffn_post[l][None],
            w_in=full["w_in"][l], w_gate=full["w_gate"][l], b_gate=b_gate[l][None],
            w_glu_val=full["w_glu_val"][l], w_glu_gate=full["w_glu_gate"][l], w_attn_out=full["w_attn_out"][l],
            conv_w=jnp.pad(full["conv_w"][l], ((0, 5), (0, 0))), w_conv_out=full["w_conv_out"][l],
            w_mix_out=full["w_mix_out"][l], w_ffn_up=full["w_ffn_up"][l], w_ffn_down=full["w_ffn_down"][l],
            lr=ssm_a_re[l].reshape(1, D_STATE), li=ssm_a_im[l].reshape(1, D_STATE),
            ldt=jnp.repeat(ssm_log_dt[l], N_STATE).reshape(1, D_STATE),
            br=jnp.transpose(ssm_b_re[l], (2, 0, 1)).reshape(GROUP_CH, D_STATE),
            bi=jnp.transpose(ssm_b_im[l], (2, 0, 1)).reshape(GROUP_CH, D_STATE),
            cc_re=_expand_groups(jnp.transpose(ssm_c_re[l], (1, 0, 2)).reshape(GROUP_CH, D_STATE)).astype(BF16),
            cc_im=_expand_groups(jnp.transpose(ssm_c_im[l], (1, 0, 2)).reshape(GROUP_CH, D_STATE)).astype(BF16),
            d_skip=ssm_d[l][None],
        )

    saved = []
    xin = x0
    for l in range(DEPTH):
        w = layer_weights(l)
        ab_re, ab_im, bbr, bbi = _ssm_params_fwd(w["lr"], w["li"], w["ldt"], w["br"], w["bi"])
        w.update(ab_re=ab_re, ab_im=ab_im, bb_re=_expand_groups(bbr).astype(BF16), bb_im=_expand_groups(bbi).astype(BF16))
        u, q, k, v, c3 = _pre_fwd(xin, w["n1"], w["w_in"])
        y256, ypre, states = _ssm_fwd(u, ab_re, ab_im, w["bb_re"], w["bb_im"], w["cc_re"], w["cc_im"], w["d_skip"])
        o, tab = _attn_fwd(q, k, v)
        x1, m2, merged = _merge_fwd(xin, y256, o, c3, w)
        x2, f = _ffn_fwd(x1, w["n3"], w["w_ffn_up"], w["w_ffn_down"], w["n4"])
        saved.append(dict(w=w, x0=xin, u=u, q=q, k=k, v=v, c3=c3, y256=y256, ypre=ypre, states=states, o=o, tab=tab, x1=x1,
                          m2=m2, merged=merged, f=f))
        xin = x2

    dxo, loss_part = _loss_head(xin, target)
    loss = lax.psum(loss_part[0, 0], ("x", "y", "c"))

    grads = {n: [None] * DEPTH for n in WEIGHTS}
    for l in reversed(range(DEPTH)):
        sv = saved[l]
        w = sv["w"]
        dx1, r, da, h2, df, dn3, dn4 = _ffn_bwd(dxo, sv["x1"], sv["f"], w["n3"], w["w_ffn_up"], w["w_ffn_down"], w["n4"])
        grads["w_ffn_down"][l] = _tn_matmul(r, df, "grad_w_ffn_down")
        grads["w_ffn_up"][l] = _tn_matmul(h2, da, "grad_w_ffn_up")
        grads["norm_ffn_pre"][l] = dn3[0]
        grads["norm_ffn_post"][l] = dn4[0]
        (dhg, hb, dgz, dvs, dy256, do, dyb, dyc, bcv, dcv, dcb, dm2, dbg, dn2) = _merge_bwd(
            dx1, sv["x0"], sv["m2"], sv["y256"], sv["o"], sv["c3"], w)
        grads["w_mix_out"][l] = _tn_matmul(sv["merged"], dm2, "grad_w_mix_out")
        grads["w_gate"][l] = _tn_matmul(hb, dgz, "grad_w_gate")
        dglu = _tn_matmul(sv["y256"], dvs, "grad_w_glu")
        grads["w_glu_val"][l] = dglu[:, :D_MODEL]
        grads["w_glu_gate"][l] = dglu[:, D_MODEL:]
        grads["w_attn_out"][l] = _tn_matmul(sv["o"], dyb, "grad_w_attn_out")
        grads["w_conv_out"][l] = _tn_matmul(bcv, dyc, "grad_w_conv_out")
        grads["b_gate"][l] = dbg[0]
        grads["norm_mix_post"][l] = dn2[0]
        du, dar, dai, dbbr, dbbi, dccr, dcci, dd = _ssm_bwd(
            sv["u"], sv["ypre"], dy256, sv["states"], w["ab_re"], w["ab_im"], w["bb_re"], w["bb_im"],
            w["cc_re"], w["cc_im"], w["d_skip"])
        glr, gli, gdt, gbr, gbi = _ssm_params_bwd(w["lr"], w["li"], w["ldt"], w["br"], w["bi"], dar, dai,
                                                   _collect_groups(dbbr), _collect_groups(dbbi))
        grads["ssm_a_re"][l] = glr.reshape(N_GROUPS, N_STATE)
        grads["ssm_a_im"][l] = gli.reshape(N_GROUPS, N_STATE)
        grads["ssm_log_dt"][l] = gdt[:, 0]
        grads["ssm_b_re"][l] = jnp.transpose(gbr.reshape(GROUP_CH, N_GROUPS, N_STATE), (1, 2, 0))
        grads["ssm_b_im"][l] = jnp.transpose(gbi.reshape(GROUP_CH, N_GROUPS, N_STATE), (1, 2, 0))
        grads["ssm_c_re"][l] = jnp.transpose(_collect_groups(dccr).reshape(GROUP_CH, N_GROUPS, N_STATE), (1, 0, 2))
        grads["ssm_c_im"][l] = jnp.transpose(_collect_groups(dcci).reshape(GROUP_CH, N_GROUPS, N_STATE), (1, 0, 2))
        grads["ssm_d"][l] = dd[0]
        dq, dk4, dv4 = _attn_bwd(sv["q"], sv["k"], sv["v"], do, sv["tab"])
        dxo, dp, dn1, dcw = _pre_bwd(dx1, dhg, sv["x0"], du, dq, dk4, dv4, dcb, dcv, sv["c3"], w["n1"], w["w_in"], w["conv_w"])
        grads["w_in"][l] = _tn_matmul(hb, dp, "grad_w_in")
        grads["norm_mix_pre"][l] = dn1[0]
        grads["conv_w"][l] = dcw[0:3]
    grad_x = dxo.reshape(x.shape)
    gfull = {n: jnp.stack(grads[n]) for n in WEIGHTS}

    rep_parts = _pack([gfull[n] for n in REPLICATED])
    sh_parts = jnp.stack([_pack([_to_shards(gfull[n], SHARD_AXIS[n])[d] for n in SHARDED]) for d in range(N_DEV)])
    rows = sh_parts.shape[1] + rep_parts.shape[0]
    total_rows = -(-rows // 512) * 512
    parts = jnp.concatenate([sh_parts, jnp.broadcast_to(rep_parts, (N_DEV,) + rep_parts.shape),
                             jnp.zeros((N_DEV, total_rows - rows, LANES), F32)], axis=1)
    recv = _all_to_all(parts)

    def mine(d):
        sh = _pack([d[n] for n in SHARDED])
        rp = _pack([d[n] for n in REPLICATED])
        return jnp.concatenate([sh, rp, jnp.zeros((total_rows - rows, LANES), F32)], axis=0)

    flat_out = _adamw(recv, mine(wts), mine(mom), mine(vel))
    n_sh_rows = sh_parts.shape[1]
    results = []
    for fo in flat_out:
        sh = _unpack(fo[:n_sh_rows], [wts[n].shape for n in SHARDED])
        rp = _unpack(fo[n_sh_rows:rows], [wts[n].shape for n in REPLICATED])
        by_name = dict(zip(SHARDED, sh))
        by_name.update(zip(REPLICATED, rp))
        results.append([by_name[n] for n in WEIGHTS])
    g_out, d_out, m_out, v_out = results
    return (loss, grad_x, *g_out, *d_out, *m_out, *v_out)
```
